```python
import math
import jax, jax.numpy as jnp
from jax import lax
import numpy as np

D_MODEL = 2048
BATCH = 4
SEQ = 2048
DEPTH = 4
DEC_BATCH = 32
DEC_SEQ = 16
PAST_LEN = 1024

CHUNK = 64
N_MIXERS = 2
N_GDN = (DEPTH + 1) // 2
N_FOX = DEPTH // 2

GDN_DK = 128
GDN_DV = 128
GDN_QK_HEADS = D_MODEL // 128
GDN_V_HEADS = 2 * GDN_QK_HEADS
GDN_QK_DIM = GDN_QK_HEADS * GDN_DK
GDN_V_DIM = GDN_V_HEADS * GDN_DV
GDN_CONV_DIM = 2 * GDN_QK_DIM + GDN_V_DIM
GDN_IN_DIM = GDN_CONV_DIM + GDN_V_DIM + 2 * GDN_V_HEADS
CONV_W = 4

FOX_HEADS = D_MODEL // 128
FOX_DH = 128
FOX_DIM = FOX_HEADS * FOX_DH
FOX_IN_DIM = 4 * FOX_DIM + FOX_HEADS
Q_BLOCK = 128
FORGET_BIAS_INIT = 2.0

D_FF = 4 * D_MODEL
EPS = 1e-6

kernel_name = "hybrid_gdn_fox_stream_step"


def rmsnorm(x, w):
    xf = x.astype(jnp.float32)
    y = xf * lax.rsqrt(jnp.mean(jnp.square(xf), axis=-1, keepdims=True) + EPS)
    return (y * w.astype(jnp.float32)).astype(x.dtype)


def l2norm(x):
    return x * lax.rsqrt(jnp.sum(jnp.square(x), axis=-1, keepdims=True) + EPS)


def sq_relu_mlp(h, w_up, w_down):
    return jnp.square(jax.nn.relu(h @ w_up)) @ w_down


def gated_delta_blocked(q, k, v, g, beta, S0, C):
    f32 = jnp.float32
    B, T, H, _ = q.shape
    N = T // C

    def blk(a):
        return a.astype(f32).reshape(B, N, C, H, -1).transpose(0, 3, 1, 2, 4)

    q, k, v = blk(q), blk(k), blk(v)
    g = g.astype(f32).reshape(B, N, C, H).transpose(0, 3, 1, 2)
    beta = beta.astype(f32).reshape(B, N, C, H).transpose(0, 3, 1, 2)
    G = jnp.cumsum(g, axis=-1)
    diff = G[..., :, None] - G[..., None, :]
    idx = jnp.arange(C)
    incl = idx[:, None] >= idx[None, :]
    strict = idx[:, None] > idx[None, :]
    decay_incl = jnp.exp(jnp.where(incl, diff, -jnp.inf))
    decay_strict = jnp.where(strict, decay_incl, 0.0)
    kb = k * beta[..., None]
    vb = v * beta[..., None]
    L = jnp.einsum('bhntd,bhnsd->bhnts', kb, k) * decay_strict
    IL = L + jnp.eye(C, dtype=f32)
    gamma = jnp.exp(G)
    Wm = lax.linalg.triangular_solve(IL, kb * gamma[..., None], left_side=True, lower=True, unit_diagonal=True)
    Uv = lax.linalg.triangular_solve(IL, vb, left_side=True, lower=True, unit_diagonal=True)
    attn = jnp.einsum('bhntd,bhnsd->bhnts', q, k) * decay_incl
    qg = q * gamma[..., None]
    kd = k * jnp.exp(G[..., -1:] - G)[..., None]
    gC = jnp.exp(G[..., -1])
    xs = tuple(jnp.moveaxis(a, 2, 0) for a in (Wm, Uv, attn, qg, kd, gC))

    def step(S, inp):
        Wc, Uc, Ac, Qc, Kc, gc = inp
        U = Uc - jnp.einsum('bhck,bhkv->bhcv', Wc, S)
        o = jnp.einsum('bhck,bhkv->bhcv', Qc, S) + jnp.einsum('bhcs,bhsv->bhcv', Ac, U)
        S = gc[..., None, None] * S + jnp.einsum('bhck,bhcv->bhkv', Kc, U)
        return S, o

    S, o = lax.scan(step, S0.astype(f32), xs)
    o = o.transpose(1, 0, 3, 2, 4).reshape(B, T, H, -1)
    return o, S


def gdn_mixer(h, conv_buf, S0, w_in, conv_w, a_log, dt_bias, norm_w, w_out, chunk):
    f32 = jnp.float32
    B, T, _ = h.shape
    proj = h @ w_in
    o1 = GDN_CONV_DIM
    o2 = o1 + GDN_V_DIM
    o3 = o2 + GDN_V_HEADS
    qkv, z, b, a = proj[..., :o1], proj[..., o1:o2], proj[..., o2:o3], proj[..., o3:]
    xc = jnp.concatenate([conv_buf.astype(qkv.dtype), qkv], axis=1)
    new_buf = xc[:, -(CONV_W - 1):]
    conv = xc[:, 0:T] * conv_w[0]
    for j in range(1, CONV_W):
        conv = conv + xc[:, j:j + T] * conv_w[j]
    conv = jax.nn.silu(conv)
    q = l2norm(conv[..., :GDN_QK_DIM].astype(f32).reshape(B, T, GDN_QK_HEADS, GDN_DK)) * (GDN_DK ** -0.5)
    k = l2norm(conv[..., GDN_QK_DIM:2 * GDN_QK_DIM].astype(f32).reshape(B, T, GDN_QK_HEADS, GDN_DK))
    v = conv[..., 2 * GDN_QK_DIM:].reshape(B, T, GDN_V_HEADS, GDN_DV)
    rep = GDN_V_HEADS // GDN_QK_HEADS
    q = jnp.repeat(q, rep, axis=2)
    k = jnp.repeat(k, rep, axis=2)
    beta = jax.nn.sigmoid(b.astype(f32))
    g = -jnp.exp(a_log.astype(f32)) * jax.nn.softplus(a.astype(f32) + dt_bias.astype(f32))
    o, S = gated_delta_blocked(q, k, v, g, beta, S0, chunk)
    zr = z.reshape(B, T, GDN_V_HEADS, GDN_DV).astype(f32)
    o = rmsnorm(o, norm_w) * jax.nn.silu(zr)
    out = o.reshape(B, T, GDN_V_DIM).astype(h.dtype) @ w_out
    return out, new_buf, S.astype(S0.dtype)


def fox_attend(q, k, v, cum_q, cum_k, q_pos, k_pos):
    f32 = jnp.float32
    s = jnp.einsum('bqhd,bkhd->bhqk', q.astype(f32), k.astype(f32)) * (FOX_DH ** -0.5)
    bias = cum_q.transpose(0, 2, 1)[..., :, None] - cum_k.transpose(0, 2, 1)[..., None, :]
    mask = k_pos[None, :] <= q_pos[:, None]
    p = jax.nn.softmax(jnp.where(mask, s + bias, -jnp.inf), axis=-1)
    return jnp.einsum('bhqk,bkhd->bqhd', p, v.astype(f32))


def fox_mixer(h, past, w_in, b_f, w_out):
    f32 = jnp.float32
    B, T, _ = h.shape
    proj = h @ w_in
    q = proj[..., :FOX_DIM].reshape(B, T, FOX_HEADS, FOX_DH)
    k = proj[..., FOX_DIM:2 * FOX_DIM].reshape(B, T, FOX_HEADS, FOX_DH)
    v = proj[..., 2 * FOX_DIM:3 * FOX_DIM].reshape(B, T, FOX_HEADS, FOX_DH)
    og = proj[..., 3 * FOX_DIM:4 * FOX_DIM]
    logf = jax.nn.log_sigmoid(proj[..., 4 * FOX_DIM:].astype(f32) + b_f.astype(f32))
    if past is None:
        cum = jnp.cumsum(logf, axis=1)
        nq = T // Q_BLOCK
        qb = q.reshape(B, nq, Q_BLOCK, FOX_HEADS, FOX_DH).transpose(1, 0, 2, 3, 4)
        cb = cum.reshape(B, nq, Q_BLOCK, FOX_HEADS).transpose(1, 0, 2, 3)
        k_pos = jnp.arange(T)

        def one_block(inp):
            qi, ci, n = inp
            return fox_attend(qi, k, v, ci, cum, n * Q_BLOCK + jnp.arange(Q_BLOCK), k_pos)

        o = lax.map(one_block, (qb, cb, jnp.arange(nq)))
        o = o.transpose(1, 0, 2, 3, 4).reshape(B, T, FOX_HEADS, FOX_DH)
    else:
        k_past, v_past, lf_past = past
        P = k_past.shape[1]
        k_all = jnp.concatenate([k_past.astype(k.dtype), k], axis=1)
        v_all = jnp.concatenate([v_past.astype(v.dtype), v], axis=1)
        cum = jnp.cumsum(jnp.concatenate([lf_past.astype(f32), logf], axis=1), axis=1)
        o = fox_attend(q, k_all, v_all, cum[:, P:], cum, P + jnp.arange(T), jnp.arange(P + T))
    o = o * jax.nn.sigmoid(og.astype(f32)).reshape(B, T, FOX_HEADS, FOX_DH)
    out = o.reshape(B, T, FOX_DIM).astype(h.dtype) @ w_out
    return out, k, v, logf.astype(h.dtype)


def trunk(x, gdn_S, gdn_buf, fox_past, gdn_chunk, norm_mix, norm_mlp, norm_final,
          gdn_w_in, gdn_conv_w, gdn_a_log, gdn_dt_bias, gdn_norm_w, gdn_w_out,
          fox_w_in, fox_b_f, fox_w_out, mlp_w_up, mlp_w_down):
    S_out, buf_out, k_out, v_out, lf_out = [], [], [], [], []
    for i in range(DEPTH):
        j = i // N_MIXERS
        hn = rmsnorm(x, norm_mix[i])
        if i % N_MIXERS == 0:
            mix, buf, S = gdn_mixer(hn, gdn_buf[j], gdn_S[j], gdn_w_in[j], gdn_conv_w[j], gdn_a_log[j],
                                    gdn_dt_bias[j], gdn_norm_w[j], gdn_w_out[j], gdn_chunk)
            S_out.append(S)
            buf_out.append(buf)
        else:
            past = None if fox_past is None else (fox_past[0][j], fox_past[1][j], fox_past[2][j])
            mix, k, v, lf = fox_mixer(hn, past, fox_w_in[j], fox_b_f[j], fox_w_out[j])
            k_out.append(k)
            v_out.append(v)
            lf_out.append(lf)
        x = x + mix
        x = x + sq_relu_mlp(rmsnorm(x, norm_mlp[i]), mlp_w_up[i], mlp_w_down[i])
    y = rmsnorm(x, norm_final)
    return y, jnp.stack(S_out), jnp.stack(buf_out), jnp.stack(k_out), jnp.stack(v_out), jnp.stack(lf_out)


def setup_inputs(seed: int = 0) -> dict:
    key = jax.random.key(seed)
    ks = jax.random.split(key, 24)
    f32 = jnp.float32

    def nrm(k, shape, s):
        return jax.random.normal(k, shape, f32) * s

    x_prompt = nrm(ks[0], (BATCH, SEQ, D_MODEL), 1.0)
    x_sample = nrm(ks[1], (DEC_BATCH, DEC_SEQ, D_MODEL), 1.0)
    state_gdn = nrm(ks[2], (N_GDN, DEC_BATCH, GDN_V_HEADS, GDN_DK, GDN_DV), 0.05)
    state_gdn_conv = nrm(ks[3], (N_GDN, DEC_BATCH, CONV_W - 1, GDN_CONV_DIM), 1.0)
    cache_fox_k = nrm(ks[4], (N_FOX, DEC_BATCH, PAST_LEN, FOX_HEADS, FOX_DH), 1.0)
    cache_fox_v = nrm(ks[5], (N_FOX, DEC_BATCH, PAST_LEN, FOX_HEADS, FOX_DH), 1.0)
    cache_fox_logf = jax.nn.log_sigmoid(FORGET_BIAS_INIT + nrm(ks[6], (N_FOX, DEC_BATCH, PAST_LEN, FOX_HEADS), 1.0))
    norm_mix = 1.0 + nrm(ks[7], (DEPTH, D_MODEL), 0.02)
    norm_mlp = 1.0 + nrm(ks[8], (DEPTH, D_MODEL), 0.02)
    norm_final = 1.0 + nrm(ks[9], (D_MODEL,), 0.02)
    gdn_w_in = nrm(ks[10], (N_GDN, D_MODEL, GDN_IN_DIM), D_MODEL ** -0.5)
    gdn_conv_w = nrm(ks[11], (N_GDN, CONV_W, GDN_CONV_DIM), CONV_W ** -0.5)
    gdn_a_log = jnp.log(jax.random.uniform(ks[12], (N_GDN, GDN_V_HEADS), f32, 1.0, 16.0))
    dt = jnp.exp(jax.random.uniform(ks[13], (N_GDN, GDN_V_HEADS), f32, math.log(1e-3), math.log(1e-1)))
    gdn_dt_bias = dt + jnp.log(-jnp.expm1(-dt))
    gdn_norm_w = 1.0 + nrm(ks[14], (N_GDN, GDN_DV), 0.02)
    gdn_w_out = nrm(ks[15], (N_GDN, GDN_V_DIM, D_MODEL), GDN_V_DIM ** -0.5)
    fox_w_in = nrm(ks[16], (N_FOX, D_MODEL, FOX_IN_DIM), D_MODEL ** -0.5)
    fox_b_f = FORGET_BIAS_INIT + nrm(ks[17], (N_FOX, FOX_HEADS), 0.5)
    fox_w_out = nrm(ks[18], (N_FOX, FOX_DIM, D_MODEL), FOX_DIM ** -0.5)
    mlp_w_up = nrm(ks[19], (DEPTH, D_MODEL, D_FF), D_MODEL ** -0.5)
    mlp_w_down = nrm(ks[20], (DEPTH, D_FF, D_MODEL), D_FF ** -0.5)
    return {"x_prompt": x_prompt, "x_sample": x_sample,
            "state_gdn": state_gdn, "state_gdn_conv": state_gdn_conv,
            "cache_fox_k": cache_fox_k, "cache_fox_v": cache_fox_v, "cache_fox_logf": cache_fox_logf,
            "norm_mix": norm_mix, "norm_mlp": norm_mlp, "norm_final": norm_final,
            "gdn_w_in": gdn_w_in, "gdn_conv_w": gdn_conv_w, "gdn_a_log": gdn_a_log,
            "gdn_dt_bias": gdn_dt_bias, "gdn_norm_w": gdn_norm_w, "gdn_w_out": gdn_w_out,
            "fox_w_in": fox_w_in, "fox_b_f": fox_b_f, "fox_w_out": fox_w_out,
            "mlp_w_up": mlp_w_up, "mlp_w_down": mlp_w_down}


def reference(x_prompt, x_sample, state_gdn, state_gdn_conv, cache_fox_k, cache_fox_v, cache_fox_logf,
              norm_mix, norm_mlp, norm_final, gdn_w_in, gdn_conv_w, gdn_a_log, gdn_dt_bias, gdn_norm_w,
              gdn_w_out, fox_w_in, fox_b_f, fox_w_out, mlp_w_up, mlp_w_down):
    B = x_prompt.shape[0]
    S0_p = jnp.zeros((N_GDN, B, GDN_V_HEADS, GDN_DK, GDN_DV), x_prompt.dtype)
    buf0_p = jnp.zeros((N_GDN, B, CONV_W - 1, GDN_CONV_DIM), x_prompt.dtype)
    y_prompt, S_p, buf_p, k_p, v_p, lf_p = trunk(
        x_prompt, S0_p, buf0_p, None, CHUNK, norm_mix, norm_mlp, norm_final,
        gdn_w_in, gdn_conv_w, gdn_a_log, gdn_dt_bias, gdn_norm_w, gdn_w_out,
        fox_w_in, fox_b_f, fox_w_out, mlp_w_up, mlp_w_down)
    y_sample, S_s, buf_s, k_s, v_s, lf_s = trunk(
        x_sample, state_gdn, state_gdn_conv, (cache_fox_k, cache_fox_v, cache_fox_logf), x_sample.shape[1],
        norm_mix, norm_mlp, norm_final,
        gdn_w_in, gdn_conv_w, gdn_a_log, gdn_dt_bias, gdn_norm_w, gdn_w_out,
        fox_w_in, fox_b_f, fox_w_out, mlp_w_up, mlp_w_down)
    return (y_prompt, y_sample, S_p, buf_p, k_p, v_p, lf_p, S_s, buf_s, k_s, v_s, lf_s)
```

```python
import functools

import jax
import jax.numpy as jnp
from jax import lax
from jax.experimental import pallas as pl
from jax.experimental.pallas import tpu as pltpu

F32 = jnp.float32
BF16 = jnp.bfloat16
EPS = 1e-6
GDN_CHUNK = 64
VMEM_LIMIT_BYTES = 56 * 1024 * 1024
HIGHEST = lax.Precision.HIGHEST
NT_DIMS = (((1,), (1,)), ((), ()))
TN_DIMS = (((0,), (0,)), ((), ()))


def _pick(n, pref, align):
    best = None
    for d in range(align, min(n, pref) + 1, align):
        if n % d == 0:
            best = d
    return n if best is None else best


def _params(*sem):
    return pltpu.CompilerParams(dimension_semantics=sem, vmem_limit_bytes=VMEM_LIMIT_BYTES)


def _sigmoid(x):
    return 1.0 / (1.0 + jnp.exp(-x))


def _softplus(x):
    return jnp.maximum(x, 0.0) + jnp.log1p(jnp.exp(-jnp.abs(x)))


def _rms_rows(x, w):
    ms = jnp.mean(x * x, axis=-1, keepdims=True)
    return x * lax.rsqrt(ms + EPS) * w


def _norm_proj_kernel(x_ref, nw_ref, w_ref, o_ref, xn_ref, *, relu2):
    @pl.when(pl.program_id(1) == 0)
    def _():
        tm = x_ref.shape[0]
        rows = _pick(tm, 272, 16)

        def body(r, carry):
            sl = pl.ds(pl.multiple_of(r * rows, rows), rows)
            xn_ref[sl, :] = _rms_rows(x_ref[sl, :], nw_ref[...]).astype(BF16)
            return carry

        lax.fori_loop(0, tm // rows, body, 0)

    acc = jnp.dot(xn_ref[...], w_ref[...].astype(BF16), preferred_element_type=F32)
    if relu2:
        acc = jnp.square(jnp.maximum(acc, 0.0))
    o_ref[...] = acc.astype(o_ref.dtype)


def norm_proj(x, nw, w, n_cols, *, relu2=False, out_dtype=F32, name="norm_proj"):
    M, K = x.shape
    tm = _pick(M, 1088, 16)
    tn = _pick(n_cols, 512, 128)
    return pl.pallas_call(
        functools.partial(_norm_proj_kernel, relu2=relu2),
        grid=(M // tm, n_cols // tn),
        in_specs=[pl.BlockSpec((tm, K), lambda i, j: (i, 0)),
                  pl.BlockSpec((1, K), lambda i, j: (0, 0)),
                  pl.BlockSpec((K, tn), lambda i, j: (0, j))],
        out_specs=pl.BlockSpec((tm, tn), lambda i, j: (i, j)),
        out_shape=jax.ShapeDtypeStruct((M, n_cols), out_dtype),
        scratch_shapes=[pltpu.VMEM((tm, K), BF16)],
        compiler_params=_params("parallel", "arbitrary"),
        name=name,
    )(x, nw.reshape(1, K), w)


def _gdn_gate_kernel(x_ref, nw_ref, wb_ref, wa_ref, alog_ref, dtb_ref, beta_ref, g_ref):
    xn = _rms_rows(x_ref[...], nw_ref[...]).astype(BF16)
    b = jnp.dot(xn, wb_ref[...].astype(BF16), preferred_element_type=F32)
    a = jnp.dot(xn, wa_ref[...].astype(BF16), preferred_element_type=F32)
    beta_ref[...] = _sigmoid(b)
    g_ref[...] = -jnp.exp(alog_ref[...]) * _softplus(a + dtb_ref[...])


def gdn_gates(x, nw, wb, wa, a_log, dt_bias):
    M, K = x.shape
    H = wb.shape[1]
    tm = _pick(M, 544, 8)
    row = lambda i: (i, 0)
    fix = lambda i: (0, 0)
    return pl.pallas_call(
        _gdn_gate_kernel,
        grid=(M // tm,),
        in_specs=[pl.BlockSpec((tm, K), row), pl.BlockSpec((1, K), fix),
                  pl.BlockSpec((K, H), fix), pl.BlockSpec((K, H), fix),
                  pl.BlockSpec((1, H), fix), pl.BlockSpec((1, H), fix)],
        out_specs=[pl.BlockSpec((tm, H), row), pl.BlockSpec((tm, H), row)],
        out_shape=[jax.ShapeDtypeStruct((M, H), F32)] * 2,
        compiler_params=_params("parallel"),
        name="gdn_gates",
    )(x, nw.reshape(1, K), wb, wa, a_log.reshape(1, H), dt_bias.reshape(1, H))


def _fox_logf_kernel(x_ref, nw_ref, wf_ref, bf_ref, lf_ref):
    xn = _rms_rows(x_ref[...], nw_ref[...]).astype(BF16)
    f = jnp.dot(xn, wf_ref[...].astype(BF16), preferred_element_type=F32) + bf_ref[...]
    lf_ref[...] = -_softplus(-f)


def fox_logf(x, nw, wf, b_f):
    M, K = x.shape
    H = wf.shape[1]
    tm = _pick(M, 544, 8)
    row = lambda i: (i, 0)
    fix = lambda i: (0, 0)
    return pl.pallas_call(
        _fox_logf_kernel,
        grid=(M // tm,),
        in_specs=[pl.BlockSpec((tm, K), row), pl.BlockSpec((1, K), fix),
                  pl.BlockSpec((K, H), fix), pl.BlockSpec((1, H), fix)],
        out_specs=pl.BlockSpec((tm, H), row),
        out_shape=jax.ShapeDtypeStruct((M, H), F32),
        compiler_params=_params("parallel"),
        name="fox_logf",
    )(x, nw.reshape(1, K), wf, b_f.reshape(1, H))


def _matmul_res_kernel(a_ref, w_ref, r_ref, o_ref, acc_ref, *, nk):
    part = jnp.dot(a_ref[...], w_ref[...].astype(BF16), preferred_element_type=F32)
    if nk == 1:
        o_ref[...] = r_ref[...] + part
        return
    k = pl.program_id(2)

    @pl.when(k == 0)
    def _():
        acc_ref[...] = part

    @pl.when(k > 0)
    def _():
        acc_ref[...] += part

    @pl.when(k == nk - 1)
    def _():
        o_ref[...] = r_ref[...] + acc_ref[...]


def matmul_res(a, w, res, *, name="matmul_res"):
    M, K = a.shape
    N = w.shape[1]
    tm = _pick(M, 1088, 16)
    tn = _pick(N, 512, 128)
    tk = _pick(K, 2048, 128)
    nk = K // tk
    return pl.pallas_call(
        functools.partial(_matmul_res_kernel, nk=nk),
        grid=(M // tm, N // tn, nk),
        in_specs=[pl.BlockSpec((tm, tk), lambda i, j, k: (i, k)),
                  pl.BlockSpec((tk, tn), lambda i, j, k: (k, j)),
                  pl.BlockSpec((tm, tn), lambda i, j, k: (i, j))],
        out_specs=pl.BlockSpec((tm, tn), lambda i, j, k: (i, j)),
        out_shape=jax.ShapeDtypeStruct((M, N), F32),
        scratch_shapes=[pltpu.VMEM((tm, tn), F32)],
        compiler_params=_params("parallel", "parallel", "arbitrary"),
        name=name,
    )(a, w, res)


def _rmsnorm_kernel(x_ref, nw_ref, o_ref):
    o_ref[...] = _rms_rows(x_ref[...], nw_ref[...])


def rmsnorm_rows(x, nw):
    M, K = x.shape
    tm = _pick(M, 544, 8)
    return pl.pallas_call(
        _rmsnorm_kernel,
        grid=(M // tm,),
        in_specs=[pl.BlockSpec((tm, K), lambda i: (i, 0)), pl.BlockSpec((1, K), lambda i: (0, 0))],
        out_specs=pl.BlockSpec((tm, K), lambda i: (i, 0)),
        out_shape=jax.ShapeDtypeStruct((M, K), F32),
        compiler_params=_params("parallel"),
        name="final_rmsnorm",
    )(x, nw.reshape(1, K))


CONV_HALO = 8


def _gdn_conv_kernel(x_ref, w_ref, *rest, tb, cb, dk, n_q_blocks, n_qk_blocks, has_buf, q_scale):
    if has_buf:
        buf_ref, o_ref, xc_ref = rest
    else:
        o_ref, xc_ref = rest
    cw = w_ref.shape[0]
    lo = CONV_HALO - (cw - 1)

    @pl.when(pl.program_id(2) == 0)
    def _():
        if has_buf:
            xc_ref[lo:CONV_HALO, :] = buf_ref[0]
        else:
            xc_ref[lo:CONV_HALO, :] = jnp.zeros((cw - 1, cb), F32)

    xc_ref[CONV_HALO:CONV_HALO + tb, :] = x_ref[...]
    acc = xc_ref[lo:lo + tb, :] * w_ref[0:1, :]
    for j in range(1, cw):
        acc = acc + xc_ref[lo + j:lo + j + tb, :] * w_ref[j:j + 1, :]
    y = acc * _sigmoid(acc)
    xc_ref[lo:CONV_HALO, :] = xc_ref[lo + tb:CONV_HALO + tb, :]

    c = pl.program_id(1)

    def l2(scale):
        for hh in range(cb // dk):
            yh = y[:, hh * dk:(hh + 1) * dk]
            ss = jnp.sum(yh * yh, axis=-1, keepdims=True)
            o_ref[:, hh * dk:(hh + 1) * dk] = yh * (lax.rsqrt(ss + EPS) * scale)

    @pl.when(c < n_q_blocks)
    def _():
        l2(q_scale)

    @pl.when(jnp.logical_and(c >= n_q_blocks, c < n_qk_blocks))
    def _():
        l2(1.0)

    @pl.when(c >= n_qk_blocks)
    def _():
        o_ref[...] = y


def gdn_conv(proj, conv_w, buf, *, row0, nseq, T, conv_dim, qk_dim, dk):
    cw = conv_w.shape[0]
    tb = _pick(T, 512, 8)
    cb = _pick(qk_dim, 1024, dk)
    nt = T // tb
    rb0 = row0 // tb
    has_buf = buf is not None
    in_specs = [pl.BlockSpec((tb, cb), lambda s, c, t: (rb0 + s * nt + t, c)),
                pl.BlockSpec((cw, cb), lambda s, c, t: (0, c))]
    args = [proj, conv_w]
    if has_buf:
        in_specs.append(pl.BlockSpec((1, cw - 1, cb), lambda s, c, t: (s, 0, c)))
        args.append(buf)
    return pl.pallas_call(
        functools.partial(_gdn_conv_kernel, tb=tb, cb=cb, dk=dk, n_q_blocks=qk_dim // cb,
                          n_qk_blocks=2 * qk_dim // cb, has_buf=has_buf, q_scale=dk ** -0.5),
        grid=(nseq, conv_dim // cb, nt),
        in_specs=in_specs,
        out_specs=pl.BlockSpec((tb, cb), lambda s, c, t: (s * nt + t, c)),
        out_shape=jax.ShapeDtypeStruct((nseq * T, conv_dim), F32),
        scratch_shapes=[pltpu.VMEM((CONV_HALO + tb, cb), F32)],
        compiler_params=_params("parallel", "parallel", "arbitrary"),
        name="gdn_conv",
    )(*args)


def _gdn_chunk_kernel(q_ref, k_ref, v_ref, z_ref, g_ref, b_ref, nw_ref, *rest,
                      C, hb, dv, n_chunks, inv_steps, has_s0):
    if has_s0:
        s0_ref, o_ref, sout_ref, s_ref = rest
    else:
        o_ref, sout_ref, s_ref = rest
    n = pl.program_id(2)

    @pl.when(n == 0)
    def _():
        if has_s0:
            s_ref[...] = s0_ref[0]
        else:
            s_ref[...] = jnp.zeros_like(s_ref)

    q = q_ref[...]
    k = k_ref[...]
    k16 = k.astype(BF16)
    q16 = q.astype(BF16)
    row = lax.broadcasted_iota(jnp.int32, (C, C), 0)
    col = lax.broadcasted_iota(jnp.int32, (C, C), 1)
    incl = row >= col
    strict = row > col
    eye = row == col
    triu = jnp.where(row <= col, 1.0, 0.0).astype(F32)
    ident = jnp.where(eye, 1.0, 0.0).astype(F32)
    qk = lax.dot_general(q16, k16, NT_DIMS, preferred_element_type=F32)

    def to_col(r):
        return jnp.sum(jnp.where(eye, r, 0.0), axis=1, keepdims=True)

    for i in range(hb):
        vs = slice(i * dv, (i + 1) * dv)
        g_row = g_ref[0, i, pl.ds(n, 1), :]
        b_row = b_ref[0, i, pl.ds(n, 1), :]
        G_row = jnp.dot(g_row, triu, precision=HIGHEST, preferred_element_type=F32)
        G_col = to_col(G_row)
        b_col = to_col(b_row)
        diff = G_col - G_row
        dec_incl = jnp.exp(jnp.where(incl, diff, -jnp.inf))
        dec_strict = jnp.where(strict, dec_incl, 0.0)
        gamma = jnp.exp(G_col)
        G_last = G_row[:, C - 1:C]
        v = v_ref[:, vs]
        kb = k * b_col
        vb = v * b_col
        kk = lax.dot_general(kb.astype(BF16), k16, NT_DIMS, preferred_element_type=F32)
        X = -(kk * dec_strict)
        P = ident + X
        for _ in range(inv_steps):
            X = jnp.dot(X, X, precision=HIGHEST, preferred_element_type=F32)
            P = P + jnp.dot(P, X, precision=HIGHEST, preferred_element_type=F32)
        P16 = P.astype(BF16)
        Wm = jnp.dot(P16, (kb * gamma).astype(BF16), preferred_element_type=F32)
        Uv = jnp.dot(P16, vb.astype(BF16), preferred_element_type=F32)
        A = qk * dec_incl
        S = s_ref[i]
        S16 = S.astype(BF16)
        U = Uv - jnp.dot(Wm.astype(BF16), S16, preferred_element_type=F32)
        U16 = U.astype(BF16)
        o = (jnp.dot((q * gamma).astype(BF16), S16, preferred_element_type=F32)
             + jnp.dot(A.astype(BF16), U16, preferred_element_type=F32))
        kd = k * jnp.exp(G_last - G_col)
        S_new = jnp.exp(G_last) * S + lax.dot_general(kd.astype(BF16), U16, TN_DIMS,
                                                      preferred_element_type=F32)
        s_ref[i] = S_new
        z = z_ref[:, vs]
        o_ref[:, vs] = (_rms_rows(o, nw_ref[...]) * (z * _sigmoid(z))).astype(o_ref.dtype)

        @pl.when(n == n_chunks - 1)
        def _():
            sout_ref[0, i] = S_new


def gdn_chunks(conv, proj, g_rows, b_rows, norm_w, s0, *, row0, nseq, T, C, hv, hqk, dk, dv, qk_dim, z_col0):
    N = T // C
    rep = hv // hqk
    hb = rep
    inv_steps = max(0, (C - 1).bit_length() - 1)
    has_s0 = s0 is not None
    zb0 = z_col0 // (hb * dv)
    vb0 = 2 * qk_dim // (hb * dv)
    kb0 = qk_dim // dk
    rz0 = row0 // C
    in_specs = [pl.BlockSpec((C, dk), lambda b, h, n: (b * N + n, h)),
                pl.BlockSpec((C, dk), lambda b, h, n: (b * N + n, kb0 + h)),
                pl.BlockSpec((C, hb * dv), lambda b, h, n: (b * N + n, vb0 + h)),
                pl.BlockSpec((C, hb * dv), lambda b, h, n: (rz0 + b * N + n, zb0 + h)),
                pl.BlockSpec((1, hb, N, C), lambda b, h, n: (b, h, 0, 0)),
                pl.BlockSpec((1, hb, N, C), lambda b, h, n: (b, h, 0, 0)),
                pl.BlockSpec((1, dv), lambda b, h, n: (0, 0))]
    args = [conv, conv, conv, proj, g_rows, b_rows, norm_w.reshape(1, dv)]
    if has_s0:
        in_specs.append(pl.BlockSpec((1, hb, dk, dv), lambda b, h, n: (b, h, 0, 0)))
        args.append(s0)
    return pl.pallas_call(
        functools.partial(_gdn_chunk_kernel, C=C, hb=hb, dv=dv, n_chunks=N, inv_steps=inv_steps,
                          has_s0=has_s0),
        grid=(nseq, hv // hb, N),
        in_specs=in_specs,
        out_specs=[pl.BlockSpec((C, hb * dv), lambda b, h, n: (b * N + n, h)),
                   pl.BlockSpec((1, hb, dk, dv), lambda b, h, n: (b, h, 0, 0))],
        out_shape=[jax.ShapeDtypeStruct((nseq * T, hv * dv), BF16),
                   jax.ShapeDtypeStruct((nseq, hv, dk, dv), F32)],
        scratch_shapes=[pltpu.VMEM((hb, dk, dv), F32)],
        compiler_params=_params("parallel", "parallel", "arbitrary"),
        name="gdn_chunks",
    )(*args)


def _cumsum_kernel(x_ref, o_ref, carry_ref, *, tb):
    @pl.when(pl.program_id(1) == 0)
    def _():
        carry_ref[...] = jnp.zeros_like(carry_ref)

    row = lax.broadcasted_iota(jnp.int32, (tb, tb), 0)
    col = lax.broadcasted_iota(jnp.int32, (tb, tb), 1)
    tril = jnp.where(row >= col, 1.0, 0.0).astype(F32)
    c = jnp.dot(tril, x_ref[...], precision=HIGHEST, preferred_element_type=F32) + carry_ref[...]
    o_ref[...] = c
    carry_ref[...] = c[tb - 1:tb, :]


def cumsum_time(x, nseq, L):
    H = x.shape[1]
    tb = _pick(L, 256, 8)
    nt = L // tb
    return pl.pallas_call(
        functools.partial(_cumsum_kernel, tb=tb),
        grid=(nseq, nt),
        in_specs=[pl.BlockSpec((tb, H), lambda s, t: (s * nt + t, 0))],
        out_specs=pl.BlockSpec((tb, H), lambda s, t: (s * nt + t, 0)),
        out_shape=jax.ShapeDtypeStruct((nseq * L, H), F32),
        scratch_shapes=[pltpu.VMEM((1, H), F32)],
        compiler_params=_params("parallel", "arbitrary"),
        name="fox_cumsum",
    )(x)


def _fox_prompt_kernel(q_ref, k_ref, v_ref, og_ref, cq_ref, ckt_ref, o_ref, m_ref, l_ref, acc_ref,
                       *, bq, H, dh, scale):
    i = pl.program_id(1)
    j = pl.program_id(2)

    @pl.when(j == 0)
    def _():
        m_ref[...] = jnp.full(m_ref.shape, -jnp.inf, F32)
        l_ref[...] = jnp.zeros_like(l_ref)
        acc_ref[...] = jnp.zeros_like(acc_ref)

    @pl.when(j <= i)
    def _():
        qpos = i * bq + lax.broadcasted_iota(jnp.int32, (bq, bq), 0)
        kpos = j * bq + lax.broadcasted_iota(jnp.int32, (bq, bq), 1)
        mask = kpos <= qpos
        lane = lax.broadcasted_iota(jnp.int32, cq_ref.shape, 1)
        cqs = cq_ref[...]

        def body(h, carry):
            hs = pl.ds(pl.multiple_of(h * dh, dh), dh)
            q = (q_ref[:, hs] * scale).astype(BF16)
            k = k_ref[:, hs].astype(BF16)
            s = lax.dot_general(q, k, NT_DIMS, preferred_element_type=F32)
            cq = jnp.sum(jnp.where(lane == h, cqs, 0.0), axis=1, keepdims=True)
            ck = ckt_ref[pl.ds(h, 1), :]
            s = jnp.where(mask, s + (cq - ck), -jnp.inf)
            m_prev = m_ref[h]
            m_new = jnp.maximum(m_prev, jnp.max(s, axis=1, keepdims=True))
            alpha = jnp.exp(m_prev - m_new)
            p = jnp.exp(s - m_new[:, 0:1])
            l_ref[h] = alpha * l_ref[h] + jnp.sum(p, axis=1, keepdims=True)
            m_ref[h] = m_new
            pv = jnp.dot(p.astype(BF16), v_ref[:, hs].astype(BF16), preferred_element_type=F32)
            acc_ref[:, hs] = alpha * acc_ref[:, hs] + pv
            return carry

        lax.fori_loop(0, H, body, 0)

    @pl.when(j == i)
    def _():
        for h in range(H):
            hs = slice(h * dh, (h + 1) * dh)
            o = acc_ref[:, hs] / l_ref[h]
            o_ref[:, hs] = (o * _sigmoid(og_ref[:, hs])).astype(o_ref.dtype)


def fox_prompt_attention(proj, cum, cum_t, *, nseq, T, H, dh):
    bq = _pick(T, 256, 128)
    nq = T // bq
    D = H * dh
    kv = lambda col: (lambda b, i, j: (b * nq + jnp.minimum(j, i), col))
    return pl.pallas_call(
        functools.partial(_fox_prompt_kernel, bq=bq, H=H, dh=dh, scale=dh ** -0.5),
        grid=(nseq, nq, nq),
        in_specs=[pl.BlockSpec((bq, D), lambda b, i, j: (b * nq + i, 0)),
                  pl.BlockSpec((bq, D), kv(1)),
                  pl.BlockSpec((bq, D), kv(2)),
                  pl.BlockSpec((bq, D), lambda b, i, j: (b * nq + i, 3)),
                  pl.BlockSpec((bq, H), lambda b, i, j: (b * nq + i, 0)),
                  pl.BlockSpec((H, bq), lambda b, i, j: (0, b * nq + jnp.minimum(j, i)))],
        out_specs=pl.BlockSpec((bq, D), lambda b, i, j: (b * nq + i, 0)),
        out_shape=jax.ShapeDtypeStruct((nseq * T, D), BF16),
        scratch_shapes=[pltpu.VMEM((H, bq, dh), F32), pltpu.VMEM((H, bq, dh), F32),
                        pltpu.VMEM((bq, D), F32)],
        compiler_params=_params("parallel", "parallel", "arbitrary"),
        name="fox_prompt_attention",
    )(proj, proj, proj, proj, cum, cum_t)


def _fox_sample_kernel(q_ref, kn_ref, vn_ref, og_ref, kc_ref, vc_ref, cq_ref, ckt_ref, o_ref,
                       *, P, T, H, dh, scale):
    mask = (lax.broadcasted_iota(jnp.int32, (T, T), 1) <= lax.broadcasted_iota(jnp.int32, (T, T), 0))
    cqs = cq_ref[0]
    lane = lax.broadcasted_iota(jnp.int32, cqs.shape, 1)

    def body(h, carry):
        hs = pl.ds(pl.multiple_of(h * dh, dh), dh)
        q = (q_ref[:, hs] * scale).astype(BF16)
        s_p = lax.dot_general(q, kc_ref[0, :, hs].astype(BF16), NT_DIMS, preferred_element_type=F32)
        s_n = lax.dot_general(q, kn_ref[:, hs].astype(BF16), NT_DIMS, preferred_element_type=F32)
        cq = jnp.sum(jnp.where(lane == h, cqs, 0.0), axis=1, keepdims=True)
        ck = ckt_ref[0, pl.ds(h, 1), :]
        s_p = s_p + (cq - ck[:, :P])
        s_n = jnp.where(mask, s_n + (cq - ck[:, P:]), -jnp.inf)
        m = jnp.maximum(jnp.max(s_p, axis=1, keepdims=True), jnp.max(s_n, axis=1, keepdims=True))
        p_p = jnp.exp(s_p - m)
        p_n = jnp.exp(s_n - m)
        l = jnp.sum(p_p, axis=1, keepdims=True) + jnp.sum(p_n, axis=1, keepdims=True)
        o = (jnp.dot(p_p.astype(BF16), vc_ref[0, :, hs].astype(BF16), preferred_element_type=F32)
             + jnp.dot(p_n.astype(BF16), vn_ref[:, hs].astype(BF16), preferred_element_type=F32))
        o_ref[:, hs] = ((o / l) * _sigmoid(og_ref[:, hs])).astype(o_ref.dtype)
        return carry

    lax.fori_loop(0, H, body, 0)


def fox_sample_attention(proj, k_cache, v_cache, cum_new, cum_t, *, row0, nseq, T, P, H, dh):
    D = H * dh
    rb0 = row0 // T
    new = lambda col: (lambda b: (rb0 + b, col))
    return pl.pallas_call(
        functools.partial(_fox_sample_kernel, P=P, T=T, H=H, dh=dh, scale=dh ** -0.5),
        grid=(nseq,),
        in_specs=[pl.BlockSpec((T, D), new(0)), pl.BlockSpec((T, D), new(1)),
                  pl.BlockSpec((T, D), new(2)), pl.BlockSpec((T, D), new(3)),
                  pl.BlockSpec((1, P, D), lambda b: (b, 0, 0)),
                  pl.BlockSpec((1, P, D), lambda b: (b, 0, 0)),
                  pl.BlockSpec((1, T, H), lambda b: (b, 0, 0)),
                  pl.BlockSpec((1, H, P + T), lambda b: (b, 0, 0))],
        out_specs=pl.BlockSpec((T, D), lambda b: (b, 0)),
        out_shape=jax.ShapeDtypeStruct((nseq * T, D), BF16),
        compiler_params=_params("parallel"),
        name="fox_sample_attention",
    )(proj, proj, proj, proj, k_cache, v_cache, cum_new, cum_t)


def kernel(x_prompt, x_sample, state_gdn, state_gdn_conv, cache_fox_k, cache_fox_v, cache_fox_logf,
           norm_mix, norm_mlp, norm_final, gdn_w_in, gdn_conv_w, gdn_a_log, gdn_dt_bias, gdn_norm_w,
           gdn_w_out, fox_w_in, fox_b_f, fox_w_out, mlp_w_up, mlp_w_down):
    Bp, Tp, D = x_prompt.shape
    Bs, Ts, _ = x_sample.shape
    depth = norm_mix.shape[0]
    _, _, hv, dk, dv = state_gdn.shape
    conv_dim = state_gdn_conv.shape[-1]
    cw1 = state_gdn_conv.shape[2]
    v_dim = hv * dv
    qk_dim = (conv_dim - v_dim) // 2
    hqk = qk_dim // dk
    _, _, P, H, dh = cache_fox_k.shape
    fox_dim = H * dh
    d_ff = mlp_w_up.shape[-1]
    Mp, Ms = Bp * Tp, Bs * Ts
    Cp = min(GDN_CHUNK, Tp)

    x = jnp.concatenate([x_prompt.reshape(Mp, D), x_sample.reshape(Ms, D)], axis=0)
    S_p, S_s, buf_p, buf_s = [], [], [], []
    k_p, k_s, v_p, v_s, lf_p, lf_s = [], [], [], [], [], []

    for i in range(depth):
        j = i // 2
        if i % 2 == 0:
            w_in = gdn_w_in[j]
            n_main = conv_dim + v_dim
            proj = norm_proj(x, norm_mix[i], w_in, n_main, name="gdn_in_proj")
            beta, g = gdn_gates(x, norm_mix[i], w_in[:, n_main:n_main + hv], w_in[:, n_main + hv:],
                                gdn_a_log[j], gdn_dt_bias[j])
            conv_p = gdn_conv(proj, gdn_conv_w[j], None, row0=0, nseq=Bp, T=Tp,
                              conv_dim=conv_dim, qk_dim=qk_dim, dk=dk)
            conv_s = gdn_conv(proj, gdn_conv_w[j], state_gdn_conv[j], row0=Mp, nseq=Bs, T=Ts,
                              conv_dim=conv_dim, qk_dim=qk_dim, dk=dk)

            def rows(a, B, T, C):
                return a.reshape(B, T // C, C, hv).transpose(0, 3, 1, 2)

            common = dict(hv=hv, hqk=hqk, dk=dk, dv=dv, qk_dim=qk_dim, z_col0=conv_dim)
            o_p, Sp = gdn_chunks(conv_p, proj, rows(g[:Mp], Bp, Tp, Cp), rows(beta[:Mp], Bp, Tp, Cp),
                                 gdn_norm_w[j], None, row0=0, nseq=Bp, T=Tp, C=Cp, **common)
            o_s, Ss = gdn_chunks(conv_s, proj, rows(g[Mp:], Bs, Ts, Ts), rows(beta[Mp:], Bs, Ts, Ts),
                                 gdn_norm_w[j], state_gdn[j], row0=Mp, nseq=Bs, T=Ts, C=Ts, **common)
            x = matmul_res(jnp.concatenate([o_p, o_s], axis=0), gdn_w_out[j], x, name="gdn_out_proj")
            S_p.append(Sp)
            S_s.append(Ss)
            qkv_p = proj[:Mp, :conv_dim].reshape(Bp, Tp, conv_dim)
            qkv_s = proj[Mp:, :conv_dim].reshape(Bs, Ts, conv_dim)
            buf_p.append(qkv_p[:, Tp - cw1:])
            buf_s.append(jnp.concatenate([state_gdn_conv[j], qkv_s], axis=1)[:, -cw1:])
        else:
            w_in = fox_w_in[j]
            proj = norm_proj(x, norm_mix[i], w_in, 4 * fox_dim, name="fox_in_proj")
            logf = fox_logf(x, norm_mix[i], w_in[:, 4 * fox_dim:], fox_b_f[j])
            cum_p = cumsum_time(logf[:Mp], Bp, Tp)
            o_p = fox_prompt_attention(proj, cum_p, cum_p.T, nseq=Bp, T=Tp, H=H, dh=dh)
            lf_new = logf[Mp:].reshape(Bs, Ts, H)
            lf_all = jnp.concatenate([cache_fox_logf[j], lf_new], axis=1)
            cum_s = cumsum_time(lf_all.reshape(Bs * (P + Ts), H), Bs, P + Ts).reshape(Bs, P + Ts, H)
            o_s = fox_sample_attention(proj, cache_fox_k[j].reshape(Bs, P, fox_dim),
                                       cache_fox_v[j].reshape(Bs, P, fox_dim),
                                       cum_s[:, P:], cum_s.transpose(0, 2, 1),
                                       row0=Mp, nseq=Bs, T=Ts, P=P, H=H, dh=dh)
            x = matmul_res(jnp.concatenate([o_p, o_s], axis=0), fox_w_out[j], x, name="fox_out_proj")
            kk = proj[:, fox_dim:2 * fox_dim]
            vv = proj[:, 2 * fox_dim:3 * fox_dim]
            k_p.append(kk[:Mp].reshape(Bp, Tp, H, dh))
            k_s.append(kk[Mp:].reshape(Bs, Ts, H, dh))
            v_p.append(vv[:Mp].reshape(Bp, Tp, H, dh))
            v_s.append(vv[Mp:].reshape(Bs, Ts, H, dh))
            lf_p.append(logf[:Mp].reshape(Bp, Tp, H))
            lf_s.append(lf_new)
        hmid = norm_proj(x, norm_mlp[i], mlp_w_up[i], d_ff, relu2=True, out_dtype=BF16, name="mlp_up")
        x = matmul_res(hmid, mlp_w_down[i], x, name="mlp_down")

    y = rmsnorm_rows(x, norm_final)
    st = jnp.stack
    return (y[:Mp].reshape(Bp, Tp, D), y[Mp:].reshape(Bs, Ts, D),
            st(S_p), st(buf_p), st(k_p), st(v_p), st(lf_p),
            st(S_s), st(buf_s), st(k_s), st(v_s), st(lf_s))
```

```python
import functools

import jax
import jax.numpy as jnp
from jax import lax
from jax.experimental import pallas as pl
from jax.experimental.pallas import tpu as pltpu

F32 = jnp.float32
BF16 = jnp.bfloat16
EPS = 1e-6
GDN_CHUNK = 64
GDN_HEADS_PER_STEP = 8
INV_BASE_BLOCK = 8
VMEM_LIMIT_BYTES = 56 * 1024 * 1024
HIGHEST = lax.Precision.HIGHEST
NT_DIMS = (((1,), (1,)), ((), ()))
TN_DIMS = (((0,), (0,)), ((), ()))


def _pick(n, pref, align):
    best = None
    for d in range(align, min(n, pref) + 1, align):
        if n % d == 0:
            best = d
    return n if best is None else best


def _params(*sem):
    return pltpu.CompilerParams(dimension_semantics=sem, vmem_limit_bytes=VMEM_LIMIT_BYTES)


def _sigmoid(x):
    return 1.0 / (1.0 + jnp.exp(-x))


def _softplus(x):
    return jnp.maximum(x, 0.0) + jnp.log1p(jnp.exp(-jnp.abs(x)))


def _rms_rows(x, w):
    ms = jnp.mean(x * x, axis=-1, keepdims=True)
    return x * lax.rsqrt(ms + EPS) * w


def _bdot(a, b):
    return jnp.dot(a.astype(BF16), b.astype(BF16), preferred_element_type=F32)


def _norm_proj_kernel(x_ref, nw_ref, w_ref, o_ref, xn_ref, *, relu2):
    @pl.when(pl.program_id(1) == 0)
    def _():
        tm = x_ref.shape[0]
        rows = _pick(tm, 272, 16)

        def body(r, carry):
            sl = pl.ds(pl.multiple_of(r * rows, rows), rows)
            xn_ref[sl, :] = _rms_rows(x_ref[sl, :], nw_ref[...]).astype(BF16)
            return carry

        lax.fori_loop(0, tm // rows, body, 0)

    acc = jnp.dot(xn_ref[...], w_ref[...].astype(BF16), preferred_element_type=F32)
    if relu2:
        acc = jnp.square(jnp.maximum(acc, 0.0))
    o_ref[...] = acc.astype(o_ref.dtype)


def norm_proj(x, nw, w_all, layer, n_cols, *, relu2=False, out_dtype=F32, name="norm_proj"):
    M, K = x.shape
    tm = _pick(M, 1088, 16)
    tn = _pick(n_cols, 512, 128)
    return pl.pallas_call(
        functools.partial(_norm_proj_kernel, relu2=relu2),
        grid=(M // tm, n_cols // tn),
        in_specs=[pl.BlockSpec((tm, K), lambda i, j: (i, 0)),
                  pl.BlockSpec((1, K), lambda i, j: (0, 0)),
                  pl.BlockSpec((None, K, tn), lambda i, j: (layer, 0, j))],
        out_specs=pl.BlockSpec((tm, tn), lambda i, j: (i, j)),
        out_shape=jax.ShapeDtypeStruct((M, n_cols), out_dtype),
        scratch_shapes=[pltpu.VMEM((tm, K), BF16)],
        compiler_params=_params("parallel", "arbitrary"),
        name=name,
    )(x, nw.reshape(1, K), w_all)


def _gdn_gate_kernel(x_ref, nw_ref, wb_ref, wa_ref, alog_ref, dtb_ref, beta_ref, g_ref):
    xn = _rms_rows(x_ref[...], nw_ref[...]).astype(BF16)
    b = jnp.dot(xn, wb_ref[...].astype(BF16), preferred_element_type=F32)
    a = jnp.dot(xn, wa_ref[...].astype(BF16), preferred_element_type=F32)
    beta_ref[...] = _sigmoid(b)
    g_ref[...] = -jnp.exp(alog_ref[...]) * _softplus(a + dtb_ref[...])


def gdn_gates(x, nw, wb, wa, a_log, dt_bias):
    M, K = x.shape
    H = wb.shape[1]
    tm = _pick(M, 544, 8)
    row = lambda i: (i, 0)
    fix = lambda i: (0, 0)
    return pl.pallas_call(
        _gdn_gate_kernel,
        grid=(M // tm,),
        in_specs=[pl.BlockSpec((tm, K), row), pl.BlockSpec((1, K), fix),
                  pl.BlockSpec((K, H), fix), pl.BlockSpec((K, H), fix),
                  pl.BlockSpec((1, H), fix), pl.BlockSpec((1, H), fix)],
        out_specs=[pl.BlockSpec((tm, H), row), pl.BlockSpec((tm, H), row)],
        out_shape=[jax.ShapeDtypeStruct((M, H), F32)] * 2,
        compiler_params=_params("parallel"),
        name="gdn_gates",
    )(x, nw.reshape(1, K), wb, wa, a_log.reshape(1, H), dt_bias.reshape(1, H))


def _fox_logf_kernel(x_ref, nw_ref, wf_ref, bf_ref, lf_ref):
    xn = _rms_rows(x_ref[...], nw_ref[...]).astype(BF16)
    f = jnp.dot(xn, wf_ref[...].astype(BF16), preferred_element_type=F32) + bf_ref[...]
    lf_ref[...] = -_softplus(-f)


def fox_logf(x, nw, wf, b_f):
    M, K = x.shape
    H = wf.shape[1]
    tm = _pick(M, 544, 8)
    row = lambda i: (i, 0)
    fix = lambda i: (0, 0)
    return pl.pallas_call(
        _fox_logf_kernel,
        grid=(M // tm,),
        in_specs=[pl.BlockSpec((tm, K), row), pl.BlockSpec((1, K), fix),
                  pl.BlockSpec((K, H), fix), pl.BlockSpec((1, H), fix)],
        out_specs=pl.BlockSpec((tm, H), row),
        out_shape=jax.ShapeDtypeStruct((M, H), F32),
        compiler_params=_params("parallel"),
        name="fox_logf",
    )(x, nw.reshape(1, K), wf, b_f.reshape(1, H))


def _matmul_res_kernel(a_ref, w_ref, r_ref, o_ref, acc_ref, *, nk):
    part = jnp.dot(a_ref[...], w_ref[...].astype(BF16), preferred_element_type=F32)
    if nk == 1:
        o_ref[...] = r_ref[...] + part
        return
    k = pl.program_id(2)

    @pl.when(k == 0)
    def _():
        acc_ref[...] = part

    @pl.when(k > 0)
    def _():
        acc_ref[...] += part

    @pl.when(k == nk - 1)
    def _():
        o_ref[...] = r_ref[...] + acc_ref[...]


def matmul_res(a, w_all, layer, res, *, name="matmul_res"):
    M, K = a.shape
    N = w_all.shape[-1]
    tm = _pick(M, 1088, 16)
    tn = _pick(N, 512, 128)
    tk = _pick(K, 2048, 128)
    nk = K // tk
    return pl.pallas_call(
        functools.partial(_matmul_res_kernel, nk=nk),
        grid=(M // tm, N // tn, nk),
        in_specs=[pl.BlockSpec((tm, tk), lambda i, j, k: (i, k)),
                  pl.BlockSpec((None, tk, tn), lambda i, j, k: (layer, k, j)),
                  pl.BlockSpec((tm, tn), lambda i, j, k: (i, j))],
        out_specs=pl.BlockSpec((tm, tn), lambda i, j, k: (i, j)),
        out_shape=jax.ShapeDtypeStruct((M, N), F32),
        scratch_shapes=[pltpu.VMEM((tm, tn), F32)],
        compiler_params=_params("parallel", "parallel", "arbitrary"),
        name=name,
    )(a, w_all, res)


def _rmsnorm_kernel(x_ref, nw_ref, o_ref):
    o_ref[...] = _rms_rows(x_ref[...], nw_ref[...])


def rmsnorm_rows(x, nw, row0, n_rows):
    K = x.shape[1]
    tm = _pick(n_rows, 512, 8)
    rb0 = row0 // tm
    return pl.pallas_call(
        _rmsnorm_kernel,
        grid=(n_rows // tm,),
        in_specs=[pl.BlockSpec((tm, K), lambda i: (rb0 + i, 0)), pl.BlockSpec((1, K), lambda i: (0, 0))],
        out_specs=pl.BlockSpec((tm, K), lambda i: (i, 0)),
        out_shape=jax.ShapeDtypeStruct((n_rows, K), F32),
        compiler_params=_params("parallel"),
        name="final_rmsnorm",
    )(x, nw.reshape(1, K))


CONV_HALO = 8


def _gdn_conv_kernel(x_ref, w_ref, *rest, tb, cb, dk, n_q_blocks, n_qk_blocks, has_buf, q_scale):
    if has_buf:
        buf_ref, o_ref, xc_ref = rest
    else:
        o_ref, xc_ref = rest
    cw = w_ref.shape[0]
    lo = CONV_HALO - (cw - 1)

    @pl.when(pl.program_id(2) == 0)
    def _():
        if has_buf:
            xc_ref[lo:CONV_HALO, :] = buf_ref[0]
        else:
            xc_ref[lo:CONV_HALO, :] = jnp.zeros((cw - 1, cb), F32)

    xc_ref[CONV_HALO:CONV_HALO + tb, :] = x_ref[...]
    acc = xc_ref[lo:lo + tb, :] * w_ref[0:1, :]
    for j in range(1, cw):
        acc = acc + xc_ref[lo + j:lo + j + tb, :] * w_ref[j:j + 1, :]
    y = acc * _sigmoid(acc)
    xc_ref[lo:CONV_HALO, :] = xc_ref[lo + tb:CONV_HALO + tb, :]

    c = pl.program_id(1)

    def l2(scale):
        for hh in range(cb // dk):
            yh = y[:, hh * dk:(hh + 1) * dk]
            ss = jnp.sum(yh * yh, axis=-1, keepdims=True)
            o_ref[:, hh * dk:(hh + 1) * dk] = yh * (lax.rsqrt(ss + EPS) * scale)

    @pl.when(c < n_q_blocks)
    def _():
        l2(q_scale)

    @pl.when(jnp.logical_and(c >= n_q_blocks, c < n_qk_blocks))
    def _():
        l2(1.0)

    @pl.when(c >= n_qk_blocks)
    def _():
        o_ref[...] = y


def gdn_conv(proj, conv_w_all, buf_all, layer, *, row0, nseq, T, conv_dim, qk_dim, dk):
    cw = conv_w_all.shape[1]
    tb = _pick(T, 512, 8)
    cb = _pick(qk_dim, 1024, dk)
    nt = T // tb
    rb0 = row0 // tb
    has_buf = buf_all is not None
    in_specs = [pl.BlockSpec((tb, cb), lambda s, c, t: (rb0 + s * nt + t, c)),
                pl.BlockSpec((None, cw, cb), lambda s, c, t: (layer, 0, c))]
    args = [proj, conv_w_all]
    if has_buf:
        in_specs.append(pl.BlockSpec((None, 1, cw - 1, cb), lambda s, c, t: (layer, s, 0, c)))
        args.append(buf_all)
    return pl.pallas_call(
        functools.partial(_gdn_conv_kernel, tb=tb, cb=cb, dk=dk, n_q_blocks=qk_dim // cb,
                          n_qk_blocks=2 * qk_dim // cb, has_buf=has_buf, q_scale=dk ** -0.5),
        grid=(nseq, conv_dim // cb, nt),
        in_specs=in_specs,
        out_specs=pl.BlockSpec((tb, cb), lambda s, c, t: (s * nt + t, c)),
        out_shape=jax.ShapeDtypeStruct((nseq * T, conv_dim), F32),
        scratch_shapes=[pltpu.VMEM((CONV_HALO + tb, cb), F32)],
        compiler_params=_params("parallel", "parallel", "arbitrary"),
        name="gdn_conv",
    )(*args)


def _gdn_chunk_kernel(q_ref, k_ref, v_ref, z_ref, g_ref, b_ref, nw_ref, *rest,
                      C, hq, rep, dk, dv, n_chunks, has_s0):
    if has_s0:
        s0_ref, o_ref, sout_ref, s_ref = rest
    else:
        o_ref, sout_ref, s_ref = rest
    n = pl.program_id(2)
    C2 = 2 * C

    @pl.when(n == 0)
    def _():
        if has_s0:
            s_ref[...] = s0_ref[0]
        else:
            s_ref[...] = jnp.zeros_like(s_ref)

    row = lax.broadcasted_iota(jnp.int32, (C, C2), 0)
    lane = lax.broadcasted_iota(jnp.int32, (C, C2), 1)
    right = lane >= C
    col = jnp.where(right, lane - C, lane)
    incl = row >= col
    strict = row > col
    diag = lane == row
    eye_right = jnp.where(lane == row + C, 1.0, 0.0).astype(F32)
    tr = lax.broadcasted_iota(jnp.int32, (C, C2), 0)
    triu2 = jnp.where(tr <= col, 1.0, 0.0).astype(F32)
    zeros_z = jnp.zeros((C, C2), BF16)
    zeros_r = jnp.zeros((C, dk + dv), BF16)

    def to_col(r2):
        return jnp.sum(jnp.where(diag, r2, 0.0), axis=1, keepdims=True)

    diag_c = (lax.broadcasted_iota(jnp.int32, (C, C), 0) == lax.broadcasted_iota(jnp.int32, (C, C), 1))

    def to_col_c(r):
        return jnp.sum(jnp.where(diag_c, r, 0.0), axis=1, keepdims=True)

    heads = range(hq * rep)
    q = [q_ref[:, a * dk:(a + 1) * dk] for a in range(hq)]
    k = [k_ref[:, a * dk:(a + 1) * dk] for a in range(hq)]
    q16 = [x.astype(BF16) for x in q]
    kk16 = [jnp.concatenate([x.astype(BF16)] * 2, axis=0) for x in k]
    G2, G_col, b_col, dec_incl, gamma, G_last, kb, vb = [], [], [], [], [], [], [], []
    for i in heads:
        g_row = g_ref[0, i, pl.ds(n, 1), :]
        b_row = b_ref[0, i, pl.ds(n, 1), :]
        G2.append(jnp.dot(g_row, triu2, precision=HIGHEST, preferred_element_type=F32))
        G_col.append(to_col(G2[i]))
        b_col.append(to_col_c(b_row))
        dec_incl.append(jnp.exp(jnp.where(incl, G_col[i] - G2[i], -jnp.inf)))
        gamma.append(jnp.exp(G_col[i]))
        G_last.append(G2[i][:, C - 1:C])
        kb.append(k[i // rep] * b_col[i])
        vb.append(v_ref[:, i * dv:(i + 1) * dv] * b_col[i])
    gram = [lax.dot_general(jnp.concatenate([kb[i].astype(BF16), q16[i // rep]], axis=0), kk16[i // rep],
                            NT_DIMS, preferred_element_type=F32) for i in heads]
    Lf = [gram[i][:C] * dec_incl[i] for i in heads]
    base = min(INV_BASE_BLOCK, C)
    sh = base.bit_length() - 1
    in_base = jnp.logical_and(strict, (row >> sh) == (col >> sh))
    Z = [jnp.where(right, eye_right, -jnp.where(in_base, Lf[i], 0.0)) for i in heads]
    for _ in range((base - 1).bit_length()):
        Z16 = [z.astype(BF16) for z in Z]
        Z = [jnp.dot(Z16[i], jnp.concatenate([Z16[i], zeros_z], axis=0), preferred_element_type=F32)
             + jnp.where(right, Z[i], 0.0) for i in heads]
    b = base
    while b < C:
        sh = b.bit_length() - 1
        lower_left = jnp.logical_and(jnp.logical_and(right, (row >> (sh + 1)) == (col >> (sh + 1))),
                                     jnp.logical_and(((row >> sh) & 1) == 1, ((col >> sh) & 1) == 0))
        Z16 = [z.astype(BF16) for z in Z]
        W16 = [jnp.dot(jnp.where(lower_left, Lf[i], 0.0).astype(BF16),
                       jnp.concatenate([zeros_z, Z16[i]], axis=0),
                       preferred_element_type=F32).astype(BF16) for i in heads]
        Z = [Z[i] - jnp.dot(Z16[i], jnp.concatenate([zeros_z, W16[i]], axis=0), preferred_element_type=F32)
             for i in heads]
        b *= 2
    wu = []
    for i in heads:
        rhs = jnp.concatenate([(kb[i] * gamma[i]).astype(BF16), vb[i].astype(BF16)], axis=1)
        wu.append(jnp.dot(Z[i].astype(BF16), jnp.concatenate([zeros_r, rhs], axis=0),
                          preferred_element_type=F32))
    S = [s_ref[i] for i in heads]
    S16 = [x.astype(BF16) for x in S]
    ws_qs = [jnp.dot(jnp.concatenate([wu[i][:, :dk].astype(BF16), (q[i // rep] * gamma[i]).astype(BF16)], axis=0),
                     S16[i], preferred_element_type=F32) for i in heads]
    U16 = [(wu[i][:, dk:] - ws_qs[i][:C]).astype(BF16) for i in heads]
    A16 = [(gram[i][C:] * dec_incl[i])[:, :C].astype(BF16) for i in heads]
    o = [ws_qs[i][C:] + jnp.dot(A16[i], U16[i], preferred_element_type=F32) for i in heads]
    for i in heads:
        kd = k[i // rep] * jnp.exp(G_last[i] - G_col[i])
        s_ref[i] = jnp.exp(G_last[i]) * S[i] + lax.dot_general(kd.astype(BF16), U16[i], TN_DIMS,
                                                               preferred_element_type=F32)
    for i in heads:
        z = z_ref[:, i * dv:(i + 1) * dv]
        o_ref[:, i * dv:(i + 1) * dv] = (_rms_rows(o[i], nw_ref[...]) * (z * _sigmoid(z))).astype(o_ref.dtype)

    @pl.when(n == n_chunks - 1)
    def _():
        sout_ref[0] = s_ref[...]


def gdn_chunks(conv, proj, g_rows, b_rows, norm_w, s0_all, layer, *, row0, nseq, T, C, hv, hqk, dk, dv,
               qk_dim, z_col0):
    N = T // C
    rep = hv // hqk
    hb = min(GDN_HEADS_PER_STEP, hv)
    hq = hb // rep
    has_s0 = s0_all is not None
    kb0 = qk_dim // (hq * dk)
    vb0 = 2 * qk_dim // (hb * dv)
    zb0 = z_col0 // (hb * dv)
    rz0 = row0 // C
    in_specs = [pl.BlockSpec((C, hq * dk), lambda b, h, n: (b * N + n, h)),
                pl.BlockSpec((C, hq * dk), lambda b, h, n: (b * N + n, kb0 + h)),
                pl.BlockSpec((C, hb * dv), lambda b, h, n: (b * N + n, vb0 + h)),
                pl.BlockSpec((C, hb * dv), lambda b, h, n: (rz0 + b * N + n, zb0 + h)),
                pl.BlockSpec((1, hb, N, C), lambda b, h, n: (b, h, 0, 0)),
                pl.BlockSpec((1, hb, N, C), lambda b, h, n: (b, h, 0, 0)),
                pl.BlockSpec((1, dv), lambda b, h, n: (0, 0))]
    args = [conv, conv, conv, proj, g_rows, b_rows, norm_w.reshape(1, dv)]
    if has_s0:
        in_specs.append(pl.BlockSpec((None, 1, hb, dk, dv), lambda b, h, n: (layer, b, h, 0, 0)))
        args.append(s0_all)
    return pl.pallas_call(
        functools.partial(_gdn_chunk_kernel, C=C, hq=hq, rep=rep, dk=dk, dv=dv, n_chunks=N,
                          has_s0=has_s0),
        grid=(nseq, hv // hb, N),
        in_specs=in_specs,
        out_specs=[pl.BlockSpec((C, hb * dv), lambda b, h, n: (b * N + n, h)),
                   pl.BlockSpec((1, hb, dk, dv), lambda b, h, n: (b, h, 0, 0))],
        out_shape=[jax.ShapeDtypeStruct((nseq * T, hv * dv), BF16),
                   jax.ShapeDtypeStruct((nseq, hv, dk, dv), F32)],
        scratch_shapes=[pltpu.VMEM((hb, dk, dv), F32)],
        compiler_params=_params("parallel", "parallel", "arbitrary"),
        name="gdn_chunks",
    )(*args)


def _cumsum_kernel(x_ref, o_ref, carry_ref, *, tb):
    @pl.when(pl.program_id(1) == 0)
    def _():
        carry_ref[...] = jnp.zeros_like(carry_ref)

    row = lax.broadcasted_iota(jnp.int32, (tb, tb), 0)
    col = lax.broadcasted_iota(jnp.int32, (tb, tb), 1)
    tril = jnp.where(row >= col, 1.0, 0.0).astype(F32)
    c = jnp.dot(tril, x_ref[...], precision=HIGHEST, preferred_element_type=F32) + carry_ref[...]
    o_ref[...] = c
    carry_ref[...] = c[tb - 1:tb, :]


def cumsum_time(x, nseq, L):
    H = x.shape[1]
    tb = _pick(L, 256, 8)
    nt = L // tb
    return pl.pallas_call(
        functools.partial(_cumsum_kernel, tb=tb),
        grid=(nseq, nt),
        in_specs=[pl.BlockSpec((tb, H), lambda s, t: (s * nt + t, 0))],
        out_specs=pl.BlockSpec((tb, H), lambda s, t: (s * nt + t, 0)),
        out_shape=jax.ShapeDtypeStruct((nseq * L, H), F32),
        scratch_shapes=[pltpu.VMEM((1, H), F32)],
        compiler_params=_params("parallel", "arbitrary"),
        name="fox_cumsum",
    )(x)


def _fox_prompt_kernel(q_ref, k_ref, v_ref, og_ref, cq_ref, ckt_ref, o_ref, m_ref, l_ref, acc_ref,
                       *, bq, H, dh, scale):
    i = pl.program_id(1)
    j = pl.program_id(2)

    @pl.when(j == 0)
    def _():
        m_ref[...] = jnp.full(m_ref.shape, -jnp.inf, F32)
        l_ref[...] = jnp.zeros_like(l_ref)
        acc_ref[...] = jnp.zeros_like(acc_ref)

    @pl.when(j <= i)
    def _():
        qpos = i * bq + lax.broadcasted_iota(jnp.int32, (bq, bq), 0)
        kpos = j * bq + lax.broadcasted_iota(jnp.int32, (bq, bq), 1)
        mask = kpos <= qpos
        lane = lax.broadcasted_iota(jnp.int32, cq_ref.shape, 1)
        cqs = cq_ref[...]

        def body(h, carry):
            hs = pl.ds(pl.multiple_of(h * dh, dh), dh)
            q = (q_ref[:, hs] * scale).astype(BF16)
            k = k_ref[:, hs].astype(BF16)
            s = lax.dot_general(q, k, NT_DIMS, preferred_element_type=F32)
            cq = jnp.sum(jnp.where(lane == h, cqs, 0.0), axis=1, keepdims=True)
            ck = ckt_ref[pl.ds(h, 1), :]
            s = jnp.where(mask, s + (cq - ck), -jnp.inf)
            m_prev = m_ref[h]
            m_new = jnp.maximum(m_prev, jnp.max(s, axis=1, keepdims=True))
            alpha = jnp.exp(m_prev - m_new)
            p = jnp.exp(s - m_new[:, 0:1])
            l_ref[h] = alpha * l_ref[h] + jnp.sum(p, axis=1, keepdims=True)
            m_ref[h] = m_new
            pv = jnp.dot(p.astype(BF16), v_ref[:, hs].astype(BF16), preferred_element_type=F32)
            acc_ref[:, hs] = alpha * acc_ref[:, hs] + pv
            return carry

        lax.fori_loop(0, H, body, 0)

    @pl.when(j == i)
    def _():
        for h in range(H):
            hs = slice(h * dh, (h + 1) * dh)
            o = acc_ref[:, hs] / l_ref[h]
            o_ref[:, hs] = (o * _sigmoid(og_ref[:, hs])).astype(o_ref.dtype)


def fox_prompt_attention(proj, cum, cum_t, *, nseq, T, H, dh):
    bq = _pick(T, 256, 128)
    nq = T // bq
    D = H * dh
    kv = lambda col: (lambda b, i, j: (b * nq + jnp.minimum(j, i), col))
    return pl.pallas_call(
        functools.partial(_fox_prompt_kernel, bq=bq, H=H, dh=dh, scale=dh ** -0.5),
        grid=(nseq, nq, nq),
        in_specs=[pl.BlockSpec((bq, D), lambda b, i, j: (b * nq + i, 0)),
                  pl.BlockSpec((bq, D), kv(1)),
                  pl.BlockSpec((bq, D), kv(2)),
                  pl.BlockSpec((bq, D), lambda b, i, j: (b * nq + i, 3)),
                  pl.BlockSpec((bq, H), lambda b, i, j: (b * nq + i, 0)),
                  pl.BlockSpec((H, bq), lambda b, i, j: (0, b * nq + jnp.minimum(j, i)))],
        out_specs=pl.BlockSpec((bq, D), lambda b, i, j: (b * nq + i, 0)),
        out_shape=jax.ShapeDtypeStruct((nseq * T, D), BF16),
        scratch_shapes=[pltpu.VMEM((H, bq, dh), F32), pltpu.VMEM((H, bq, dh), F32),
                        pltpu.VMEM((bq, D), F32)],
        compiler_params=_params("parallel", "parallel", "arbitrary"),
        name="fox_prompt_attention",
    )(proj, proj, proj, proj, cum, cum_t)


def _fox_sample_kernel(q_ref, kn_ref, vn_ref, og_ref, kc_ref, vc_ref, cq_ref, ckt_ref, o_ref,
                       *, P, T, H, dh, scale):
    mask = (lax.broadcasted_iota(jnp.int32, (T, T), 1) <= lax.broadcasted_iota(jnp.int32, (T, T), 0))
    cqs = cq_ref[0]
    for h in range(H):
        hs = slice(h * dh, (h + 1) * dh)
        q = (q_ref[:, hs] * scale).astype(BF16)
        s_p = lax.dot_general(q, kc_ref[0, :, h, :].astype(BF16), NT_DIMS, preferred_element_type=F32)
        s_n = lax.dot_general(q, kn_ref[:, hs].astype(BF16), NT_DIMS, preferred_element_type=F32)
        cq = cqs[:, h:h + 1]
        ck = ckt_ref[0, h:h + 1, :]
        s_p = s_p + (cq - ck[:, :P])
        s_n = jnp.where(mask, s_n + (cq - ck[:, P:]), -jnp.inf)
        m = jnp.maximum(jnp.max(s_p, axis=1, keepdims=True), jnp.max(s_n, axis=1, keepdims=True))
        p_p = jnp.exp(s_p - m)
        p_n = jnp.exp(s_n - m)
        l = jnp.sum(p_p, axis=1, keepdims=True) + jnp.sum(p_n, axis=1, keepdims=True)
        o = (jnp.dot(p_p.astype(BF16), vc_ref[0, :, h, :].astype(BF16), preferred_element_type=F32)
             + jnp.dot(p_n.astype(BF16), vn_ref[:, hs].astype(BF16), preferred_element_type=F32))
        o_ref[:, hs] = ((o / l) * _sigmoid(og_ref[:, hs])).astype(o_ref.dtype)


def fox_sample_attention(proj, k_cache_all, v_cache_all, layer, cum_new, cum_t, *, row0, nseq, T, P, H, dh):
    D = H * dh
    rb0 = row0 // T
    new = lambda col: (lambda b: (rb0 + b, col))
    cache = pl.BlockSpec((None, 1, P, H, dh), lambda b: (layer, b, 0, 0, 0))
    return pl.pallas_call(
        functools.partial(_fox_sample_kernel, P=P, T=T, H=H, dh=dh, scale=dh ** -0.5),
        grid=(nseq,),
        in_specs=[pl.BlockSpec((T, D), new(0)), pl.BlockSpec((T, D), new(1)),
                  pl.BlockSpec((T, D), new(2)), pl.BlockSpec((T, D), new(3)),
                  cache, cache,
                  pl.BlockSpec((1, T, H), lambda b: (b, 0, 0)),
                  pl.BlockSpec((1, H, P + T), lambda b: (b, 0, 0))],
        out_specs=pl.BlockSpec((T, D), lambda b: (b, 0)),
        out_shape=jax.ShapeDtypeStruct((nseq * T, D), BF16),
        compiler_params=_params("parallel"),
        name="fox_sample_attention",
    )(proj, proj, proj, proj, k_cache_all, v_cache_all, cum_new, cum_t)


def kernel(x_prompt, x_sample, state_gdn, state_gdn_conv, cache_fox_k, cache_fox_v, cache_fox_logf,
           norm_mix, norm_mlp, norm_final, gdn_w_in, gdn_conv_w, gdn_a_log, gdn_dt_bias, gdn_norm_w,
           gdn_w_out, fox_w_in, fox_b_f, fox_w_out, mlp_w_up, mlp_w_down):
    Bp, Tp, D = x_prompt.shape
    Bs, Ts, _ = x_sample.shape
    depth = norm_mix.shape[0]
    _, _, hv, dk, dv = state_gdn.shape
    conv_dim = state_gdn_conv.shape[-1]
    cw1 = state_gdn_conv.shape[2]
    v_dim = hv * dv
    qk_dim = (conv_dim - v_dim) // 2
    hqk = qk_dim // dk
    _, _, P, H, dh = cache_fox_k.shape
    fox_dim = H * dh
    d_ff = mlp_w_up.shape[-1]
    Mp, Ms = Bp * Tp, Bs * Ts
    Cp = min(GDN_CHUNK, Tp)

    x = jnp.concatenate([x_prompt.reshape(Mp, D), x_sample.reshape(Ms, D)], axis=0)
    S_p, S_s, buf_p, buf_s = [], [], [], []
    k_p, k_s, v_p, v_s, lf_p, lf_s = [], [], [], [], [], []

    for i in range(depth):
        j = i // 2
        if i % 2 == 0:
            n_main = conv_dim + v_dim
            proj = norm_proj(x, norm_mix[i], gdn_w_in, j, n_main, name="gdn_in_proj")
            beta, g = gdn_gates(x, norm_mix[i], gdn_w_in[j, :, n_main:n_main + hv],
                                gdn_w_in[j, :, n_main + hv:], gdn_a_log[j], gdn_dt_bias[j])
            conv_p = gdn_conv(proj, gdn_conv_w, None, j, row0=0, nseq=Bp, T=Tp,
                              conv_dim=conv_dim, qk_dim=qk_dim, dk=dk)
            conv_s = gdn_conv(proj, gdn_conv_w, state_gdn_conv, j, row0=Mp, nseq=Bs, T=Ts,
                              conv_dim=conv_dim, qk_dim=qk_dim, dk=dk)

            def rows(a, B, T, C):
                return a.reshape(B, T // C, C, hv).transpose(0, 3, 1, 2)

            common = dict(hv=hv, hqk=hqk, dk=dk, dv=dv, qk_dim=qk_dim, z_col0=conv_dim)
            o_p, Sp = gdn_chunks(conv_p, proj, rows(g[:Mp], Bp, Tp, Cp), rows(beta[:Mp], Bp, Tp, Cp),
                                 gdn_norm_w[j], None, j, row0=0, nseq=Bp, T=Tp, C=Cp, **common)
            o_s, Ss = gdn_chunks(conv_s, proj, rows(g[Mp:], Bs, Ts, Ts), rows(beta[Mp:], Bs, Ts, Ts),
                                 gdn_norm_w[j], state_gdn, j, row0=Mp, nseq=Bs, T=Ts, C=Ts, **common)
            x = matmul_res(jnp.concatenate([o_p, o_s], axis=0), gdn_w_out, j, x, name="gdn_out_proj")
            S_p.append(Sp)
            S_s.append(Ss)
            buf_p.append(jnp.stack([proj[(b + 1) * Tp - cw1:(b + 1) * Tp, :conv_dim] for b in range(Bp)]))
            qkv_s = proj[Mp:, :conv_dim].reshape(Bs, Ts, conv_dim)
            buf_s.append(jnp.concatenate([state_gdn_conv[j], qkv_s], axis=1)[:, -cw1:])
        else:
            proj = norm_proj(x, norm_mix[i], fox_w_in, j, 4 * fox_dim, name="fox_in_proj")
            logf = fox_logf(x, norm_mix[i], fox_w_in[j, :, 4 * fox_dim:], fox_b_f[j])
            cum_p = cumsum_time(logf[:Mp], Bp, Tp)
            o_p = fox_prompt_attention(proj, cum_p, cum_p.T, nseq=Bp, T=Tp, H=H, dh=dh)
            lf_new = logf[Mp:].reshape(Bs, Ts, H)
            lf_all = jnp.concatenate([cache_fox_logf[j], lf_new], axis=1)
            cum_s = cumsum_time(lf_all.reshape(Bs * (P + Ts), H), Bs, P + Ts).reshape(Bs, P + Ts, H)
            o_s = fox_sample_attention(proj, cache_fox_k, cache_fox_v, j, cum_s[:, P:],
                                       cum_s.transpose(0, 2, 1), row0=Mp, nseq=Bs, T=Ts, P=P, H=H, dh=dh)
            x = matmul_res(jnp.concatenate([o_p, o_s], axis=0), fox_w_out, j, x, name="fox_out_proj")
            kk = proj[:, fox_dim:2 * fox_dim]
            vv = proj[:, 2 * fox_dim:3 * fox_dim]
            k_p.append(kk[:Mp].reshape(Bp, Tp, H, dh))
            k_s.append(kk[Mp:].reshape(Bs, Ts, H, dh))
            v_p.append(vv[:Mp].reshape(Bp, Tp, H, dh))
            v_s.append(vv[Mp:].reshape(Bs, Ts, H, dh))
            lf_p.append(logf[:Mp].reshape(Bp, Tp, H))
            lf_s.append(lf_new)
        hmid = norm_proj(x, norm_mlp[i], mlp_w_up, i, d_ff, relu2=True, out_dtype=BF16, name="mlp_up")
        x = matmul_res(hmid, mlp_w_down, i, x, name="mlp_down")

    y_p = rmsnorm_rows(x, norm_final, 0, Mp)
    y_s = rmsnorm_rows(x, norm_final, Mp, Ms)
    st = jnp.stack
    return (y_p.reshape(Bp, Tp, D), y_s.reshape(Bs, Ts, D),
            st(S_p), st(buf_p), st(k_p), st(v_p), st(lf_p),
            st(S_s), st(buf_s), st(k_s), st(v_s), st(lf_s))
```

```python
import functools

import jax
import jax.numpy as jnp
from jax import lax
from jax.experimental import pallas as pl
from jax.experimental.pallas import tpu as pltpu

F32 = jnp.float32
BF16 = jnp.bfloat16
EPS = 1e-6
GDN_CHUNK = 64
GDN_HEADS_PER_STEP = 32
INV_BASE_BLOCK = 8
VMEM_LIMIT_BYTES = 56 * 1024 * 1024
LOG2E = 1.4426950408889634
SUBLANES = 8
HIGHEST = lax.Precision.HIGHEST
NT_DIMS = (((1,), (1,)), ((), ()))
TN_DIMS = (((0,), (0,)), ((), ()))


def _pick(n, pref, align):
    best = None
    for d in range(align, min(n, pref) + 1, align):
        if n % d == 0:
            best = d
    return n if best is None else best


def _params(*sem):
    return pltpu.CompilerParams(dimension_semantics=sem, vmem_limit_bytes=VMEM_LIMIT_BYTES)


def _sigmoid(x):
    return 1.0 / (1.0 + jnp.exp(-x))


def _softplus(x):
    return jnp.maximum(x, 0.0) + jnp.log1p(jnp.exp(-jnp.abs(x)))


def _rms_rows(x, w):
    ms = jnp.mean(x * x, axis=-1, keepdims=True)
    return x * lax.rsqrt(ms + EPS) * w


def _bdot(a, b):
    return jnp.dot(a.astype(BF16), b.astype(BF16), preferred_element_type=F32)


def _norm_proj_kernel(x_ref, nw_ref, w_ref, o_ref, xn_ref, *, relu2):
    @pl.when(pl.program_id(1) == 0)
    def _():
        tm = x_ref.shape[0]
        rows = _pick(tm, 272, 16)

        def body(r, carry):
            sl = pl.ds(pl.multiple_of(r * rows, rows), rows)
            xn_ref[sl, :] = _rms_rows(x_ref[sl, :], nw_ref[...]).astype(BF16)
            return carry

        lax.fori_loop(0, tm // rows, body, 0)

    acc = jnp.dot(xn_ref[...], w_ref[...].astype(BF16), preferred_element_type=F32)
    if relu2:
        acc = jnp.square(jnp.maximum(acc, 0.0))
    o_ref[...] = acc.astype(o_ref.dtype)


def norm_proj(x, nw, w_all, layer, n_cols, *, relu2=False, out_dtype=F32, name="norm_proj"):
    M, K = x.shape
    tm = _pick(M, 1088, 16)
    tn = _pick(n_cols, 512, 128)
    return pl.pallas_call(
        functools.partial(_norm_proj_kernel, relu2=relu2),
        grid=(M // tm, n_cols // tn),
        in_specs=[pl.BlockSpec((tm, K), lambda i, j: (i, 0)),
                  pl.BlockSpec((1, K), lambda i, j: (0, 0)),
                  pl.BlockSpec((None, K, tn), lambda i, j: (layer, 0, j))],
        out_specs=pl.BlockSpec((tm, tn), lambda i, j: (i, j)),
        out_shape=jax.ShapeDtypeStruct((M, n_cols), out_dtype),
        scratch_shapes=[pltpu.VMEM((tm, K), BF16)],
        compiler_params=_params("parallel", "arbitrary"),
        name=name,
    )(x, nw.reshape(1, K), w_all)


LANES = 128


def _gdn_gate_kernel(x_ref, nw_ref, w_ref, alog_ref, dtb_ref, beta_ref, g_ref, *, hv):
    xn = _rms_rows(x_ref[...], nw_ref[...]).astype(BF16)
    acc = jnp.dot(xn, w_ref[...].astype(BF16), preferred_element_type=F32)
    beta_ref[...] = _sigmoid(acc[:, :hv])
    g_ref[...] = -jnp.exp(alog_ref[...]) * _softplus(acc[:, hv:2 * hv] + dtb_ref[...])


def gdn_gates(x, nw, w_all, layer, col0, a_log, dt_bias):
    M, K = x.shape
    H = a_log.shape[0]
    assert col0 % LANES == 0 and 2 * H <= LANES
    tm = _pick(M, 544, 8)
    row = lambda i: (i, 0)
    fix = lambda i: (0, 0)
    return pl.pallas_call(
        functools.partial(_gdn_gate_kernel, hv=H),
        grid=(M // tm,),
        in_specs=[pl.BlockSpec((tm, K), row), pl.BlockSpec((1, K), fix),
                  pl.BlockSpec((None, K, LANES), lambda i: (layer, 0, col0 // LANES)),
                  pl.BlockSpec((1, H), fix), pl.BlockSpec((1, H), fix)],
        out_specs=[pl.BlockSpec((tm, H), row), pl.BlockSpec((tm, H), row)],
        out_shape=[jax.ShapeDtypeStruct((M, H), F32)] * 2,
        compiler_params=_params("parallel"),
        name="gdn_gates",
    )(x, nw.reshape(1, K), w_all, a_log.reshape(1, H), dt_bias.reshape(1, H))


def _fox_logf_kernel(x_ref, nw_ref, w_ref, bf_ref, lf_ref, *, H):
    xn = _rms_rows(x_ref[...], nw_ref[...]).astype(BF16)
    acc = jnp.dot(xn, w_ref[...].astype(BF16), preferred_element_type=F32)
    f = acc[:, :H] + bf_ref[...]
    lf_ref[...] = -_softplus(-f)


def fox_logf(x, nw, w_all, layer, col0, b_f):
    M, K = x.shape
    H = b_f.shape[0]
    assert col0 % LANES == 0 and H <= LANES
    tm = _pick(M, 544, 8)
    row = lambda i: (i, 0)
    fix = lambda i: (0, 0)
    return pl.pallas_call(
        functools.partial(_fox_logf_kernel, H=H),
        grid=(M // tm,),
        in_specs=[pl.BlockSpec((tm, K), row), pl.BlockSpec((1, K), fix),
                  pl.BlockSpec((None, K, LANES), lambda i: (layer, 0, col0 // LANES)),
                  pl.BlockSpec((1, H), fix)],
        out_specs=pl.BlockSpec((tm, H), row),
        out_shape=jax.ShapeDtypeStruct((M, H), F32),
        compiler_params=_params("parallel"),
        name="fox_logf",
    )(x, nw.reshape(1, K), w_all, b_f.reshape(1, H))


def _matmul_res_kernel(a_ref, w_ref, r_ref, o_ref, acc_ref, *, nk):
    part = jnp.dot(a_ref[...], w_ref[...].astype(BF16), preferred_element_type=F32)
    if nk == 1:
        o_ref[...] = r_ref[...] + part
        return
    k = pl.program_id(2)

    @pl.when(k == 0)
    def _():
        acc_ref[...] = part

    @pl.when(k > 0)
    def _():
        acc_ref[...] += part

    @pl.when(k == nk - 1)
    def _():
        o_ref[...] = r_ref[...] + acc_ref[...]


def matmul_res(a, w_all, layer, res, *, name="matmul_res"):
    M, K = a.shape
    N = w_all.shape[-1]
    tm = _pick(M, 1088, 16)
    tn = _pick(N, 512, 128)
    tk = _pick(K, 2048, 128)
    nk = K // tk
    return pl.pallas_call(
        functools.partial(_matmul_res_kernel, nk=nk),
        grid=(M // tm, N // tn, nk),
        in_specs=[pl.BlockSpec((tm, tk), lambda i, j, k: (i, k)),
                  pl.BlockSpec((None, tk, tn), lambda i, j, k: (layer, k, j)),
                  pl.BlockSpec((tm, tn), lambda i, j, k: (i, j))],
        out_specs=pl.BlockSpec((tm, tn), lambda i, j, k: (i, j)),
        out_shape=jax.ShapeDtypeStruct((M, N), F32),
        scratch_shapes=[pltpu.VMEM((tm, tn), F32)],
        compiler_params=_params("parallel", "parallel", "arbitrary"),
        name=name,
    )(a, w_all, res)


def _rmsnorm_kernel(x_ref, nw_ref, o_ref):
    o_ref[...] = _rms_rows(x_ref[...], nw_ref[...])


def rmsnorm_rows(x, nw, row0, n_rows):
    K = x.shape[1]
    tm = _pick(n_rows, 512, 8)
    rb0 = row0 // tm
    return pl.pallas_call(
        _rmsnorm_kernel,
        grid=(n_rows // tm,),
        in_specs=[pl.BlockSpec((tm, K), lambda i: (rb0 + i, 0)), pl.BlockSpec((1, K), lambda i: (0, 0))],
        out_specs=pl.BlockSpec((tm, K), lambda i: (i, 0)),
        out_shape=jax.ShapeDtypeStruct((n_rows, K), F32),
        compiler_params=_params("parallel"),
        name="final_rmsnorm",
    )(x, nw.reshape(1, K))


CONV_HALO = 8


def _gdn_conv_kernel(x_ref, w_ref, *rest, tb, cb, dk, n_q_blocks, n_qk_blocks, has_buf, q_scale):
    if has_buf:
        buf_ref, o_ref, xc_ref = rest
    else:
        o_ref, xc_ref = rest
    cw = w_ref.shape[0]
    lo = CONV_HALO - (cw - 1)

    @pl.when(pl.program_id(2) == 0)
    def _():
        if has_buf:
            xc_ref[lo:CONV_HALO, :] = buf_ref[0]
        else:
            xc_ref[lo:CONV_HALO, :] = jnp.zeros((cw - 1, cb), F32)

    xc_ref[CONV_HALO:CONV_HALO + tb, :] = x_ref[...]
    acc = xc_ref[lo:lo + tb, :] * w_ref[0:1, :]
    for j in range(1, cw):
        acc = acc + xc_ref[lo + j:lo + j + tb, :] * w_ref[j:j + 1, :]
    y = acc * _sigmoid(acc)
    xc_ref[lo:CONV_HALO, :] = xc_ref[lo + tb:CONV_HALO + tb, :]

    c = pl.program_id(1)

    def l2(scale):
        for hh in range(cb // dk):
            yh = y[:, hh * dk:(hh + 1) * dk]
            ss = jnp.sum(yh * yh, axis=-1, keepdims=True)
            o_ref[:, hh * dk:(hh + 1) * dk] = yh * (lax.rsqrt(ss + EPS) * scale)

    @pl.when(c < n_q_blocks)
    def _():
        l2(q_scale)

    @pl.when(jnp.logical_and(c >= n_q_blocks, c < n_qk_blocks))
    def _():
        l2(1.0)

    @pl.when(c >= n_qk_blocks)
    def _():
        o_ref[...] = y


def gdn_conv(proj, conv_w_all, buf_all, layer, *, row0, nseq, T, conv_dim, qk_dim, dk):
    cw = conv_w_all.shape[1]
    tb = _pick(T, 512, 8)
    cb = _pick(qk_dim, 1024, dk)
    nt = T // tb
    rb0 = row0 // tb
    has_buf = buf_all is not None
    in_specs = [pl.BlockSpec((tb, cb), lambda s, c, t: (rb0 + s * nt + t, c)),
                pl.BlockSpec((None, cw, cb), lambda s, c, t: (layer, 0, c))]
    args = [proj, conv_w_all]
    if has_buf:
        in_specs.append(pl.BlockSpec((None, 1, cw - 1, cb), lambda s, c, t: (layer, s, 0, c)))
        args.append(buf_all)
    return pl.pallas_call(
        functools.partial(_gdn_conv_kernel, tb=tb, cb=cb, dk=dk, n_q_blocks=qk_dim // cb,
                          n_qk_blocks=2 * qk_dim // cb, has_buf=has_buf, q_scale=dk ** -0.5),
        grid=(nseq, conv_dim // cb, nt),
        in_specs=in_specs,
        out_specs=pl.BlockSpec((tb, cb), lambda s, c, t: (s * nt + t, c)),
        out_shape=jax.ShapeDtypeStruct((nseq * T, conv_dim), F32),
        scratch_shapes=[pltpu.VMEM((CONV_HALO + tb, cb), F32)],
        compiler_params=_params("parallel", "parallel", "arbitrary"),
        name="gdn_conv",
    )(*args)


def _gdn_chunk_kernel(q_ref, k_ref, v_ref, z_ref, g_ref, b_ref, nw_ref, *rest,
                      C, hq, rep, dk, dv, n_chunks, has_s0):
    if has_s0:
        s0_ref, o_ref, sout_ref, s_ref = rest
    else:
        o_ref, sout_ref, s_ref = rest
    n = pl.program_id(2)
    C2 = 2 * C

    @pl.when(n == 0)
    def _():
        if has_s0:
            s_ref[...] = s0_ref[0]
        else:
            s_ref[...] = jnp.zeros_like(s_ref)

    row = lax.broadcasted_iota(jnp.int32, (C, C2), 0)
    lane = lax.broadcasted_iota(jnp.int32, (C, C2), 1)
    right = lane >= C
    col = jnp.where(right, lane - C, lane)
    incl = row >= col
    strict = row > col
    diag = lane == row
    eye_right = jnp.where(lane == row + C, 1.0, 0.0).astype(F32)
    tr = lax.broadcasted_iota(jnp.int32, (C, C2), 0)
    triu2 = jnp.where(tr <= col, 1.0, 0.0).astype(F32)
    zeros_z = jnp.zeros((C, C2), BF16)
    zeros_r = jnp.zeros((C, dk + dv), BF16)

    def to_col(r2):
        return jnp.sum(jnp.where(diag, r2, 0.0), axis=1, keepdims=True)

    diag_c = (lax.broadcasted_iota(jnp.int32, (C, C), 0) == lax.broadcasted_iota(jnp.int32, (C, C), 1))

    def to_col_c(r):
        return jnp.sum(jnp.where(diag_c, r, 0.0), axis=1, keepdims=True)

    heads = range(hq * rep)
    q = [q_ref[:, a * dk:(a + 1) * dk] for a in range(hq)]
    k = [k_ref[:, a * dk:(a + 1) * dk] for a in range(hq)]
    q16 = [x.astype(BF16) for x in q]
    kk16 = [jnp.concatenate([x.astype(BF16)] * 2, axis=0) for x in k]
    G2, G_col, b_col, dec_incl, gamma, G_last, kb, vb = [], [], [], [], [], [], [], []
    for i in heads:
        g_row = g_ref[0, i, pl.ds(n, 1), :]
        b_row = b_ref[0, i, pl.ds(n, 1), :]
        G2.append(jnp.dot(g_row, triu2, precision=HIGHEST, preferred_element_type=F32))
        G_col.append(to_col(G2[i]))
        b_col.append(to_col_c(b_row))
        dec_incl.append(jnp.exp(jnp.where(incl, G_col[i] - G2[i], -jnp.inf)))
        gamma.append(jnp.exp(G_col[i]))
        G_last.append(G2[i][:, C - 1:C])
        kb.append(k[i // rep] * b_col[i])
        vb.append(v_ref[:, i * dv:(i + 1) * dv] * b_col[i])
    gram = [lax.dot_general(jnp.concatenate([kb[i].astype(BF16), q16[i // rep]], axis=0), kk16[i // rep],
                            NT_DIMS, preferred_element_type=F32) for i in heads]
    Lf = [gram[i][:C] * dec_incl[i] for i in heads]
    base = min(INV_BASE_BLOCK, C)
    sh = base.bit_length() - 1
    in_base = jnp.logical_and(strict, (row >> sh) == (col >> sh))
    Z = [jnp.where(right, eye_right, -jnp.where(in_base, Lf[i], 0.0)) for i in heads]
    for _ in range((base - 1).bit_length()):
        Z16 = [z.astype(BF16) for z in Z]
        Z = [jnp.dot(Z16[i], jnp.concatenate([Z16[i], zeros_z], axis=0), preferred_element_type=F32)
             + jnp.where(right, Z[i], 0.0) for i in heads]
    b = base
    while b < C:
        sh = b.bit_length() - 1
        lower_left = jnp.logical_and(jnp.logical_and(right, (row >> (sh + 1)) == (col >> (sh + 1))),
                                     jnp.logical_and(((row >> sh) & 1) == 1, ((col >> sh) & 1) == 0))
        Z16 = [z.astype(BF16) for z in Z]
        W16 = [jnp.dot(jnp.where(lower_left, Lf[i], 0.0).astype(BF16),
                       jnp.concatenate([zeros_z, Z16[i]], axis=0),
                       preferred_element_type=F32).astype(BF16) for i in heads]
        Z = [Z[i] - jnp.dot(Z16[i], jnp.concatenate([zeros_z, W16[i]], axis=0), preferred_element_type=F32)
             for i in heads]
        b *= 2
    wu = []
    for i in heads:
        rhs = jnp.concatenate([(kb[i] * gamma[i]).astype(BF16), vb[i].astype(BF16)], axis=1)
        wu.append(jnp.dot(Z[i].astype(BF16), jnp.concatenate([zeros_r, rhs], axis=0),
                          preferred_element_type=F32))
    S = [s_ref[i] for i in heads]
    S16 = [x.astype(BF16) for x in S]
    ws_qs = [jnp.dot(jnp.concatenate([wu[i][:, :dk].astype(BF16), (q[i // rep] * gamma[i]).astype(BF16)], axis=0),
                     S16[i], preferred_element_type=F32) for i in heads]
    U16 = [(wu[i][:, dk:] - ws_qs[i][:C]).astype(BF16) for i in heads]
    A16 = [(gram[i][C:] * dec_incl[i])[:, :C].astype(BF16) for i in heads]
    o = [ws_qs[i][C:] + jnp.dot(A16[i], U16[i], preferred_element_type=F32) for i in heads]
    for i in heads:
        kd = k[i // rep] * jnp.exp(G_last[i] - G_col[i])
        s_ref[i] = jnp.exp(G_last[i]) * S[i] + lax.dot_general(kd.astype(BF16), U16[i], TN_DIMS,
                                                               preferred_element_type=F32)
    for i in heads:
        z = z_ref[:, i * dv:(i + 1) * dv]
        o_ref[:, i * dv:(i + 1) * dv] = (_rms_rows(o[i], nw_ref[...]) * (z * _sigmoid(z))).astype(o_ref.dtype)

    @pl.when(n == n_chunks - 1)
    def _():
        sout_ref[0] = s_ref[...]


def gdn_chunks(conv, proj, g_rows, b_rows, norm_w, s0_all, layer, *, row0, nseq, T, C, hv, hqk, dk, dv,
               qk_dim, z_col0):
    N = T // C
    rep = hv // hqk
    hb = min(GDN_HEADS_PER_STEP, hv)
    hq = hb // rep
    has_s0 = s0_all is not None
    kb0 = qk_dim // (hq * dk)
    vb0 = 2 * qk_dim // (hb * dv)
    zb0 = z_col0 // (hb * dv)
    rz0 = row0 // C
    in_specs = [pl.BlockSpec((C, hq * dk), lambda b, h, n: (b * N + n, h)),
                pl.BlockSpec((C, hq * dk), lambda b, h, n: (b * N + n, kb0 + h)),
                pl.BlockSpec((C, hb * dv), lambda b, h, n: (b * N + n, vb0 + h)),
                pl.BlockSpec((C, hb * dv), lambda b, h, n: (rz0 + b * N + n, zb0 + h)),
                pl.BlockSpec((1, hb, N, C), lambda b, h, n: (b, h, 0, 0)),
                pl.BlockSpec((1, hb, N, C), lambda b, h, n: (b, h, 0, 0)),
                pl.BlockSpec((1, dv), lambda b, h, n: (0, 0))]
    args = [conv, conv, conv, proj, g_rows, b_rows, norm_w.reshape(1, dv)]
    if has_s0:
        in_specs.append(pl.BlockSpec((None, 1, hb, dk, dv), lambda b, h, n: (layer, b, h, 0, 0)))
        args.append(s0_all)
    return pl.pallas_call(
        functools.partial(_gdn_chunk_kernel, C=C, hq=hq, rep=rep, dk=dk, dv=dv, n_chunks=N,
                          has_s0=has_s0),
        grid=(nseq, hv // hb, N),
        in_specs=in_specs,
        out_specs=[pl.BlockSpec((C, hb * dv), lambda b, h, n: (b * N + n, h)),
                   pl.BlockSpec((1, hb, dk, dv), lambda b, h, n: (b, h, 0, 0))],
        out_shape=[jax.ShapeDtypeStruct((nseq * T, hv * dv), BF16),
                   jax.ShapeDtypeStruct((nseq, hv, dk, dv), F32)],
        scratch_shapes=[pltpu.VMEM((hb, dk, dv), F32)],
        compiler_params=_params("parallel", "parallel", "arbitrary"),
        name="gdn_chunks",
    )(*args)


def _cumsum_kernel(x_ref, o_ref, carry_ref, *, tb):
    @pl.when(pl.program_id(1) == 0)
    def _():
        carry_ref[...] = jnp.zeros_like(carry_ref)

    row = lax.broadcasted_iota(jnp.int32, (tb, tb), 0)
    col = lax.broadcasted_iota(jnp.int32, (tb, tb), 1)
    tril = jnp.where(row >= col, 1.0, 0.0).astype(F32)
    c = jnp.dot(tril, x_ref[...], precision=HIGHEST, preferred_element_type=F32) + carry_ref[...]
    o_ref[...] = c
    carry_ref[...] = c[tb - 1:tb, :]


def cumsum_time(x, nseq, L):
    H = x.shape[1]
    tb = _pick(L, 256, 8)
    nt = L // tb
    return pl.pallas_call(
        functools.partial(_cumsum_kernel, tb=tb),
        grid=(nseq, nt),
        in_specs=[pl.BlockSpec((tb, H), lambda s, t: (s * nt + t, 0))],
        out_specs=pl.BlockSpec((tb, H), lambda s, t: (s * nt + t, 0)),
        out_shape=jax.ShapeDtypeStruct((nseq * L, H), F32),
        scratch_shapes=[pltpu.VMEM((1, H), F32)],
        compiler_params=_params("parallel", "arbitrary"),
        name="fox_cumsum",
    )(x)


def _fox_prompt_kernel(q_ref, k_ref, v_ref, og_ref, cq_ref, ckt_ref, o_ref, acc_ref, m_ref, cqr_ref,
                       *, bq, H, dh, scale, hg):
    i = pl.program_id(1)
    j = pl.program_id(2)
    rep = bq // dh

    @pl.when(j == 0)
    def _():
        m_ref[...] = jnp.full(m_ref.shape, -jnp.inf, F32)
        acc_ref[...] = jnp.zeros_like(acc_ref)
        cqs = cq_ref[...] * LOG2E
        for h in range(H):
            cqr_ref[h] = jnp.broadcast_to(cqs[:, h:h + 1], (bq, dh))

    def block(masked):
        if masked:
            mask = lax.broadcasted_iota(jnp.int32, (bq, bq), 1) <= lax.broadcasted_iota(jnp.int32, (bq, bq), 0)
        ones16 = jnp.ones((bq, dh), BF16)
        ck2 = ckt_ref[...] * LOG2E

        def qk(h):
            hs = slice(h * dh, (h + 1) * dh)
            return lax.dot_general((q_ref[:, hs] * (scale * LOG2E)).astype(BF16), k_ref[:, hs].astype(BF16),
                                   NT_DIMS, preferred_element_type=F32)

        groups = [range(g * hg, (g + 1) * hg) for g in range(H // hg)]
        s = {h: qk(h) for h in groups[0]}
        for gi, grp in enumerate(groups):
            if gi + 1 < len(groups):
                for h in groups[gi + 1]:
                    s[h] = qk(h)
            p16, alpha2 = {}, {}
            for h in grp:
                t = s.pop(h) - ck2[h:h + 1, :]
                if masked:
                    t = jnp.where(mask, t, -jnp.inf)
                cq2 = cqr_ref[h]
                m_prev = m_ref[h]
                m_new = jnp.maximum(m_prev, jnp.max(t, axis=1, keepdims=True) + cq2)
                m_ref[h] = m_new
                alpha = jnp.exp2(m_prev - m_new)
                c = cq2 - m_new
                p16[h] = jnp.exp2(t + jnp.concatenate([c] * rep, axis=1)).astype(BF16)
                alpha2[h] = jnp.concatenate([alpha, alpha], axis=1)
            for h in grp:
                hs = slice(h * dh, (h + 1) * dh)
                v_ext = jnp.concatenate([v_ref[:, hs].astype(BF16), ones16], axis=1)
                acc_ref[h] = alpha2[h] * acc_ref[h] + jnp.dot(p16[h], v_ext, preferred_element_type=F32)

    @pl.when(j < i)
    def _():
        block(False)

    @pl.when(j == i)
    def _():
        block(True)
        for h in range(H):
            hs = slice(h * dh, (h + 1) * dh)
            a = acc_ref[h]
            o_ref[:, hs] = ((a[:, :dh] / a[:, dh:]) * _sigmoid(og_ref[:, hs])).astype(o_ref.dtype)


def fox_prompt_attention(proj, cum, cum_t, *, nseq, T, H, dh):
    bq = _pick(T, 256, 128)
    nq = T // bq
    D = H * dh
    hg = _pick(H, 4, 1)
    kv = lambda col: (lambda b, i, j: (b * nq + jnp.minimum(j, i), col))
    return pl.pallas_call(
        functools.partial(_fox_prompt_kernel, bq=bq, H=H, dh=dh, scale=dh ** -0.5, hg=hg),
        grid=(nseq, nq, nq),
        in_specs=[pl.BlockSpec((bq, D), lambda b, i, j: (b * nq + i, 0)),
                  pl.BlockSpec((bq, D), kv(1)),
                  pl.BlockSpec((bq, D), kv(2)),
                  pl.BlockSpec((bq, D), lambda b, i, j: (b * nq + i, 3)),
                  pl.BlockSpec((bq, H), lambda b, i, j: (b * nq + i, 0)),
                  pl.BlockSpec((H, bq), lambda b, i, j: (0, b * nq + jnp.minimum(j, i)))],
        out_specs=pl.BlockSpec((bq, D), lambda b, i, j: (b * nq + i, 0)),
        out_shape=jax.ShapeDtypeStruct((nseq * T, D), BF16),
        scratch_shapes=[pltpu.VMEM((H, bq, 2 * dh), F32), pltpu.VMEM((H, bq, dh), F32),
                        pltpu.VMEM((H, bq, dh), F32)],
        compiler_params=_params("parallel", "parallel", "arbitrary"),
        name="fox_prompt_attention",
    )(proj, proj, proj, proj, cum, cum_t)


def _fox_sample_kernel(q_ref, kn_ref, vn_ref, og_ref, kc_ref, vc_ref, cq_ref, ck_ref, o_ref,
                       *, P, T, G, dh, scale):
    R, PG = T * G, P * G
    gs = G.bit_length() - 1
    q = (q_ref[0, 0] * scale).astype(BF16)
    kc = kc_ref[0].reshape(PG, dh).astype(BF16)
    vc = vc_ref[0].reshape(PG, dh).astype(BF16)
    s_p = lax.dot_general(q, kc, NT_DIMS, preferred_element_type=F32)
    s_n = lax.dot_general(q, kn_ref[0, 0].astype(BF16), NT_DIMS, preferred_element_type=F32)
    cq = cq_ref[0, 0]
    ck = ck_ref[0, 0]
    same_p = ((lax.broadcasted_iota(jnp.int32, (R, PG), 0) & (G - 1))
              == (lax.broadcasted_iota(jnp.int32, (R, PG), 1) & (G - 1)))
    rn = lax.broadcasted_iota(jnp.int32, (R, R), 0)
    cn = lax.broadcasted_iota(jnp.int32, (R, R), 1)
    same_n = jnp.logical_and((rn & (G - 1)) == (cn & (G - 1)), (cn >> gs) <= (rn >> gs))
    s_p = jnp.where(same_p, s_p + (cq - ck[:, :PG]), -jnp.inf)
    s_n = jnp.where(same_n, s_n + (cq - ck[:, PG:]), -jnp.inf)
    m = jnp.maximum(jnp.max(s_p, axis=1, keepdims=True), jnp.max(s_n, axis=1, keepdims=True))
    p_p = jnp.exp(s_p - m)
    p_n = jnp.exp(s_n - m)
    l = jnp.sum(p_p, axis=1, keepdims=True) + jnp.sum(p_n, axis=1, keepdims=True)
    o = (jnp.dot(p_p.astype(BF16), vc, preferred_element_type=F32)
         + jnp.dot(p_n.astype(BF16), vn_ref[0, 0].astype(BF16), preferred_element_type=F32))
    o_ref[0, 0] = ((o / l) * _sigmoid(og_ref[0, 0])).astype(o_ref.dtype)


def fox_sample_attention(proj, k_cache_all, v_cache_all, layer, cum, *, row0, nseq, T, P, H, dh):
    G = SUBLANES if H % SUBLANES == 0 else H
    assert G & (G - 1) == 0
    NG = H // G
    R = T * G
    new = proj[row0:row0 + nseq * T].reshape(nseq, T, 4, NG, G, dh).transpose(2, 0, 3, 1, 4, 5)
    new = new.reshape(4, nseq, NG, R, dh)
    ck = cum.reshape(nseq, P + T, NG, G).transpose(0, 2, 1, 3).reshape(nseq, NG, 1, (P + T) * G)
    cq = cum[:, P:].reshape(nseq, T, NG, G).transpose(0, 2, 1, 3).reshape(nseq, NG, R, 1)
    part = lambda a: pl.BlockSpec((None, 1, 1, R, dh), lambda b, g: (a, b, g, 0, 0))
    cache = pl.BlockSpec((None, 1, P, G, dh), lambda b, g: (layer, b, 0, g, 0))
    out = pl.pallas_call(
        functools.partial(_fox_sample_kernel, P=P, T=T, G=G, dh=dh, scale=dh ** -0.5),
        grid=(nseq, NG),
        in_specs=[part(0), part(1), part(2), part(3), cache, cache,
                  pl.BlockSpec((1, 1, R, 1), lambda b, g: (b, g, 0, 0)),
                  pl.BlockSpec((1, 1, 1, (P + T) * G), lambda b, g: (b, g, 0, 0))],
        out_specs=pl.BlockSpec((1, 1, R, dh), lambda b, g: (b, g, 0, 0)),
        out_shape=jax.ShapeDtypeStruct((nseq, NG, R, dh), BF16),
        compiler_params=_params("parallel", "parallel"),
        name="fox_sample_attention",
    )(new, new, new, new, k_cache_all, v_cache_all, cq, ck)
    return out.reshape(nseq, NG, T, G, dh).transpose(0, 2, 1, 3, 4).reshape(nseq * T, H * dh)


def kernel(x_prompt, x_sample, state_gdn, state_gdn_conv, cache_fox_k, cache_fox_v, cache_fox_logf,
           norm_mix, norm_mlp, norm_final, gdn_w_in, gdn_conv_w, gdn_a_log, gdn_dt_bias, gdn_norm_w,
           gdn_w_out, fox_w_in, fox_b_f, fox_w_out, mlp_w_up, mlp_w_down):
    Bp, Tp, D = x_prompt.shape
    Bs, Ts, _ = x_sample.shape
    depth = norm_mix.shape[0]
    _, _, hv, dk, dv = state_gdn.shape
    conv_dim = state_gdn_conv.shape[-1]
    cw1 = state_gdn_conv.shape[2]
    v_dim = hv * dv
    qk_dim = (conv_dim - v_dim) // 2
    hqk = qk_dim // dk
    _, _, P, H, dh = cache_fox_k.shape
    fox_dim = H * dh
    d_ff = mlp_w_up.shape[-1]
    Mp, Ms = Bp * Tp, Bs * Ts
    Cp = min(GDN_CHUNK, Tp)

    x = jnp.concatenate([x_prompt.reshape(Mp, D), x_sample.reshape(Ms, D)], axis=0)
    S_p, S_s, buf_p, buf_s = [], [], [], []
    k_p, k_s, v_p, v_s, lf_p, lf_s = [], [], [], [], [], []

    for i in range(depth):
        j = i // 2
        if i % 2 == 0:
            n_main = conv_dim + v_dim
            proj = norm_proj(x, norm_mix[i], gdn_w_in, j, n_main, name="gdn_in_proj")
            beta, g = gdn_gates(x, norm_mix[i], gdn_w_in, j, n_main, gdn_a_log[j], gdn_dt_bias[j])
            conv_p = gdn_conv(proj, gdn_conv_w, None, j, row0=0, nseq=Bp, T=Tp,
                              conv_dim=conv_dim, qk_dim=qk_dim, dk=dk)
            conv_s = gdn_conv(proj, gdn_conv_w, state_gdn_conv, j, row0=Mp, nseq=Bs, T=Ts,
                              conv_dim=conv_dim, qk_dim=qk_dim, dk=dk)

            def rows(a, B, T, C):
                return a.reshape(B, T // C, C, hv).transpose(0, 3, 1, 2)

            common = dict(hv=hv, hqk=hqk, dk=dk, dv=dv, qk_dim=qk_dim, z_col0=conv_dim)
            o_p, Sp = gdn_chunks(conv_p, proj, rows(g[:Mp], Bp, Tp, Cp), rows(beta[:Mp], Bp, Tp, Cp),
                                 gdn_norm_w[j], None, j, row0=0, nseq=Bp, T=Tp, C=Cp, **common)
            o_s, Ss = gdn_chunks(conv_s, proj, rows(g[Mp:], Bs, Ts, Ts), rows(beta[Mp:], Bs, Ts, Ts),
                                 gdn_norm_w[j], state_gdn, j, row0=Mp, nseq=Bs, T=Ts, C=Ts, **common)
            x = matmul_res(jnp.concatenate([o_p, o_s], axis=0), gdn_w_out, j, x, name="gdn_out_proj")
            S_p.append(Sp)
            S_s.append(Ss)
            buf_p.append(jnp.stack([proj[(b + 1) * Tp - cw1:(b + 1) * Tp, :conv_dim] for b in range(Bp)]))
            qkv_s = proj[Mp:, :conv_dim].reshape(Bs, Ts, conv_dim)
            buf_s.append(jnp.concatenate([state_gdn_conv[j], qkv_s], axis=1)[:, -cw1:])
        else:
            proj = norm_proj(x, norm_mix[i], fox_w_in, j, 4 * fox_dim, name="fox_in_proj")
            logf = fox_logf(x, norm_mix[i], fox_w_in, j, 4 * fox_dim, fox_b_f[j])
            cum_p = cumsum_time(logf[:Mp], Bp, Tp)
            o_p = fox_prompt_attention(proj, cum_p, cum_p.T, nseq=Bp, T=Tp, H=H, dh=dh)
            lf_new = logf[Mp:].reshape(Bs, Ts, H)
            lf_all = jnp.concatenate([cache_fox_logf[j], lf_new], axis=1)
            cum_s = cumsum_time(lf_all.reshape(Bs * (P + Ts), H), Bs, P + Ts).reshape(Bs, P + Ts, H)
            o_s = fox_sample_attention(proj, cache_fox_k, cache_fox_v, j, cum_s,
                                       row0=Mp, nseq=Bs, T=Ts, P=P, H=H, dh=dh)
            x = matmul_res(jnp.concatenate([o_p, o_s], axis=0), fox_w_out, j, x, name="fox_out_proj")
            kk = proj[:, fox_dim:2 * fox_dim]
            vv = proj[:, 2 * fox_dim:3 * fox_dim]
            k_p.append(kk[:Mp].reshape(Bp, Tp, H, dh))
            k_s.append(kk[Mp:].reshape(Bs, Ts, H, dh))
            v_p.append(vv[:Mp].reshape(Bp, Tp, H, dh))
            v_s.append(vv[Mp:].reshape(Bs, Ts, H, dh))
            lf_p.append(logf[:Mp].reshape(Bp, Tp, H))
            lf_s.append(lf_new)
        hmid = norm_proj(x, norm_mlp[i], mlp_w_up, i, d_ff, relu2=True, out_dtype=BF16, name="mlp_up")
        x = matmul_res(hmid, mlp_w_down, i, x, name="mlp_down")

    y_p = rmsnorm_rows(x, norm_final, 0, Mp)
    y_s = rmsnorm_rows(x, norm_final, Mp, Ms)
    st = jnp.stack
    return (y_p.reshape(Bp, Tp, D), y_s.reshape(Bs, Ts, D),
            st(S_p), st(buf_p), st(k_p), st(v_p), st(lf_p),
            st(S_s), st(buf_s), st(k_s), st(v_s), st(lf_s))
```

```python
import functools

import jax
import jax.numpy as jnp
from jax import lax
from jax.experimental import pallas as pl
from jax.experimental.pallas import tpu as pltpu

F32 = jnp.float32
BF16 = jnp.bfloat16
EPS = 1e-6
GDN_CHUNK = 64
GDN_HEADS_PER_STEP = 32
INV_BASE_BLOCK = 8
VMEM_LIMIT_BYTES = 58 * 1024 * 1024
LOG2E = 1.4426950408889634
SUBLANES = 8
HIGHEST = lax.Precision.HIGHEST
NT_DIMS = (((1,), (1,)), ((), ()))
TN_DIMS = (((0,), (0,)), ((), ()))


def _pick(n, pref, align):
    best = None
    for d in range(align, min(n, pref) + 1, align):
        if n % d == 0:
            best = d
    return n if best is None else best


def _params(*sem):
    return pltpu.CompilerParams(dimension_semantics=sem, vmem_limit_bytes=VMEM_LIMIT_BYTES)


def _sigmoid(x):
    return 1.0 / (1.0 + jnp.exp(-x))


def _softplus(x):
    return jnp.maximum(x, 0.0) + jnp.log1p(jnp.exp(-jnp.abs(x)))


def _rms_rows(x, w):
    ms = jnp.mean(x * x, axis=-1, keepdims=True)
    return x * lax.rsqrt(ms + EPS) * w


def _bdot(a, b):
    return jnp.dot(a.astype(BF16), b.astype(BF16), preferred_element_type=F32)


def _norm_proj_kernel(x_ref, nw_ref, w_ref, o_ref, xn_ref, *, relu2):
    @pl.when(pl.program_id(1) == 0)
    def _():
        tm = x_ref.shape[0]
        rows = _pick(tm, 272, 16)

        def body(r, carry):
            sl = pl.ds(pl.multiple_of(r * rows, rows), rows)
            xn_ref[sl, :] = _rms_rows(x_ref[sl, :], nw_ref[...]).astype(BF16)
            return carry

        lax.fori_loop(0, tm // rows, body, 0)

    acc = jnp.dot(xn_ref[...], w_ref[...].astype(BF16), preferred_element_type=F32)
    if relu2:
        acc = jnp.square(jnp.maximum(acc, 0.0))
    o_ref[...] = acc.astype(o_ref.dtype)


def norm_proj(x, nw, w_all, layer, n_cols, *, relu2=False, name="norm_proj"):
    M, K = x.shape
    tm = _pick(M, 1088, 16)
    tn = _pick(n_cols, 1024, 128)
    return pl.pallas_call(
        functools.partial(_norm_proj_kernel, relu2=relu2),
        grid=(M // tm, n_cols // tn),
        in_specs=[pl.BlockSpec((tm, K), lambda i, j: (i, 0)),
                  pl.BlockSpec((1, K), lambda i, j: (0, 0)),
                  pl.BlockSpec((None, K, tn), lambda i, j: (layer, 0, j))],
        out_specs=pl.BlockSpec((tm, tn), lambda i, j: (i, j)),
        out_shape=jax.ShapeDtypeStruct((M, n_cols), BF16),
        scratch_shapes=[pltpu.VMEM((tm, K), BF16)],
        compiler_params=_params("parallel", "arbitrary"),
        name=name,
    )(x, nw.reshape(1, K), w_all)


LANES = 128


def _gdn_gate_kernel(x_ref, nw_ref, w_ref, alog_ref, dtb_ref, beta_ref, g_ref, *, hv):
    xn = _rms_rows(x_ref[...], nw_ref[...]).astype(BF16)
    acc = jnp.dot(xn, w_ref[...].astype(BF16), preferred_element_type=F32)
    beta_ref[...] = _sigmoid(acc[:, :hv])
    g_ref[...] = -jnp.exp(alog_ref[...]) * _softplus(acc[:, hv:2 * hv] + dtb_ref[...])


def gdn_gates(x, nw, w_all, layer, col0, a_log, dt_bias):
    M, K = x.shape
    H = a_log.shape[0]
    assert col0 % LANES == 0 and 2 * H <= LANES
    tm = _pick(M, 544, 8)
    row = lambda i: (i, 0)
    fix = lambda i: (0, 0)
    return pl.pallas_call(
        functools.partial(_gdn_gate_kernel, hv=H),
        grid=(M // tm,),
        in_specs=[pl.BlockSpec((tm, K), row), pl.BlockSpec((1, K), fix),
                  pl.BlockSpec((None, K, LANES), lambda i: (layer, 0, col0 // LANES)),
                  pl.BlockSpec((1, H), fix), pl.BlockSpec((1, H), fix)],
        out_specs=[pl.BlockSpec((tm, H), row), pl.BlockSpec((tm, H), row)],
        out_shape=[jax.ShapeDtypeStruct((M, H), F32)] * 2,
        compiler_params=_params("parallel"),
        name="gdn_gates",
    )(x, nw.reshape(1, K), w_all, a_log.reshape(1, H), dt_bias.reshape(1, H))


def _fox_logf_kernel(x_ref, nw_ref, w_ref, bf_ref, lf_ref, *, H):
    xn = _rms_rows(x_ref[...], nw_ref[...]).astype(BF16)
    acc = jnp.dot(xn, w_ref[...].astype(BF16), preferred_element_type=F32)
    f = acc[:, :H] + bf_ref[...]
    lf_ref[...] = -_softplus(-f)


def fox_logf(x, nw, w_all, layer, col0, b_f):
    M, K = x.shape
    H = b_f.shape[0]
    assert col0 % LANES == 0 and H <= LANES
    tm = _pick(M, 544, 8)
    row = lambda i: (i, 0)
    fix = lambda i: (0, 0)
    return pl.pallas_call(
        functools.partial(_fox_logf_kernel, H=H),
        grid=(M // tm,),
        in_specs=[pl.BlockSpec((tm, K), row), pl.BlockSpec((1, K), fix),
                  pl.BlockSpec((None, K, LANES), lambda i: (layer, 0, col0 // LANES)),
                  pl.BlockSpec((1, H), fix)],
        out_specs=pl.BlockSpec((tm, H), row),
        out_shape=jax.ShapeDtypeStruct((M, H), F32),
        compiler_params=_params("parallel"),
        name="fox_logf",
    )(x, nw.reshape(1, K), w_all, b_f.reshape(1, H))


def _matmul_res_kernel(a_ref, w_ref, r_ref, o_ref, acc_ref, *, nk):
    part = jnp.dot(a_ref[...], w_ref[...].astype(BF16), preferred_element_type=F32)
    if nk == 1:
        o_ref[...] = r_ref[...] + part
        return
    k = pl.program_id(2)

    @pl.when(k == 0)
    def _():
        acc_ref[...] = part

    @pl.when(k > 0)
    def _():
        acc_ref[...] += part

    @pl.when(k == nk - 1)
    def _():
        o_ref[...] = r_ref[...] + acc_ref[...]


def matmul_res(a, w_all, layer, res, *, name="matmul_res"):
    M, K = a.shape
    N = w_all.shape[-1]
    tm = _pick(M, 1088, 16)
    tn = _pick(N, 512, 128)
    tk = _pick(K, 4096, 128)
    nk = K // tk
    return pl.pallas_call(
        functools.partial(_matmul_res_kernel, nk=nk),
        grid=(M // tm, N // tn, nk),
        in_specs=[pl.BlockSpec((tm, tk), lambda i, j, k: (i, k)),
                  pl.BlockSpec((None, tk, tn), lambda i, j, k: (layer, k, j)),
                  pl.BlockSpec((tm, tn), lambda i, j, k: (i, j))],
        out_specs=pl.BlockSpec((tm, tn), lambda i, j, k: (i, j)),
        out_shape=jax.ShapeDtypeStruct((M, N), F32),
        scratch_shapes=[pltpu.VMEM((tm, tn), F32)],
        compiler_params=_params("parallel", "parallel", "arbitrary"),
        name=name,
    )(a, w_all, res)


def _rmsnorm_kernel(x_ref, nw_ref, o_ref):
    o_ref[...] = _rms_rows(x_ref[...], nw_ref[...])


def rmsnorm_rows(x, nw, row0, n_rows):
    K = x.shape[1]
    tm = _pick(n_rows, 512, 8)
    rb0 = row0 // tm
    return pl.pallas_call(
        _rmsnorm_kernel,
        grid=(n_rows // tm,),
        in_specs=[pl.BlockSpec((tm, K), lambda i: (rb0 + i, 0)), pl.BlockSpec((1, K), lambda i: (0, 0))],
        out_specs=pl.BlockSpec((tm, K), lambda i: (i, 0)),
        out_shape=jax.ShapeDtypeStruct((n_rows, K), F32),
        compiler_params=_params("parallel"),
        name="final_rmsnorm",
    )(x, nw.reshape(1, K))


CONV_HALO = 8


def _gdn_conv_kernel(x_ref, w_ref, *rest, tb, cb, dk, n_q_blocks, n_qk_blocks, has_buf, q_scale):
    if has_buf:
        buf_ref, o_ref, xc_ref = rest
    else:
        o_ref, xc_ref = rest
    cw = w_ref.shape[0]
    lo = CONV_HALO - (cw - 1)

    @pl.when(pl.program_id(2) == 0)
    def _():
        if has_buf:
            xc_ref[lo:CONV_HALO, :] = buf_ref[0]
        else:
            xc_ref[lo:CONV_HALO, :] = jnp.zeros((cw - 1, cb), F32)

    xc_ref[CONV_HALO:CONV_HALO + tb, :] = x_ref[...].astype(F32)
    acc = xc_ref[lo:lo + tb, :] * w_ref[0:1, :]
    for j in range(1, cw):
        acc = acc + xc_ref[lo + j:lo + j + tb, :] * w_ref[j:j + 1, :]
    y = acc * _sigmoid(acc)
    xc_ref[lo:CONV_HALO, :] = xc_ref[lo + tb:CONV_HALO + tb, :]

    c = pl.program_id(1)

    def l2(scale):
        for hh in range(cb // dk):
            yh = y[:, hh * dk:(hh + 1) * dk]
            ss = jnp.sum(yh * yh, axis=-1, keepdims=True)
            o_ref[:, hh * dk:(hh + 1) * dk] = yh * (lax.rsqrt(ss + EPS) * scale)

    @pl.when(c < n_q_blocks)
    def _():
        l2(q_scale)

    @pl.when(jnp.logical_and(c >= n_q_blocks, c < n_qk_blocks))
    def _():
        l2(1.0)

    @pl.when(c >= n_qk_blocks)
    def _():
        o_ref[...] = y


def gdn_conv(proj, conv_w_all, buf_all, layer, *, row0, nseq, T, conv_dim, qk_dim, dk):
    cw = conv_w_all.shape[1]
    tb = _pick(T, 512, 8)
    cb = _pick(qk_dim, 1024, dk)
    nt = T // tb
    rb0 = row0 // tb
    has_buf = buf_all is not None
    in_specs = [pl.BlockSpec((tb, cb), lambda s, c, t: (rb0 + s * nt + t, c)),
                pl.BlockSpec((None, cw, cb), lambda s, c, t: (layer, 0, c))]
    args = [proj, conv_w_all]
    if has_buf:
        in_specs.append(pl.BlockSpec((None, 1, cw - 1, cb), lambda s, c, t: (layer, s, 0, c)))
        args.append(buf_all)
    return pl.pallas_call(
        functools.partial(_gdn_conv_kernel, tb=tb, cb=cb, dk=dk, n_q_blocks=qk_dim // cb,
                          n_qk_blocks=2 * qk_dim // cb, has_buf=has_buf, q_scale=dk ** -0.5),
        grid=(nseq, conv_dim // cb, nt),
        in_specs=in_specs,
        out_specs=pl.BlockSpec((tb, cb), lambda s, c, t: (s * nt + t, c)),
        out_shape=jax.ShapeDtypeStruct((nseq * T, conv_dim), F32),
        scratch_shapes=[pltpu.VMEM((CONV_HALO + tb, cb), F32)],
        compiler_params=_params("parallel", "parallel", "arbitrary"),
        name="gdn_conv",
    )(*args)


def _gdn_chunk_kernel(q_ref, k_ref, v_ref, z_ref, g_ref, b_ref, nw_ref, *rest,
                      C, hq, rep, dk, dv, n_chunks, has_s0):
    if has_s0:
        s0_ref, o_ref, sout_ref, s_ref = rest
    else:
        o_ref, sout_ref, s_ref = rest
    n = pl.program_id(2)
    C2 = 2 * C

    @pl.when(n == 0)
    def _():
        if has_s0:
            s_ref[...] = s0_ref[0]
        else:
            s_ref[...] = jnp.zeros_like(s_ref)

    row = lax.broadcasted_iota(jnp.int32, (C, C2), 0)
    lane = lax.broadcasted_iota(jnp.int32, (C, C2), 1)
    right = lane >= C
    col = jnp.where(right, lane - C, lane)
    incl = row >= col
    strict = row > col
    diag = lane == row
    eye_right = jnp.where(lane == row + C, 1.0, 0.0).astype(F32)
    tr = lax.broadcasted_iota(jnp.int32, (C, C2), 0)
    triu2 = jnp.where(tr <= col, 1.0, 0.0).astype(F32)
    zeros_z = jnp.zeros((C, C2), BF16)
    zeros_r = jnp.zeros((C, dk + dv), BF16)

    def to_col(r2):
        return jnp.sum(jnp.where(diag, r2, 0.0), axis=1, keepdims=True)

    diag_c = (lax.broadcasted_iota(jnp.int32, (C, C), 0) == lax.broadcasted_iota(jnp.int32, (C, C), 1))

    def to_col_c(r):
        return jnp.sum(jnp.where(diag_c, r, 0.0), axis=1, keepdims=True)

    heads = range(hq * rep)
    q = [q_ref[:, a * dk:(a + 1) * dk] for a in range(hq)]
    k = [k_ref[:, a * dk:(a + 1) * dk] for a in range(hq)]
    q16 = [x.astype(BF16) for x in q]
    kk16 = [jnp.concatenate([x.astype(BF16)] * 2, axis=0) for x in k]
    G2, G_col, b_col, dec_incl, gamma, G_last, kb, vb = [], [], [], [], [], [], [], []
    for i in heads:
        g_row = g_ref[0, i, pl.ds(n, 1), :]
        b_row = b_ref[0, i, pl.ds(n, 1), :]
        G2.append(jnp.dot(g_row, triu2, precision=HIGHEST, preferred_element_type=F32))
        G_col.append(to_col(G2[i]))
        b_col.append(to_col_c(b_row))
        dec_incl.append(jnp.exp(jnp.where(incl, G_col[i] - G2[i], -jnp.inf)))
        gamma.append(jnp.exp(G_col[i]))
        G_last.append(G2[i][:, C - 1:C])
        kb.append(k[i // rep] * b_col[i])
        vb.append(v_ref[:, i * dv:(i + 1) * dv] * b_col[i])
    gram = [lax.dot_general(jnp.concatenate([kb[i].astype(BF16), q16[i // rep]], axis=0), kk16[i // rep],
                            NT_DIMS, preferred_element_type=F32) for i in heads]
    Lf = [gram[i][:C] * dec_incl[i] for i in heads]
    base = min(INV_BASE_BLOCK, C)
    sh = base.bit_length() - 1
    in_base = jnp.logical_and(strict, (row >> sh) == (col >> sh))
    Z = [jnp.where(right, eye_right, -jnp.where(in_base, Lf[i], 0.0)) for i in heads]
    for _ in range((base - 1).bit_length()):
        Z16 = [z.astype(BF16) for z in Z]
        Z = [jnp.dot(Z16[i], jnp.concatenate([Z16[i], zeros_z], axis=0), preferred_element_type=F32)
             + jnp.where(right, Z[i], 0.0) for i in heads]
    b = base
    while b < C:
        sh = b.bit_length() - 1
        lower_left = jnp.logical_and(jnp.logical_and(right, (row >> (sh + 1)) == (col >> (sh + 1))),
                                     jnp.logical_and(((row >> sh) & 1) == 1, ((col >> sh) & 1) == 0))
        Z16 = [z.astype(BF16) for z in Z]
        W16 = [jnp.dot(jnp.where(lower_left, Lf[i], 0.0).astype(BF16),
                       jnp.concatenate([zeros_z, Z16[i]], axis=0),
                       preferred_element_type=F32).astype(BF16) for i in heads]
        Z = [Z[i] - jnp.dot(Z16[i], jnp.concatenate([zeros_z, W16[i]], axis=0), preferred_element_type=F32)
             for i in heads]
        b *= 2
    wu = []
    for i in heads:
        rhs = jnp.concatenate([(kb[i] * gamma[i]).astype(BF16), vb[i].astype(BF16)], axis=1)
        wu.append(jnp.dot(Z[i].astype(BF16), jnp.concatenate([zeros_r, rhs], axis=0),
                          preferred_element_type=F32))
    S = [s_ref[i] for i in heads]
    S16 = [x.astype(BF16) for x in S]
    ws_qs = [jnp.dot(jnp.concatenate([wu[i][:, :dk].astype(BF16), (q[i // rep] * gamma[i]).astype(BF16)], axis=0),
                     S16[i], preferred_element_type=F32) for i in heads]
    U16 = [(wu[i][:, dk:] - ws_qs[i][:C]).astype(BF16) for i in heads]
    A16 = [(gram[i][C:] * dec_incl[i])[:, :C].astype(BF16) for i in heads]
    o = [ws_qs[i][C:] + jnp.dot(A16[i], U16[i], preferred_element_type=F32) for i in heads]
    for i in heads:
        kd = k[i // rep] * jnp.exp(G_last[i] - G_col[i])
        s_ref[i] = jnp.exp(G_last[i]) * S[i] + lax.dot_general(kd.astype(BF16), U16[i], TN_DIMS,
                                                               preferred_element_type=F32)
    for i in heads:
        z = z_ref[:, i * dv:(i + 1) * dv].astype(F32)
        o_ref[:, i * dv:(i + 1) * dv] = (_rms_rows(o[i], nw_ref[...]) * (z * _sigmoid(z))).astype(o_ref.dtype)

    @pl.when(n == n_chunks - 1)
    def _():
        sout_ref[0] = s_ref[...]


def gdn_chunks(conv, proj, g_rows, b_rows, norm_w, s0_all, layer, *, row0, nseq, T, C, hv, hqk, dk, dv,
               qk_dim, z_col0):
    N = T // C
    rep = hv // hqk
    hb = min(GDN_HEADS_PER_STEP, hv)
    hq = hb // rep
    has_s0 = s0_all is not None
    kb0 = qk_dim // (hq * dk)
    vb0 = 2 * qk_dim // (hb * dv)
    zb0 = z_col0 // (hb * dv)
    rz0 = row0 // C
    in_specs = [pl.BlockSpec((C, hq * dk), lambda b, h, n: (b * N + n, h)),
                pl.BlockSpec((C, hq * dk), lambda b, h, n: (b * N + n, kb0 + h)),
                pl.BlockSpec((C, hb * dv), lambda b, h, n: (b * N + n, vb0 + h)),
                pl.BlockSpec((C, hb * dv), lambda b, h, n: (rz0 + b * N + n, zb0 + h)),
                pl.BlockSpec((1, hb, N, C), lambda b, h, n: (b, h, 0, 0)),
                pl.BlockSpec((1, hb, N, C), lambda b, h, n: (b, h, 0, 0)),
                pl.BlockSpec((1, dv), lambda b, h, n: (0, 0))]
    args = [conv, conv, conv, proj, g_rows, b_rows, norm_w.reshape(1, dv)]
    if has_s0:
        in_specs.append(pl.BlockSpec((None, 1, hb, dk, dv), lambda b, h, n: (layer, b, h, 0, 0)))
        args.append(s0_all)
    return pl.pallas_call(
        functools.partial(_gdn_chunk_kernel, C=C, hq=hq, rep=rep, dk=dk, dv=dv, n_chunks=N,
                          has_s0=has_s0),
        grid=(nseq, hv // hb, N),
        in_specs=in_specs,
        out_specs=[pl.BlockSpec((C, hb * dv), lambda b, h, n: (b * N + n, h)),
                   pl.BlockSpec((1, hb, dk, dv), lambda b, h, n: (b, h, 0, 0))],
        out_shape=[jax.ShapeDtypeStruct((nseq * T, hv * dv), BF16),
                   jax.ShapeDtypeStruct((nseq, hv, dk, dv), F32)],
        scratch_shapes=[pltpu.VMEM((hb, dk, dv), F32)],
        compiler_params=_params("parallel", "parallel", "arbitrary"),
        name="gdn_chunks",
    )(*args)


def _cumsum_kernel(x_ref, o_ref, carry_ref, *, tb):
    @pl.when(pl.program_id(1) == 0)
    def _():
        carry_ref[...] = jnp.zeros_like(carry_ref)

    row = lax.broadcasted_iota(jnp.int32, (tb, tb), 0)
    col = lax.broadcasted_iota(jnp.int32, (tb, tb), 1)
    tril = jnp.where(row >= col, 1.0, 0.0).astype(F32)
    c = jnp.dot(tril, x_ref[...], precision=HIGHEST, preferred_element_type=F32) + carry_ref[...]
    o_ref[...] = c
    carry_ref[...] = c[tb - 1:tb, :]


def cumsum_time(x, nseq, L):
    H = x.shape[1]
    tb = _pick(L, 256, 8)
    nt = L // tb
    return pl.pallas_call(
        functools.partial(_cumsum_kernel, tb=tb),
        grid=(nseq, nt),
        in_specs=[pl.BlockSpec((tb, H), lambda s, t: (s * nt + t, 0))],
        out_specs=pl.BlockSpec((tb, H), lambda s, t: (s * nt + t, 0)),
        out_shape=jax.ShapeDtypeStruct((nseq * L, H), F32),
        scratch_shapes=[pltpu.VMEM((1, H), F32)],
        compiler_params=_params("parallel", "arbitrary"),
        name="fox_cumsum",
    )(x)


def _fox_prompt_kernel(q_ref, k_ref, v_ref, og_ref, cq_ref, ckt_ref, o_ref, acc_ref, m_ref, cqr_ref,
                       *, bq, H, dh, scale, hg):
    i = pl.program_id(1)
    j = pl.program_id(2)
    rep = bq // dh

    @pl.when(j == 0)
    def _():
        m_ref[...] = jnp.full(m_ref.shape, -jnp.inf, F32)
        acc_ref[...] = jnp.zeros_like(acc_ref)
        cqs = cq_ref[...] * LOG2E
        for h in range(H):
            cqr_ref[h] = jnp.broadcast_to(cqs[:, h:h + 1], (bq, dh))

    def block(masked):
        if masked:
            mask = lax.broadcasted_iota(jnp.int32, (bq, bq), 1) <= lax.broadcasted_iota(jnp.int32, (bq, bq), 0)
        ones16 = jnp.ones((bq, dh), BF16)
        ck2 = ckt_ref[...] * LOG2E

        def qk(h):
            hs = slice(h * dh, (h + 1) * dh)
            return lax.dot_general((q_ref[:, hs].astype(F32) * (scale * LOG2E)).astype(BF16), k_ref[:, hs],
                                   NT_DIMS, preferred_element_type=F32)

        groups = [range(g * hg, (g + 1) * hg) for g in range(H // hg)]
        s = {h: qk(h) for h in groups[0]}
        for gi, grp in enumerate(groups):
            if gi + 1 < len(groups):
                for h in groups[gi + 1]:
                    s[h] = qk(h)
            p16, alpha2 = {}, {}
            for h in grp:
                t = s.pop(h) - ck2[h:h + 1, :]
                if masked:
                    t = jnp.where(mask, t, -jnp.inf)
                cq2 = cqr_ref[h]
                m_prev = m_ref[h]
                m_new = jnp.maximum(m_prev, jnp.max(t, axis=1, keepdims=True) + cq2)
                m_ref[h] = m_new
                alpha = jnp.exp2(m_prev - m_new)
                c = cq2 - m_new
                p16[h] = jnp.exp2(t + jnp.concatenate([c] * rep, axis=1)).astype(BF16)
                alpha2[h] = jnp.concatenate([alpha, alpha], axis=1)
            for h in grp:
                hs = slice(h * dh, (h + 1) * dh)
                v_ext = jnp.concatenate([v_ref[:, hs], ones16], axis=1)
                acc_ref[h] = alpha2[h] * acc_ref[h] + jnp.dot(p16[h], v_ext, preferred_element_type=F32)

    @pl.when(j < i)
    def _():
        block(False)

    @pl.when(j == i)
    def _():
        block(True)
        for h in range(H):
            hs = slice(h * dh, (h + 1) * dh)
            a = acc_ref[h]
            o_ref[:, hs] = ((a[:, :dh] / a[:, dh:]) * _sigmoid(og_ref[:, hs].astype(F32))).astype(o_ref.dtype)


def fox_prompt_attention(proj, cum, cum_t, *, nseq, T, H, dh):
    bq = _pick(T, 256, 128)
    nq = T // bq
    D = H * dh
    hg = _pick(H, 4, 1)
    kv = lambda col: (lambda b, i, j: (b * nq + jnp.minimum(j, i), col))
    return pl.pallas_call(
        functools.partial(_fox_prompt_kernel, bq=bq, H=H, dh=dh, scale=dh ** -0.5, hg=hg),
        grid=(nseq, nq, nq),
        in_specs=[pl.BlockSpec((bq, D), lambda b, i, j: (b * nq + i, 0)),
                  pl.BlockSpec((bq, D), kv(1)),
                  pl.BlockSpec((bq, D), kv(2)),
                  pl.BlockSpec((bq, D), lambda b, i, j: (b * nq + i, 3)),
                  pl.BlockSpec((bq, H), lambda b, i, j: (b * nq + i, 0)),
                  pl.BlockSpec((H, bq), lambda b, i, j: (0, b * nq + jnp.minimum(j, i)))],
        out_specs=pl.BlockSpec((bq, D), lambda b, i, j: (b * nq + i, 0)),
        out_shape=jax.ShapeDtypeStruct((nseq * T, D), BF16),
        scratch_shapes=[pltpu.VMEM((H, bq, 2 * dh), F32), pltpu.VMEM((H, bq, dh), F32),
                        pltpu.VMEM((H, bq, dh), F32)],
        compiler_params=_params("parallel", "parallel", "arbitrary"),
        name="fox_prompt_attention",
    )(proj, proj, proj, proj, cum, cum_t)


def _fox_sample_kernel(q_ref, kn_ref, vn_ref, og_ref, kc_ref, vc_ref, cq_ref, ck_ref, o_ref,
                       *, P, T, G, dh, scale):
    R, PG = T * G, P * G
    gs = G.bit_length() - 1
    q = (q_ref[0, 0].astype(F32) * scale).astype(BF16)
    kc = kc_ref[0].reshape(PG, dh).astype(BF16)
    vc = vc_ref[0].reshape(PG, dh).astype(BF16)
    s_p = lax.dot_general(q, kc, NT_DIMS, preferred_element_type=F32)
    s_n = lax.dot_general(q, kn_ref[0, 0], NT_DIMS, preferred_element_type=F32)
    cq = cq_ref[0, 0]
    ck = ck_ref[0, 0]
    same_p = ((lax.broadcasted_iota(jnp.int32, (R, PG), 0) & (G - 1))
              == (lax.broadcasted_iota(jnp.int32, (R, PG), 1) & (G - 1)))
    rn = lax.broadcasted_iota(jnp.int32, (R, R), 0)
    cn = lax.broadcasted_iota(jnp.int32, (R, R), 1)
    same_n = jnp.logical_and((rn & (G - 1)) == (cn & (G - 1)), (cn >> gs) <= (rn >> gs))
    s_p = jnp.where(same_p, s_p + (cq - ck[:, :PG]), -jnp.inf)
    s_n = jnp.where(same_n, s_n + (cq - ck[:, PG:]), -jnp.inf)
    m = jnp.maximum(jnp.max(s_p, axis=1, keepdims=True), jnp.max(s_n, axis=1, keepdims=True))
    p_p = jnp.exp(s_p - m)
    p_n = jnp.exp(s_n - m)
    l = jnp.sum(p_p, axis=1, keepdims=True) + jnp.sum(p_n, axis=1, keepdims=True)
    o = (jnp.dot(p_p.astype(BF16), vc, preferred_element_type=F32)
         + jnp.dot(p_n.astype(BF16), vn_ref[0, 0], preferred_element_type=F32))
    o_ref[0, 0] = ((o / l) * _sigmoid(og_ref[0, 0].astype(F32))).astype(o_ref.dtype)


def fox_sample_attention(proj, k_cache_all, v_cache_all, layer, cum, *, row0, nseq, T, P, H, dh):
    G = SUBLANES if H % SUBLANES == 0 else H
    assert G & (G - 1) == 0
    NG = H // G
    R = T * G
    new = proj[row0:row0 + nseq * T].reshape(nseq, T, 4, NG, G, dh).transpose(2, 0, 3, 1, 4, 5)
    new = new.reshape(4, nseq, NG, R, dh)
    ck = cum.reshape(nseq, P + T, NG, G).transpose(0, 2, 1, 3).reshape(nseq, NG, 1, (P + T) * G)
    cq = cum[:, P:].reshape(nseq, T, NG, G).transpose(0, 2, 1, 3).reshape(nseq, NG, R, 1)
    part = lambda a: pl.BlockSpec((None, 1, 1, R, dh), lambda b, g: (a, b, g, 0, 0))
    cache = pl.BlockSpec((None, 1, P, G, dh), lambda b, g: (layer, b, 0, g, 0))
    out = pl.pallas_call(
        functools.partial(_fox_sample_kernel, P=P, T=T, G=G, dh=dh, scale=dh ** -0.5),
        grid=(nseq, NG),
        in_specs=[part(0), part(1), part(2), part(3), cache, cache,
                  pl.BlockSpec((1, 1, R, 1), lambda b, g: (b, g, 0, 0)),
                  pl.BlockSpec((1, 1, 1, (P + T) * G), lambda b, g: (b, g, 0, 0))],
        out_specs=pl.BlockSpec((1, 1, R, dh), lambda b, g: (b, g, 0, 0)),
        out_shape=jax.ShapeDtypeStruct((nseq, NG, R, dh), BF16),
        compiler_params=_params("parallel", "parallel"),
        name="fox_sample_attention",
    )(new, new, new, new, k_cache_all, v_cache_all, cq, ck)
    return out.reshape(nseq, NG, T, G, dh).transpose(0, 2, 1, 3, 4).reshape(nseq * T, H * dh)


def kernel(x_prompt, x_sample, state_gdn, state_gdn_conv, cache_fox_k, cache_fox_v, cache_fox_logf,
           norm_mix, norm_mlp, norm_final, gdn_w_in, gdn_conv_w, gdn_a_log, gdn_dt_bias, gdn_norm_w,
           gdn_w_out, fox_w_in, fox_b_f, fox_w_out, mlp_w_up, mlp_w_down):
    Bp, Tp, D = x_prompt.shape
    Bs, Ts, _ = x_sample.shape
    depth = norm_mix.shape[0]
    _, _, hv, dk, dv = state_gdn.shape
    conv_dim = state_gdn_conv.shape[-1]
    cw1 = state_gdn_conv.shape[2]
    v_dim = hv * dv
    qk_dim = (conv_dim - v_dim) // 2
    hqk = qk_dim // dk
    _, _, P, H, dh = cache_fox_k.shape
    fox_dim = H * dh
    d_ff = mlp_w_up.shape[-1]
    Mp, Ms = Bp * Tp, Bs * Ts
    Cp = min(GDN_CHUNK, Tp)

    x = jnp.concatenate([x_prompt.reshape(Mp, D), x_sample.reshape(Ms, D)], axis=0)
    S_p, S_s, buf_p, buf_s = [], [], [], []
    k_p, k_s, v_p, v_s, lf_p, lf_s = [], [], [], [], [], []

    for i in range(depth):
        j = i // 2
        if i % 2 == 0:
            n_main = conv_dim + v_dim
            proj = norm_proj(x, norm_mix[i], gdn_w_in, j, n_main, name="gdn_in_proj")
            beta, g = gdn_gates(x, norm_mix[i], gdn_w_in, j, n_main, gdn_a_log[j], gdn_dt_bias[j])
            conv_p = gdn_conv(proj, gdn_conv_w, None, j, row0=0, nseq=Bp, T=Tp,
                              conv_dim=conv_dim, qk_dim=qk_dim, dk=dk)
            conv_s = gdn_conv(proj, gdn_conv_w, state_gdn_conv, j, row0=Mp, nseq=Bs, T=Ts,
                              conv_dim=conv_dim, qk_dim=qk_dim, dk=dk)

            def rows(a, B, T, C):
                return a.reshape(B, T // C, C, hv).transpose(0, 3, 1, 2)

            common = dict(hv=hv, hqk=hqk, dk=dk, dv=dv, qk_dim=qk_dim, z_col0=conv_dim)
            o_p, Sp = gdn_chunks(conv_p, proj, rows(g[:Mp], Bp, Tp, Cp), rows(beta[:Mp], Bp, Tp, Cp),
                                 gdn_norm_w[j], None, j, row0=0, nseq=Bp, T=Tp, C=Cp, **common)
            o_s, Ss = gdn_chunks(conv_s, proj, rows(g[Mp:], Bs, Ts, Ts), rows(beta[Mp:], Bs, Ts, Ts),
                                 gdn_norm_w[j], state_gdn, j, row0=Mp, nseq=Bs, T=Ts, C=Ts, **common)
            x = matmul_res(jnp.concatenate([o_p, o_s], axis=0), gdn_w_out, j, x, name="gdn_out_proj")
            S_p.append(Sp)
            S_s.append(Ss)
            buf_p.append(jnp.stack([proj[(b + 1) * Tp - cw1:(b + 1) * Tp, :conv_dim]
                                    for b in range(Bp)]).astype(F32))
            qkv_s = proj[Mp:, :conv_dim].reshape(Bs, Ts, conv_dim).astype(F32)
            buf_s.append(jnp.concatenate([state_gdn_conv[j], qkv_s], axis=1)[:, -cw1:])
        else:
            proj = norm_proj(x, norm_mix[i], fox_w_in, j, 4 * fox_dim, name="fox_in_proj")
            logf = fox_logf(x, norm_mix[i], fox_w_in, j, 4 * fox_dim, fox_b_f[j])
            cum_p = cumsum_time(logf[:Mp], Bp, Tp)
            o_p = fox_prompt_attention(proj, cum_p, cum_p.T, nseq=Bp, T=Tp, H=H, dh=dh)
            lf_new = logf[Mp:].reshape(Bs, Ts, H)
            lf_all = jnp.concatenate([cache_fox_logf[j], lf_new], axis=1)
            cum_s = cumsum_time(lf_all.reshape(Bs * (P + Ts), H), Bs, P + Ts).reshape(Bs, P + Ts, H)
            o_s = fox_sample_attention(proj, cache_fox_k, cache_fox_v, j, cum_s,
                                       row0=Mp, nseq=Bs, T=Ts, P=P, H=H, dh=dh)
            x = matmul_res(jnp.concatenate([o_p, o_s], axis=0), fox_w_out, j, x, name="fox_out_proj")
            kk = proj[:, fox_dim:2 * fox_dim].astype(F32)
            vv = proj[:, 2 * fox_dim:3 * fox_dim].astype(F32)
            k_p.append(kk[:Mp].reshape(Bp, Tp, H, dh))
            k_s.append(kk[Mp:].reshape(Bs, Ts, H, dh))
            v_p.append(vv[:Mp].reshape(Bp, Tp, H, dh))
            v_s.append(vv[Mp:].reshape(Bs, Ts, H, dh))
            lf_p.append(logf[:Mp].reshape(Bp, Tp, H))
            lf_s.append(lf_new)
        hmid = norm_proj(x, norm_mlp[i], mlp_w_up, i, d_ff, relu2=True, name="mlp_up")
        x = matmul_res(hmid, mlp_w_down, i, x, name="mlp_down")

    y_p = rmsnorm_rows(x, norm_final, 0, Mp)
    y_s = rmsnorm_rows(x, norm_final, Mp, Ms)
    st = jnp.stack
    return (y_p.reshape(Bp, Tp, D), y_s.reshape(Bs, Ts, D),
            st(S_p), st(buf_p), st(k_p), st(v_p), st(lf_p),
            st(S_s), st(buf_s), st(k_s), st(v_s), st(lf_s))
```

```python
import functools

import jax
import jax.numpy as jnp
from jax import lax
from jax.experimental import pallas as pl
from jax.experimental.pallas import tpu as pltpu

F32 = jnp.float32
BF16 = jnp.bfloat16
EPS = 1e-6
GDN_CHUNK = 64
GDN_HEADS_PER_STEP = 32
INV_BASE_BLOCK = 8
VMEM_LIMIT_BYTES = 58 * 1024 * 1024
LOG2E = 1.4426950408889634
SUBLANES = 8
HIGHEST = lax.Precision.HIGHEST
NT_DIMS = (((1,), (1,)), ((), ()))
TN_DIMS = (((0,), (0,)), ((), ()))


def _pick(n, pref, align):
    best = None
    for d in range(align, min(n, pref) + 1, align):
        if n % d == 0:
            best = d
    return n if best is None else best


def _params(*sem):
    return pltpu.CompilerParams(dimension_semantics=sem, vmem_limit_bytes=VMEM_LIMIT_BYTES)


def _sigmoid(x):
    return 1.0 / (1.0 + jnp.exp(-x))


def _softplus(x):
    return jnp.maximum(x, 0.0) + jnp.log1p(jnp.exp(-jnp.abs(x)))


def _rms_rows(x, w):
    ms = jnp.mean(x * x, axis=-1, keepdims=True)
    return x * lax.rsqrt(ms + EPS) * w


def _bdot(a, b):
    return jnp.dot(a.astype(BF16), b.astype(BF16), preferred_element_type=F32)


def _wdot(xn, w_ref, w_t):
    w = w_ref[...].astype(BF16)
    if w_t:
        return lax.dot_general(xn, w, NT_DIMS, preferred_element_type=F32)
    return jnp.dot(xn, w, preferred_element_type=F32)


def _norm_proj_kernel(x_ref, nw_ref, w_ref, o_ref, xn_ref, *, relu2, w_t):
    @pl.when(pl.program_id(1) == 0)
    def _():
        tm = x_ref.shape[0]
        rows = _pick(tm, 272, 16)

        def body(r, carry):
            sl = pl.ds(pl.multiple_of(r * rows, rows), rows)
            xn_ref[sl, :] = _rms_rows(x_ref[sl, :], nw_ref[...]).astype(BF16)
            return carry

        lax.fori_loop(0, tm // rows, body, 0)

    acc = _wdot(xn_ref[...], w_ref, w_t)
    if relu2:
        acc = jnp.square(jnp.maximum(acc, 0.0))
    o_ref[...] = acc.astype(o_ref.dtype)


def norm_proj(x, nw, w_all, layer, n_cols, *, relu2=False, w_t=False, name="norm_proj"):
    M, K = x.shape
    tm = _pick(M, 1088, 16)
    tn = _pick(n_cols, 1024, 128)
    if w_t:
        w_spec = pl.BlockSpec((None, tn, K), lambda i, j: (layer, j, 0))
    else:
        w_spec = pl.BlockSpec((None, K, tn), lambda i, j: (layer, 0, j))
    return pl.pallas_call(
        functools.partial(_norm_proj_kernel, relu2=relu2, w_t=w_t),
        grid=(M // tm, n_cols // tn),
        in_specs=[pl.BlockSpec((tm, K), lambda i, j: (i, 0)),
                  pl.BlockSpec((1, K), lambda i, j: (0, 0)),
                  w_spec],
        out_specs=pl.BlockSpec((tm, tn), lambda i, j: (i, j)),
        out_shape=jax.ShapeDtypeStruct((M, n_cols), BF16),
        scratch_shapes=[pltpu.VMEM((tm, K), BF16)],
        compiler_params=_params("parallel", "arbitrary"),
        name=name,
    )(x, nw.reshape(1, K), w_all)


LANES = 128


def _gdn_gate_kernel(x_ref, nw_ref, w_ref, alog_ref, dtb_ref, beta_ref, g_ref, *, hv):
    xn = _rms_rows(x_ref[...], nw_ref[...]).astype(BF16)
    acc = _wdot(xn, w_ref, True)
    beta_ref[...] = _sigmoid(acc[:, :hv])
    g_ref[...] = -jnp.exp(alog_ref[...]) * _softplus(acc[:, hv:2 * hv] + dtb_ref[...])


def gdn_gates(x, nw, wt_all, layer, col0, a_log, dt_bias):
    M, K = x.shape
    H = a_log.shape[0]
    assert col0 % LANES == 0 and 2 * H <= LANES
    tm = _pick(M, 544, 8)
    row = lambda i: (i, 0)
    fix = lambda i: (0, 0)
    return pl.pallas_call(
        functools.partial(_gdn_gate_kernel, hv=H),
        grid=(M // tm,),
        in_specs=[pl.BlockSpec((tm, K), row), pl.BlockSpec((1, K), fix),
                  pl.BlockSpec((None, LANES, K), lambda i: (layer, col0 // LANES, 0)),
                  pl.BlockSpec((1, H), fix), pl.BlockSpec((1, H), fix)],
        out_specs=[pl.BlockSpec((tm, H), row), pl.BlockSpec((tm, H), row)],
        out_shape=[jax.ShapeDtypeStruct((M, H), F32)] * 2,
        compiler_params=_params("parallel"),
        name="gdn_gates",
    )(x, nw.reshape(1, K), wt_all, a_log.reshape(1, H), dt_bias.reshape(1, H))


def _fox_logf_kernel(x_ref, nw_ref, w_ref, bf_ref, lf_ref, *, H):
    xn = _rms_rows(x_ref[...], nw_ref[...]).astype(BF16)
    acc = _wdot(xn, w_ref, True)
    f = acc[:, :H] + bf_ref[...]
    lf_ref[...] = -_softplus(-f)


def fox_logf(x, nw, wt_all, layer, col0, b_f):
    M, K = x.shape
    H = b_f.shape[0]
    assert col0 % LANES == 0 and H <= LANES
    tm = _pick(M, 544, 8)
    row = lambda i: (i, 0)
    fix = lambda i: (0, 0)
    return pl.pallas_call(
        functools.partial(_fox_logf_kernel, H=H),
        grid=(M // tm,),
        in_specs=[pl.BlockSpec((tm, K), row), pl.BlockSpec((1, K), fix),
                  pl.BlockSpec((None, LANES, K), lambda i: (layer, col0 // LANES, 0)),
                  pl.BlockSpec((1, H), fix)],
        out_specs=pl.BlockSpec((tm, H), row),
        out_shape=jax.ShapeDtypeStruct((M, H), F32),
        compiler_params=_params("parallel"),
        name="fox_logf",
    )(x, nw.reshape(1, K), wt_all, b_f.reshape(1, H))


def _matmul_res_kernel(a_ref, w_ref, r_ref, o_ref, acc_ref, *, nk):
    part = jnp.dot(a_ref[...], w_ref[...].astype(BF16), preferred_element_type=F32)
    if nk == 1:
        o_ref[...] = r_ref[...] + part
        return
    k = pl.program_id(2)

    @pl.when(k == 0)
    def _():
        acc_ref[...] = part

    @pl.when(k > 0)
    def _():
        acc_ref[...] += part

    @pl.when(k == nk - 1)
    def _():
        o_ref[...] = r_ref[...] + acc_ref[...]


def matmul_res(a, w_all, layer, res, *, name="matmul_res"):
    M, K = a.shape
    N = w_all.shape[-1]
    tm = _pick(M, 1088, 16)
    tn = _pick(N, 512, 128)
    tk = _pick(K, 4096, 128)
    nk = K // tk
    return pl.pallas_call(
        functools.partial(_matmul_res_kernel, nk=nk),
        grid=(M // tm, N // tn, nk),
        in_specs=[pl.BlockSpec((tm, tk), lambda i, j, k: (i, k)),
                  pl.BlockSpec((None, tk, tn), lambda i, j, k: (layer, k, j)),
                  pl.BlockSpec((tm, tn), lambda i, j, k: (i, j))],
        out_specs=pl.BlockSpec((tm, tn), lambda i, j, k: (i, j)),
        out_shape=jax.ShapeDtypeStruct((M, N), F32),
        scratch_shapes=[pltpu.VMEM((tm, tn), F32)],
        compiler_params=_params("parallel", "parallel", "arbitrary"),
        name=name,
    )(a, w_all, res)


def _rmsnorm_kernel(x_ref, nw_ref, o_ref):
    o_ref[...] = _rms_rows(x_ref[...], nw_ref[...])


def rmsnorm_rows(x, nw, row0, n_rows):
    K = x.shape[1]
    tm = _pick(n_rows, 512, 8)
    rb0 = row0 // tm
    return pl.pallas_call(
        _rmsnorm_kernel,
        grid=(n_rows // tm,),
        in_specs=[pl.BlockSpec((tm, K), lambda i: (rb0 + i, 0)), pl.BlockSpec((1, K), lambda i: (0, 0))],
        out_specs=pl.BlockSpec((tm, K), lambda i: (i, 0)),
        out_shape=jax.ShapeDtypeStruct((n_rows, K), F32),
        compiler_params=_params("parallel"),
        name="final_rmsnorm",
    )(x, nw.reshape(1, K))


CONV_HALO = 8
CONV_SINGLE_BLOCK_ELEMS = 256 * 1024


def _gdn_conv_kernel(x_ref, w_ref, *rest, tb, cb, dk, v_dim, n_q_blocks, n_qk_blocks, has_buf, q_scale):
    if has_buf:
        buf_ref, o_ref, xc_ref = rest
    else:
        o_ref, xc_ref = rest
    cw = w_ref.shape[0]
    lo = CONV_HALO - (cw - 1)

    @pl.when(pl.program_id(2) == 0)
    def _():
        if has_buf:
            xc_ref[lo:CONV_HALO, :] = buf_ref[0]
        else:
            xc_ref[lo:CONV_HALO, :] = jnp.zeros((cw - 1, cb), F32)

    xc_ref[CONV_HALO:CONV_HALO + tb, :] = x_ref[...].astype(F32)
    acc = xc_ref[lo:lo + tb, :] * w_ref[0:1, :]
    for j in range(1, cw):
        acc = acc + xc_ref[lo + j:lo + j + tb, :] * w_ref[j:j + 1, :]
    y = acc * _sigmoid(acc)
    xc_ref[lo:CONV_HALO, :] = xc_ref[lo + tb:CONV_HALO + tb, :]

    def l2(heads, scale):
        for hh in heads:
            yh = y[:, hh * dk:(hh + 1) * dk]
            ss = jnp.sum(yh * yh, axis=-1, keepdims=True)
            o_ref[:, hh * dk:(hh + 1) * dk] = yh * (lax.rsqrt(ss + EPS) * scale)

    if n_qk_blocks == 0:
        nq = (cb - v_dim) // (2 * dk)
        l2(range(nq), q_scale)
        l2(range(nq, 2 * nq), 1.0)
        o_ref[:, 2 * nq * dk:] = y[:, 2 * nq * dk:]
        return

    c = pl.program_id(1)

    @pl.when(c < n_q_blocks)
    def _():
        l2(range(cb // dk), q_scale)

    @pl.when(jnp.logical_and(c >= n_q_blocks, c < n_qk_blocks))
    def _():
        l2(range(cb // dk), 1.0)

    @pl.when(c >= n_qk_blocks)
    def _():
        o_ref[...] = y


def gdn_conv(proj, conv_w_all, buf_all, layer, *, row0, nseq, T, conv_dim, qk_dim, dk):
    cw = conv_w_all.shape[1]
    tb = _pick(T, 512, 8)
    cb = conv_dim if T * conv_dim <= CONV_SINGLE_BLOCK_ELEMS else _pick(qk_dim, 1024, dk)
    nt = T // tb
    rb0 = row0 // tb
    has_buf = buf_all is not None
    in_specs = [pl.BlockSpec((tb, cb), lambda s, c, t: (rb0 + s * nt + t, c)),
                pl.BlockSpec((None, cw, cb), lambda s, c, t: (layer, 0, c))]
    args = [proj, conv_w_all]
    if has_buf:
        in_specs.append(pl.BlockSpec((None, 1, cw - 1, cb), lambda s, c, t: (layer, s, 0, c)))
        args.append(buf_all)
    return pl.pallas_call(
        functools.partial(_gdn_conv_kernel, tb=tb, cb=cb, dk=dk, v_dim=conv_dim - 2 * qk_dim,
                          n_q_blocks=qk_dim // cb,
                          n_qk_blocks=2 * qk_dim // cb, has_buf=has_buf, q_scale=dk ** -0.5),
        grid=(nseq, conv_dim // cb, nt),
        in_specs=in_specs,
        out_specs=pl.BlockSpec((tb, cb), lambda s, c, t: (s * nt + t, c)),
        out_shape=jax.ShapeDtypeStruct((nseq * T, conv_dim), F32),
        scratch_shapes=[pltpu.VMEM((CONV_HALO + tb, cb), F32)],
        compiler_params=_params("parallel", "parallel", "arbitrary"),
        name="gdn_conv",
    )(*args)


def _gdn_chunk_kernel(q_ref, k_ref, v_ref, z_ref, g_ref, b_ref, nw_ref, *rest,
                      C, hq, rep, dk, dv, n_chunks, has_s0, has_prev):
    rest = list(rest)
    s0_ref = rest.pop(0) if has_s0 else None
    if has_prev:
        rest.pop(0)
    o_ref, sout_ref, s_ref = rest
    n = pl.program_id(2)
    C2 = 2 * C

    @pl.when(n == 0)
    def _():
        if has_s0:
            s_ref[...] = s0_ref[0]
        else:
            s_ref[...] = jnp.zeros_like(s_ref)

    row = lax.broadcasted_iota(jnp.int32, (C, C2), 0)
    lane = lax.broadcasted_iota(jnp.int32, (C, C2), 1)
    right = lane >= C
    col = jnp.where(right, lane - C, lane)
    incl = row >= col
    strict = row > col
    diag = lane == row
    eye_right = jnp.where(lane == row + C, 1.0, 0.0).astype(F32)
    tr = lax.broadcasted_iota(jnp.int32, (C, C2), 0)
    triu2 = jnp.where(tr <= col, 1.0, 0.0).astype(F32)
    zeros_z = jnp.zeros((C, C2), BF16)
    zeros_r = jnp.zeros((C, dk + dv), BF16)

    def to_col(r2):
        return jnp.sum(jnp.where(diag, r2, 0.0), axis=1, keepdims=True)

    diag_c = (lax.broadcasted_iota(jnp.int32, (C, C), 0) == lax.broadcasted_iota(jnp.int32, (C, C), 1))

    def to_col_c(r):
        return jnp.sum(jnp.where(diag_c, r, 0.0), axis=1, keepdims=True)

    heads = range(hq * rep)
    q = [q_ref[:, a * dk:(a + 1) * dk] for a in range(hq)]
    k = [k_ref[:, a * dk:(a + 1) * dk] for a in range(hq)]
    q16 = [x.astype(BF16) for x in q]
    kk16 = [jnp.concatenate([x.astype(BF16)] * 2, axis=0) for x in k]
    G2, G_col, b_col, dec_incl, gamma, G_last, kb, vb = [], [], [], [], [], [], [], []
    for i in heads:
        g_row = g_ref[0, i, pl.ds(n, 1), :]
        b_row = b_ref[0, i, pl.ds(n, 1), :]
        G2.append(jnp.dot(g_row, triu2, precision=HIGHEST, preferred_element_type=F32))
        G_col.append(to_col(G2[i]))
        b_col.append(to_col_c(b_row))
        dec_incl.append(jnp.exp(jnp.where(incl, G_col[i] - G2[i], -jnp.inf)))
        gamma.append(jnp.exp(G_col[i]))
        G_last.append(G2[i][:, C - 1:C])
        kb.append(k[i // rep] * b_col[i])
        vb.append(v_ref[:, i * dv:(i + 1) * dv] * b_col[i])
    gram = [lax.dot_general(jnp.concatenate([kb[i].astype(BF16), q16[i // rep]], axis=0), kk16[i // rep],
                            NT_DIMS, preferred_element_type=F32) for i in heads]
    Lf = [gram[i][:C] * dec_incl[i] for i in heads]
    base = min(INV_BASE_BLOCK, C)
    sh = base.bit_length() - 1
    in_base = jnp.logical_and(strict, (row >> sh) == (col >> sh))
    Z = [jnp.where(right, eye_right, -jnp.where(in_base, Lf[i], 0.0)) for i in heads]
    for _ in range((base - 1).bit_length()):
        Z16 = [z.astype(BF16) for z in Z]
        Z = [jnp.dot(Z16[i], jnp.concatenate([Z16[i], zeros_z], axis=0), preferred_element_type=F32)
             + jnp.where(right, Z[i], 0.0) for i in heads]
    b = base
    while b < C:
        sh = b.bit_length() - 1
        lower_left = jnp.logical_and(jnp.logical_and(right, (row >> (sh + 1)) == (col >> (sh + 1))),
                                     jnp.logical_and(((row >> sh) & 1) == 1, ((col >> sh) & 1) == 0))
        Z16 = [z.astype(BF16) for z in Z]
        W16 = [jnp.dot(jnp.where(lower_left, Lf[i], 0.0).astype(BF16),
                       jnp.concatenate([zeros_z, Z16[i]], axis=0),
                       preferred_element_type=F32).astype(BF16) for i in heads]
        Z = [Z[i] - jnp.dot(Z16[i], jnp.concatenate([zeros_z, W16[i]], axis=0), preferred_element_type=F32)
             for i in heads]
        b *= 2
    wu = []
    for i in heads:
        rhs = jnp.concatenate([(kb[i] * gamma[i]).astype(BF16), vb[i].astype(BF16)], axis=1)
        wu.append(jnp.dot(Z[i].astype(BF16), jnp.concatenate([zeros_r, rhs], axis=0),
                          preferred_element_type=F32))
    S = [s_ref[i] for i in heads]
    S16 = [x.astype(BF16) for x in S]
    ws_qs = [jnp.dot(jnp.concatenate([wu[i][:, :dk].astype(BF16), (q[i // rep] * gamma[i]).astype(BF16)], axis=0),
                     S16[i], preferred_element_type=F32) for i in heads]
    U16 = [(wu[i][:, dk:] - ws_qs[i][:C]).astype(BF16) for i in heads]
    A16 = [(gram[i][C:] * dec_incl[i])[:, :C].astype(BF16) for i in heads]
    o = [ws_qs[i][C:] + jnp.dot(A16[i], U16[i], preferred_element_type=F32) for i in heads]
    for i in heads:
        kd = k[i // rep] * jnp.exp(G_last[i] - G_col[i])
        s_ref[i] = jnp.exp(G_last[i]) * S[i] + lax.dot_general(kd.astype(BF16), U16[i], TN_DIMS,
                                                               preferred_element_type=F32)
    for i in heads:
        z = z_ref[:, i * dv:(i + 1) * dv].astype(F32)
        o_ref[:, i * dv:(i + 1) * dv] = (_rms_rows(o[i], nw_ref[...]) * (z * _sigmoid(z))).astype(o_ref.dtype)

    @pl.when(n == n_chunks - 1)
    def _():
        sout_ref[0] = s_ref[...]


def gdn_chunks(conv, proj, g_rows, b_rows, norm_w, s0_all, layer, n_layers, s_prev, *, row0, nseq, T, C,
               hv, hqk, dk, dv, qk_dim, z_col0, out_rows):
    N = T // C
    rep = hv // hqk
    hb = min(GDN_HEADS_PER_STEP, hv)
    hq = hb // rep
    has_s0 = s0_all is not None
    kb0 = qk_dim // (hq * dk)
    vb0 = 2 * qk_dim // (hb * dv)
    zb0 = z_col0 // (hb * dv)
    rz0 = row0 // C
    in_specs = [pl.BlockSpec((C, hq * dk), lambda b, h, n: (b * N + n, h)),
                pl.BlockSpec((C, hq * dk), lambda b, h, n: (b * N + n, kb0 + h)),
                pl.BlockSpec((C, hb * dv), lambda b, h, n: (b * N + n, vb0 + h)),
                pl.BlockSpec((C, hb * dv), lambda b, h, n: (rz0 + b * N + n, zb0 + h)),
                pl.BlockSpec((1, hb, N, C), lambda b, h, n: (b, h, 0, 0)),
                pl.BlockSpec((1, hb, N, C), lambda b, h, n: (b, h, 0, 0)),
                pl.BlockSpec((1, dv), lambda b, h, n: (0, 0))]
    args = [conv, conv, conv, proj, g_rows, b_rows, norm_w.reshape(1, dv)]
    if has_s0:
        in_specs.append(pl.BlockSpec((None, 1, hb, dk, dv), lambda b, h, n: (layer, b, h, 0, 0)))
        args.append(s0_all)
    aliases = {}
    if s_prev is not None:
        aliases[len(args)] = 1
        in_specs.append(pl.BlockSpec(memory_space=pl.ANY))
        args.append(s_prev)
    return pl.pallas_call(
        functools.partial(_gdn_chunk_kernel, C=C, hq=hq, rep=rep, dk=dk, dv=dv, n_chunks=N,
                          has_s0=has_s0, has_prev=s_prev is not None),
        grid=(nseq, hv // hb, N),
        in_specs=in_specs,
        out_specs=[pl.BlockSpec((C, hb * dv), lambda b, h, n: (b * N + n, h)),
                   pl.BlockSpec((None, 1, hb, dk, dv), lambda b, h, n: (layer, b, h, 0, 0))],
        out_shape=[jax.ShapeDtypeStruct((out_rows, hv * dv), BF16),
                   jax.ShapeDtypeStruct((n_layers, nseq, hv, dk, dv), F32)],
        input_output_aliases=aliases,
        scratch_shapes=[pltpu.VMEM((hb, dk, dv), F32)],
        compiler_params=_params("parallel", "parallel", "arbitrary"),
        name="gdn_chunks",
    )(*args)


def _cumsum_kernel(x_ref, o_ref, carry_ref, *, tb):
    @pl.when(pl.program_id(0) == 0)
    def _():
        carry_ref[...] = jnp.zeros_like(carry_ref)

    row = lax.broadcasted_iota(jnp.int32, (tb, tb), 0)
    col = lax.broadcasted_iota(jnp.int32, (tb, tb), 1)
    tril = jnp.where(row >= col, 1.0, 0.0).astype(F32)
    c = jnp.dot(tril, x_ref[...], precision=HIGHEST, preferred_element_type=F32) + carry_ref[...]
    o_ref[...] = c
    carry_ref[...] = c[tb - 1:tb, :]


def cumsum_time(x):
    nseq, L, H = x.shape
    S = nseq * H
    tb = _pick(L, 256, 8)
    out = pl.pallas_call(
        functools.partial(_cumsum_kernel, tb=tb),
        grid=(L // tb,),
        in_specs=[pl.BlockSpec((tb, S), lambda t: (t, 0))],
        out_specs=pl.BlockSpec((tb, S), lambda t: (t, 0)),
        out_shape=jax.ShapeDtypeStruct((L, S), F32),
        scratch_shapes=[pltpu.VMEM((1, S), F32)],
        compiler_params=_params("arbitrary"),
        name="fox_cumsum",
    )(x.transpose(1, 0, 2).reshape(L, S))
    return out.reshape(L, nseq, H)


def _fox_prompt_kernel(q_ref, k_ref, v_ref, og_ref, cq_ref, ckt_ref, o_ref, acc_ref, m_ref, cqr_ref,
                       *, bq, H, dh, scale, hg):
    i = pl.program_id(1)
    j = pl.program_id(2)
    rep = bq // dh

    @pl.when(j == 0)
    def _():
        m_ref[...] = jnp.full(m_ref.shape, -jnp.inf, F32)
        acc_ref[...] = jnp.zeros_like(acc_ref)
        cqs = cq_ref[...] * LOG2E
        for h in range(H):
            cqr_ref[h] = jnp.broadcast_to(cqs[:, h:h + 1], (bq, dh))

    def block(masked):
        if masked:
            mask = lax.broadcasted_iota(jnp.int32, (bq, bq), 1) <= lax.broadcasted_iota(jnp.int32, (bq, bq), 0)
        ones16 = jnp.ones((bq, dh), BF16)
        ck2 = ckt_ref[...] * LOG2E

        def qk(h):
            hs = slice(h * dh, (h + 1) * dh)
            return lax.dot_general((q_ref[:, hs].astype(F32) * (scale * LOG2E)).astype(BF16), k_ref[:, hs],
                                   NT_DIMS, preferred_element_type=F32)

        groups = [range(g * hg, (g + 1) * hg) for g in range(H // hg)]
        s = {h: qk(h) for h in groups[0]}
        for gi, grp in enumerate(groups):
            if gi + 1 < len(groups):
                for h in groups[gi + 1]:
                    s[h] = qk(h)
            p16, alpha2 = {}, {}
            for h in grp:
                t = s.pop(h) - ck2[h:h + 1, :]
                if masked:
                    t = jnp.where(mask, t, -jnp.inf)
                cq2 = cqr_ref[h]
                m_prev = m_ref[h]
                m_new = jnp.maximum(m_prev, jnp.max(t, axis=1, keepdims=True) + cq2)
                m_ref[h] = m_new
                alpha = jnp.exp2(m_prev - m_new)
                c = cq2 - m_new
                p16[h] = jnp.exp2(t + jnp.concatenate([c] * rep, axis=1)).astype(BF16)
                alpha2[h] = jnp.concatenate([alpha, alpha], axis=1)
            for h in grp:
                hs = slice(h * dh, (h + 1) * dh)
                v_ext = jnp.concatenate([v_ref[:, hs], ones16], axis=1)
                acc_ref[h] = alpha2[h] * acc_ref[h] + jnp.dot(p16[h], v_ext, preferred_element_type=F32)

    @pl.when(j < i)
    def _():
        block(False)

    @pl.when(j == i)
    def _():
        block(True)
        for h in range(H):
            hs = slice(h * dh, (h + 1) * dh)
            a = acc_ref[h]
            o_ref[:, hs] = ((a[:, :dh] / a[:, dh:]) * _sigmoid(og_ref[:, hs].astype(F32))).astype(o_ref.dtype)


def fox_prompt_attention(proj, cum, cum_t, *, nseq, T, H, dh, out_rows):
    bq = _pick(T, 256, 128)
    nq = T // bq
    D = H * dh
    hg = _pick(H, 4, 1)
    kv = lambda col: (lambda b, i, j: (b * nq + jnp.minimum(j, i), col))
    return pl.pallas_call(
        functools.partial(_fox_prompt_kernel, bq=bq, H=H, dh=dh, scale=dh ** -0.5, hg=hg),
        grid=(nseq, nq, nq),
        in_specs=[pl.BlockSpec((bq, D), lambda b, i, j: (b * nq + i, 0)),
                  pl.BlockSpec((bq, D), kv(1)),
                  pl.BlockSpec((bq, D), kv(2)),
                  pl.BlockSpec((bq, D), lambda b, i, j: (b * nq + i, 3)),
                  pl.BlockSpec((bq, H), lambda b, i, j: (b * nq + i, 0)),
                  pl.BlockSpec((H, bq), lambda b, i, j: (0, b * nq + jnp.minimum(j, i)))],
        out_specs=pl.BlockSpec((bq, D), lambda b, i, j: (b * nq + i, 0)),
        out_shape=jax.ShapeDtypeStruct((out_rows, D), BF16),
        scratch_shapes=[pltpu.VMEM((H, bq, 2 * dh), F32), pltpu.VMEM((H, bq, dh), F32),
                        pltpu.VMEM((H, bq, dh), F32)],
        compiler_params=_params("parallel", "parallel", "arbitrary"),
        name="fox_prompt_attention",
    )(proj, proj, proj, proj, cum, cum_t)


def _fox_sample_kernel(q_ref, kn_ref, vn_ref, og_ref, kc_ref, vc_ref, cq_ref, ck_ref, o_ref,
                       *, P, T, G, dh, scale):
    R, PG = T * G, P * G
    gs = G.bit_length() - 1
    q = (q_ref[0, 0].astype(F32) * scale).astype(BF16)
    kc = kc_ref[0].reshape(PG, dh).astype(BF16)
    vc = vc_ref[0].reshape(PG, dh).astype(BF16)
    s_p = lax.dot_general(q, kc, NT_DIMS, preferred_element_type=F32)
    s_n = lax.dot_general(q, kn_ref[0, 0], NT_DIMS, preferred_element_type=F32)
    cq = cq_ref[0, 0]
    ck = ck_ref[0, 0]
    same_p = ((lax.broadcasted_iota(jnp.int32, (R, PG), 0) & (G - 1))
              == (lax.broadcasted_iota(jnp.int32, (R, PG), 1) & (G - 1)))
    rn = lax.broadcasted_iota(jnp.int32, (R, R), 0)
    cn = lax.broadcasted_iota(jnp.int32, (R, R), 1)
    same_n = jnp.logical_and((rn & (G - 1)) == (cn & (G - 1)), (cn >> gs) <= (rn >> gs))
    s_p = jnp.where(same_p, s_p + (cq - ck[:, :PG]), -jnp.inf)
    s_n = jnp.where(same_n, s_n + (cq - ck[:, PG:]), -jnp.inf)
    m = jnp.maximum(jnp.max(s_p, axis=1, keepdims=True), jnp.max(s_n, axis=1, keepdims=True))
    p_p = jnp.exp(s_p - m)
    p_n = jnp.exp(s_n - m)
    l = jnp.sum(p_p, axis=1, keepdims=True) + jnp.sum(p_n, axis=1, keepdims=True)
    o = (jnp.dot(p_p.astype(BF16), vc, preferred_element_type=F32)
         + jnp.dot(p_n.astype(BF16), vn_ref[0, 0], preferred_element_type=F32))
    o_ref[0, 0] = ((o / l) * _sigmoid(og_ref[0, 0].astype(F32))).astype(o_ref.dtype)


def fox_sample_attention(proj, k_cache_all, v_cache_all, layer, cum, *, row0, nseq, T, P, H, dh):
    G = SUBLANES if H % SUBLANES == 0 else H
    assert G & (G - 1) == 0
    NG = H // G
    R = T * G
    new = proj[row0:row0 + nseq * T].reshape(nseq, T, 4, NG, G, dh).transpose(2, 0, 3, 1, 4, 5)
    new = new.reshape(4, nseq, NG, R, dh)
    ck = cum.reshape(nseq, P + T, NG, G).transpose(0, 2, 1, 3).reshape(nseq, NG, 1, (P + T) * G)
    cq = cum[:, P:].reshape(nseq, T, NG, G).transpose(0, 2, 1, 3).reshape(nseq, NG, R, 1)
    part = lambda a: pl.BlockSpec((None, 1, 1, R, dh), lambda b, g: (a, b, g, 0, 0))
    cache = pl.BlockSpec((None, 1, P, G, dh), lambda b, g: (layer, b, 0, g, 0))
    out = pl.pallas_call(
        functools.partial(_fox_sample_kernel, P=P, T=T, G=G, dh=dh, scale=dh ** -0.5),
        grid=(nseq, NG),
        in_specs=[part(0), part(1), part(2), part(3), cache, cache,
                  pl.BlockSpec((1, 1, R, 1), lambda b, g: (b, g, 0, 0)),
                  pl.BlockSpec((1, 1, 1, (P + T) * G), lambda b, g: (b, g, 0, 0))],
        out_specs=pl.BlockSpec((1, 1, R, dh), lambda b, g: (b, g, 0, 0)),
        out_shape=jax.ShapeDtypeStruct((nseq, NG, R, dh), BF16),
        compiler_params=_params("parallel", "parallel"),
        name="fox_sample_attention",
    )(new, new, new, new, k_cache_all, v_cache_all, cq, ck)
    return out.reshape(nseq, NG, T, G, dh).transpose(0, 2, 1, 3, 4).reshape(nseq * T, H * dh)


def kernel(x_prompt, x_sample, state_gdn, state_gdn_conv, cache_fox_k, cache_fox_v, cache_fox_logf,
           norm_mix, norm_mlp, norm_final, gdn_w_in, gdn_conv_w, gdn_a_log, gdn_dt_bias, gdn_norm_w,
           gdn_w_out, fox_w_in, fox_b_f, fox_w_out, mlp_w_up, mlp_w_down):
    Bp, Tp, D = x_prompt.shape
    Bs, Ts, _ = x_sample.shape
    depth = norm_mix.shape[0]
    _, _, hv, dk, dv = state_gdn.shape
    conv_dim = state_gdn_conv.shape[-1]
    cw1 = state_gdn_conv.shape[2]
    v_dim = hv * dv
    qk_dim = (conv_dim - v_dim) // 2
    hqk = qk_dim // dk
    _, _, P, H, dh = cache_fox_k.shape
    fox_dim = H * dh
    d_ff = mlp_w_up.shape[-1]
    Mp, Ms = Bp * Tp, Bs * Ts
    Cp = min(GDN_CHUNK, Tp)

    x = jnp.concatenate([x_prompt.reshape(Mp, D), x_sample.reshape(Ms, D)], axis=0)
    gdn_w_in_t = jnp.swapaxes(gdn_w_in, 1, 2)
    fox_w_in_t = jnp.swapaxes(fox_w_in, 1, 2)
    S_p, S_s, buf_p, buf_s = None, None, [], []
    n_gdn = state_gdn.shape[0]
    k_p, k_s, v_p, v_s, lf_p, lf_s = [], [], [], [], [], []

    for i in range(depth):
        j = i // 2
        if i % 2 == 0:
            n_main = conv_dim + v_dim
            proj = norm_proj(x, norm_mix[i], gdn_w_in_t, j, n_main, w_t=True, name="gdn_in_proj")
            beta, g = gdn_gates(x, norm_mix[i], gdn_w_in_t, j, n_main, gdn_a_log[j], gdn_dt_bias[j])
            conv_p = gdn_conv(proj, gdn_conv_w, None, j, row0=0, nseq=Bp, T=Tp,
                              conv_dim=conv_dim, qk_dim=qk_dim, dk=dk)
            conv_s = gdn_conv(proj, gdn_conv_w, state_gdn_conv, j, row0=Mp, nseq=Bs, T=Ts,
                              conv_dim=conv_dim, qk_dim=qk_dim, dk=dk)

            def rows(a, B, T, C):
                return a.reshape(B, T // C, C, hv).transpose(0, 3, 1, 2)

            common = dict(hv=hv, hqk=hqk, dk=dk, dv=dv, qk_dim=qk_dim, z_col0=conv_dim)
            o, S_p = gdn_chunks(conv_p, proj, rows(g[:Mp], Bp, Tp, Cp), rows(beta[:Mp], Bp, Tp, Cp),
                                gdn_norm_w[j], None, j, n_gdn, S_p, row0=0, nseq=Bp, T=Tp, C=Cp,
                                out_rows=Mp + Ms, **common)
            o_s, S_s = gdn_chunks(conv_s, proj, rows(g[Mp:], Bs, Ts, Ts), rows(beta[Mp:], Bs, Ts, Ts),
                                  gdn_norm_w[j], state_gdn, j, n_gdn, S_s, row0=Mp, nseq=Bs, T=Ts, C=Ts,
                                  out_rows=Ms, **common)
            o = lax.dynamic_update_slice(o, o_s, (Mp, 0))
            x = matmul_res(o, gdn_w_out, j, x, name="gdn_out_proj")
            buf_p.append(jnp.stack([proj[(b + 1) * Tp - cw1:(b + 1) * Tp, :conv_dim]
                                    for b in range(Bp)]).astype(F32))
            qkv_s = proj[Mp:, :conv_dim].reshape(Bs, Ts, conv_dim).astype(F32)
            buf_s.append(jnp.concatenate([state_gdn_conv[j], qkv_s], axis=1)[:, -cw1:])
        else:
            proj = norm_proj(x, norm_mix[i], fox_w_in_t, j, 4 * fox_dim, w_t=True, name="fox_in_proj")
            logf = fox_logf(x, norm_mix[i], fox_w_in_t, j, 4 * fox_dim, fox_b_f[j])
            cum_p = cumsum_time(logf[:Mp].reshape(Bp, Tp, H))
            o = fox_prompt_attention(proj, cum_p.transpose(1, 0, 2).reshape(Mp, H),
                                     cum_p.transpose(2, 1, 0).reshape(H, Mp), nseq=Bp, T=Tp, H=H, dh=dh,
                                     out_rows=Mp + Ms)
            lf_new = logf[Mp:].reshape(Bs, Ts, H)
            lf_all = jnp.concatenate([cache_fox_logf[j], lf_new], axis=1)
            cum_s = cumsum_time(lf_all).transpose(1, 0, 2)
            o_s = fox_sample_attention(proj, cache_fox_k, cache_fox_v, j, cum_s,
                                       row0=Mp, nseq=Bs, T=Ts, P=P, H=H, dh=dh)
            o = lax.dynamic_update_slice(o, o_s, (Mp, 0))
            x = matmul_res(o, fox_w_out, j, x, name="fox_out_proj")
            kk = proj[:, fox_dim:2 * fox_dim].astype(F32)
            vv = proj[:, 2 * fox_dim:3 * fox_dim].astype(F32)
            k_p.append(kk[:Mp].reshape(Bp, Tp, H, dh))
            k_s.append(kk[Mp:].reshape(Bs, Ts, H, dh))
            v_p.append(vv[:Mp].reshape(Bp, Tp, H, dh))
            v_s.append(vv[Mp:].reshape(Bs, Ts, H, dh))
            lf_p.append(logf[:Mp].reshape(Bp, Tp, H))
            lf_s.append(lf_new)
        hmid = norm_proj(x, norm_mlp[i], mlp_w_up, i, d_ff, relu2=True, name="mlp_up")
        x = matmul_res(hmid, mlp_w_down, i, x, name="mlp_down")

    y_p = rmsnorm_rows(x, norm_final, 0, Mp)
    y_s = rmsnorm_rows(x, norm_final, Mp, Ms)
    st = jnp.stack
    return (y_p.reshape(Bp, Tp, D), y_s.reshape(Bs, Ts, D),
            S_p, st(buf_p), st(k_p), st(v_p), st(lf_p),
            S_s, st(buf_s), st(k_s), st(v_s), st(lf_s))
```

```python
import functools

import jax
import jax.numpy as jnp
from jax import lax
from jax.experimental import pallas as pl
from jax.experimental.pallas import tpu as pltpu

F32 = jnp.float32
BF16 = jnp.bfloat16
EPS = 1e-6
GDN_CHUNK = 64
GDN_HEADS_PER_STEP = 32
INV_BASE_BLOCK = 8
VMEM_LIMIT_BYTES = 58 * 1024 * 1024
LOG2E = 1.4426950408889634
SUBLANES = 8
HIGHEST = lax.Precision.HIGHEST
NT_DIMS = (((1,), (1,)), ((), ()))
TN_DIMS = (((0,), (0,)), ((), ()))


def _pick(n, pref, align):
    best = None
    for d in range(align, min(n, pref) + 1, align):
        if n % d == 0:
            best = d
    return n if best is None else best


def _params(*sem):
    return pltpu.CompilerParams(dimension_semantics=sem, vmem_limit_bytes=VMEM_LIMIT_BYTES)


def _sigmoid(x):
    return 1.0 / (1.0 + jnp.exp(-x))


def _softplus(x):
    return jnp.maximum(x, 0.0) + jnp.log1p(jnp.exp(-jnp.abs(x)))


def _rms_rows(x, w):
    ms = jnp.mean(x * x, axis=-1, keepdims=True)
    return x * lax.rsqrt(ms + EPS) * w


def _bdot(a, b):
    return jnp.dot(a.astype(BF16), b.astype(BF16), preferred_element_type=F32)


def _wdot(xn, w_ref, w_t):
    w = w_ref[...].astype(BF16)
    if w_t:
        return lax.dot_general(xn, w, NT_DIMS, preferred_element_type=F32)
    return jnp.dot(xn, w, preferred_element_type=F32)


def _norm_proj_kernel(x_ref, nw_ref, w_ref, o_ref, xn_ref, *, relu2, w_t):
    @pl.when(pl.program_id(1) == 0)
    def _():
        tm = x_ref.shape[0]
        rows = _pick(tm, 272, 16)

        def body(r, carry):
            sl = pl.ds(pl.multiple_of(r * rows, rows), rows)
            xn_ref[sl, :] = _rms_rows(x_ref[sl, :], nw_ref[...]).astype(BF16)
            return carry

        lax.fori_loop(0, tm // rows, body, 0)

    acc = _wdot(xn_ref[...], w_ref, w_t)
    if relu2:
        acc = jnp.square(jnp.maximum(acc, 0.0))
    o_ref[...] = acc.astype(o_ref.dtype)


def norm_proj(x, nw, w_all, layer, n_cols, *, relu2=False, w_t=False, name="norm_proj"):
    M, K = x.shape
    tm = _pick(M, 1088, 16)
    tn = _pick(n_cols, 1024, 128)
    if w_t:
        w_spec = pl.BlockSpec((None, tn, K), lambda i, j: (layer, j, 0))
    else:
        w_spec = pl.BlockSpec((None, K, tn), lambda i, j: (layer, 0, j))
    return pl.pallas_call(
        functools.partial(_norm_proj_kernel, relu2=relu2, w_t=w_t),
        grid=(M // tm, n_cols // tn),
        in_specs=[pl.BlockSpec((tm, K), lambda i, j: (i, 0)),
                  pl.BlockSpec((1, K), lambda i, j: (0, 0)),
                  w_spec],
        out_specs=pl.BlockSpec((tm, tn), lambda i, j: (i, j)),
        out_shape=jax.ShapeDtypeStruct((M, n_cols), BF16),
        scratch_shapes=[pltpu.VMEM((tm, K), BF16)],
        compiler_params=_params("parallel", "arbitrary"),
        name=name,
    )(x, nw.reshape(1, K), w_all)


LANES = 128


def _gdn_gate_kernel(x_ref, nw_ref, w_ref, alog_ref, dtb_ref, beta_ref, g_ref, *, hv):
    xn = _rms_rows(x_ref[...], nw_ref[...]).astype(BF16)
    acc = _wdot(xn, w_ref, True)
    beta_ref[...] = _sigmoid(acc[:, :hv])
    g_ref[...] = -jnp.exp(alog_ref[...]) * _softplus(acc[:, hv:2 * hv] + dtb_ref[...])


def gdn_gates(x, nw, wt_all, layer, col0, a_log, dt_bias):
    M, K = x.shape
    H = a_log.shape[0]
    assert col0 % LANES == 0 and 2 * H <= LANES
    tm = _pick(M, 544, 8)
    row = lambda i: (i, 0)
    fix = lambda i: (0, 0)
    return pl.pallas_call(
        functools.partial(_gdn_gate_kernel, hv=H),
        grid=(M // tm,),
        in_specs=[pl.BlockSpec((tm, K), row), pl.BlockSpec((1, K), fix),
                  pl.BlockSpec((None, LANES, K), lambda i: (layer, col0 // LANES, 0)),
                  pl.BlockSpec((1, H), fix), pl.BlockSpec((1, H), fix)],
        out_specs=[pl.BlockSpec((tm, H), row), pl.BlockSpec((tm, H), row)],
        out_shape=[jax.ShapeDtypeStruct((M, H), F32)] * 2,
        compiler_params=_params("parallel"),
        name="gdn_gates",
    )(x, nw.reshape(1, K), wt_all, a_log.reshape(1, H), dt_bias.reshape(1, H))


def _fox_logf_kernel(x_ref, nw_ref, w_ref, bf_ref, lf_ref, *, H):
    xn = _rms_rows(x_ref[...], nw_ref[...]).astype(BF16)
    acc = _wdot(xn, w_ref, True)
    f = acc[:, :H] + bf_ref[...]
    lf_ref[...] = -_softplus(-f)


def fox_logf(x, nw, wt_all, layer, col0, b_f):
    M, K = x.shape
    H = b_f.shape[0]
    assert col0 % LANES == 0 and H <= LANES
    tm = _pick(M, 544, 8)
    row = lambda i: (i, 0)
    fix = lambda i: (0, 0)
    return pl.pallas_call(
        functools.partial(_fox_logf_kernel, H=H),
        grid=(M // tm,),
        in_specs=[pl.BlockSpec((tm, K), row), pl.BlockSpec((1, K), fix),
                  pl.BlockSpec((None, LANES, K), lambda i: (layer, col0 // LANES, 0)),
                  pl.BlockSpec((1, H), fix)],
        out_specs=pl.BlockSpec((tm, H), row),
        out_shape=jax.ShapeDtypeStruct((M, H), F32),
        compiler_params=_params("parallel"),
        name="fox_logf",
    )(x, nw.reshape(1, K), wt_all, b_f.reshape(1, H))


def _matmul_res_kernel(a_ref, w_ref, r_ref, o_ref, acc_ref, *, nk):
    part = jnp.dot(a_ref[...], w_ref[...].astype(BF16), preferred_element_type=F32)
    if nk == 1:
        o_ref[...] = r_ref[...] + part
        return
    k = pl.program_id(2)

    @pl.when(k == 0)
    def _():
        acc_ref[...] = part

    @pl.when(k > 0)
    def _():
        acc_ref[...] += part

    @pl.when(k == nk - 1)
    def _():
        o_ref[...] = r_ref[...] + acc_ref[...]


def matmul_res(a, w_all, layer, res, *, name="matmul_res"):
    M, K = a.shape
    N = w_all.shape[-1]
    tm = _pick(M, 1088, 16)
    tn = _pick(N, 512, 128)
    tk = _pick(K, 4096, 128)
    nk = K // tk
    return pl.pallas_call(
        functools.partial(_matmul_res_kernel, nk=nk),
        grid=(M // tm, N // tn, nk),
        in_specs=[pl.BlockSpec((tm, tk), lambda i, j, k: (i, k)),
                  pl.BlockSpec((None, tk, tn), lambda i, j, k: (layer, k, j)),
                  pl.BlockSpec((tm, tn), lambda i, j, k: (i, j))],
        out_specs=pl.BlockSpec((tm, tn), lambda i, j, k: (i, j)),
        out_shape=jax.ShapeDtypeStruct((M, N), F32),
        scratch_shapes=[pltpu.VMEM((tm, tn), F32)],
        compiler_params=_params("parallel", "parallel", "arbitrary"),
        name=name,
    )(a, w_all, res)


def _rmsnorm_kernel(x_ref, nw_ref, o_ref):
    o_ref[...] = _rms_rows(x_ref[...], nw_ref[...])


def rmsnorm_rows(x, nw, row0, n_rows):
    K = x.shape[1]
    tm = _pick(n_rows, 512, 8)
    rb0 = row0 // tm
    return pl.pallas_call(
        _rmsnorm_kernel,
        grid=(n_rows // tm,),
        in_specs=[pl.BlockSpec((tm, K), lambda i: (rb0 + i, 0)), pl.BlockSpec((1, K), lambda i: (0, 0))],
        out_specs=pl.BlockSpec((tm, K), lambda i: (i, 0)),
        out_shape=jax.ShapeDtypeStruct((n_rows, K), F32),
        compiler_params=_params("parallel"),
        name="final_rmsnorm",
    )(x, nw.reshape(1, K))


CONV_HALO = 8
CONV_SINGLE_BLOCK_ELEMS = 256 * 1024


def _gdn_conv_kernel(x_ref, w_ref, *rest, tb, cb, dk, v_dim, n_q_blocks, n_qk_blocks, has_buf, q_scale):
    if has_buf:
        buf_ref, o_ref, xc_ref = rest
    else:
        o_ref, xc_ref = rest
    cw = w_ref.shape[0]
    lo = CONV_HALO - (cw - 1)

    @pl.when(pl.program_id(2) == 0)
    def _():
        if has_buf:
            xc_ref[lo:CONV_HALO, :] = buf_ref[0]
        else:
            xc_ref[lo:CONV_HALO, :] = jnp.zeros((cw - 1, cb), F32)

    xc_ref[CONV_HALO:CONV_HALO + tb, :] = x_ref[...].astype(F32)
    acc = xc_ref[lo:lo + tb, :] * w_ref[0:1, :]
    for j in range(1, cw):
        acc = acc + xc_ref[lo + j:lo + j + tb, :] * w_ref[j:j + 1, :]
    y = acc * _sigmoid(acc)
    xc_ref[lo:CONV_HALO, :] = xc_ref[lo + tb:CONV_HALO + tb, :]

    def l2(heads, scale):
        for hh in heads:
            yh = y[:, hh * dk:(hh + 1) * dk]
            ss = jnp.sum(yh * yh, axis=-1, keepdims=True)
            o_ref[:, hh * dk:(hh + 1) * dk] = yh * (lax.rsqrt(ss + EPS) * scale)

    if n_qk_blocks == 0:
        nq = (cb - v_dim) // (2 * dk)
        l2(range(nq), q_scale)
        l2(range(nq, 2 * nq), 1.0)
        o_ref[:, 2 * nq * dk:] = y[:, 2 * nq * dk:]
        return

    c = pl.program_id(1)

    @pl.when(c < n_q_blocks)
    def _():
        l2(range(cb // dk), q_scale)

    @pl.when(jnp.logical_and(c >= n_q_blocks, c < n_qk_blocks))
    def _():
        l2(range(cb // dk), 1.0)

    @pl.when(c >= n_qk_blocks)
    def _():
        o_ref[...] = y


def gdn_conv(proj, conv_w_all, buf_all, layer, *, row0, nseq, T, conv_dim, qk_dim, dk):
    cw = conv_w_all.shape[1]
    tb = _pick(T, 512, 8)
    cb = conv_dim if T * conv_dim <= CONV_SINGLE_BLOCK_ELEMS else _pick(qk_dim, 1024, dk)
    nt = T // tb
    rb0 = row0 // tb
    has_buf = buf_all is not None
    in_specs = [pl.BlockSpec((tb, cb), lambda s, c, t: (rb0 + s * nt + t, c)),
                pl.BlockSpec((None, cw, cb), lambda s, c, t: (layer, 0, c))]
    args = [proj, conv_w_all]
    if has_buf:
        in_specs.append(pl.BlockSpec((None, 1, cw - 1, cb), lambda s, c, t: (layer, s, 0, c)))
        args.append(buf_all)
    return pl.pallas_call(
        functools.partial(_gdn_conv_kernel, tb=tb, cb=cb, dk=dk, v_dim=conv_dim - 2 * qk_dim,
                          n_q_blocks=qk_dim // cb,
                          n_qk_blocks=2 * qk_dim // cb, has_buf=has_buf, q_scale=dk ** -0.5),
        grid=(nseq, conv_dim // cb, nt),
        in_specs=in_specs,
        out_specs=pl.BlockSpec((tb, cb), lambda s, c, t: (s * nt + t, c)),
        out_shape=jax.ShapeDtypeStruct((nseq * T, conv_dim), F32),
        scratch_shapes=[pltpu.VMEM((CONV_HALO + tb, cb), F32)],
        compiler_params=_params("parallel", "parallel", "arbitrary"),
        name="gdn_conv",
    )(*args)


def _gdn_chunk_kernel(q_ref, k_ref, v_ref, z_ref, g_ref, b_ref, nw_ref, *rest,
                      C, hq, rep, dk, dv, n_chunks, has_s0, has_prev):
    rest = list(rest)
    s0_ref = rest.pop(0) if has_s0 else None
    if has_prev:
        rest.pop(0)
    o_ref, sout_ref, s_ref = rest
    n = pl.program_id(2)
    C2 = 2 * C

    @pl.when(n == 0)
    def _():
        if has_s0:
            s_ref[...] = s0_ref[0]
        else:
            s_ref[...] = jnp.zeros_like(s_ref)

    row = lax.broadcasted_iota(jnp.int32, (C, C2), 0)
    lane = lax.broadcasted_iota(jnp.int32, (C, C2), 1)
    right = lane >= C
    col = jnp.where(right, lane - C, lane)
    incl = row >= col
    strict = row > col
    diag = lane == row
    eye_right = jnp.where(lane == row + C, 1.0, 0.0).astype(F32)
    tr = lax.broadcasted_iota(jnp.int32, (C, C2), 0)
    triu2 = jnp.where(tr <= col, 1.0, 0.0).astype(F32)
    zeros_z = jnp.zeros((C, C2), BF16)
    zeros_r = jnp.zeros((C, dk + dv), BF16)

    def to_col(r2):
        return jnp.sum(jnp.where(diag, r2, 0.0), axis=1, keepdims=True)

    diag_c = (lax.broadcasted_iota(jnp.int32, (C, C), 0) == lax.broadcasted_iota(jnp.int32, (C, C), 1))

    def to_col_c(r):
        return jnp.sum(jnp.where(diag_c, r, 0.0), axis=1, keepdims=True)

    heads = range(hq * rep)
    q = [q_ref[:, a * dk:(a + 1) * dk] for a in range(hq)]
    k = [k_ref[:, a * dk:(a + 1) * dk] for a in range(hq)]
    q16 = [x.astype(BF16) for x in q]
    kk16 = [jnp.concatenate([x.astype(BF16)] * 2, axis=0) for x in k]
    G2, G_col, b_col, dec_incl, gamma, G_last, kb, vb = [], [], [], [], [], [], [], []
    for i in heads:
        g_row = g_ref[0, i, pl.ds(n, 1), :]
        b_row = b_ref[0, i, pl.ds(n, 1), :]
        G2.append(jnp.dot(g_row, triu2, precision=HIGHEST, preferred_element_type=F32))
        G_col.append(to_col(G2[i]))
        b_col.append(to_col_c(b_row))
        dec_incl.append(jnp.exp(jnp.where(incl, G_col[i] - G2[i], -jnp.inf)))
        gamma.append(jnp.exp(G_col[i]))
        G_last.append(G2[i][:, C - 1:C])
        kb.append(k[i // rep] * b_col[i])
        vb.append(v_ref[:, i * dv:(i + 1) * dv] * b_col[i])
    gram = [lax.dot_general(jnp.concatenate([kb[i].astype(BF16), q16[i // rep]], axis=0), kk16[i // rep],
                            NT_DIMS, preferred_element_type=F32) for i in heads]
    Lf = [gram[i][:C] * dec_incl[i] for i in heads]
    base = min(INV_BASE_BLOCK, C)
    sh = base.bit_length() - 1
    in_base = jnp.logical_and(strict, (row >> sh) == (col >> sh))
    Z = [jnp.where(right, eye_right, -jnp.where(in_base, Lf[i], 0.0)) for i in heads]
    for _ in range((base - 1).bit_length()):
        Z16 = [z.astype(BF16) for z in Z]
        Z = [jnp.dot(Z16[i], jnp.concatenate([Z16[i], zeros_z], axis=0), preferred_element_type=F32)
             + jnp.where(right, Z[i], 0.0) for i in heads]
    b = base
    while b < C:
        sh = b.bit_length() - 1
        lower_left = jnp.logical_and(jnp.logical_and(right, (row >> (sh + 1)) == (col >> (sh + 1))),
                                     jnp.logical_and(((row >> sh) & 1) == 1, ((col >> sh) & 1) == 0))
        Z16 = [z.astype(BF16) for z in Z]
        W16 = [jnp.dot(jnp.where(lower_left, Lf[i], 0.0).astype(BF16),
                       jnp.concatenate([zeros_z, Z16[i]], axis=0),
                       preferred_element_type=F32).astype(BF16) for i in heads]
        Z = [Z[i] - jnp.dot(Z16[i], jnp.concatenate([zeros_z, W16[i]], axis=0), preferred_element_type=F32)
             for i in heads]
        b *= 2
    wu = []
    for i in heads:
        rhs = jnp.concatenate([(kb[i] * gamma[i]).astype(BF16), vb[i].astype(BF16)], axis=1)
        wu.append(jnp.dot(Z[i].astype(BF16), jnp.concatenate([zeros_r, rhs], axis=0),
                          preferred_element_type=F32))
    S = [s_ref[i] for i in heads]
    S16 = [x.astype(BF16) for x in S]
    ws_qs = [jnp.dot(jnp.concatenate([wu[i][:, :dk].astype(BF16), (q[i // rep] * gamma[i]).astype(BF16)], axis=0),
                     S16[i], preferred_element_type=F32) for i in heads]
    U16 = [(wu[i][:, dk:] - ws_qs[i][:C]).astype(BF16) for i in heads]
    A16 = [(gram[i][C:] * dec_incl[i])[:, :C].astype(BF16) for i in heads]
    o = [ws_qs[i][C:] + jnp.dot(A16[i], U16[i], preferred_element_type=F32) for i in heads]
    for i in heads:
        kd = k[i // rep] * jnp.exp(G_last[i] - G_col[i])
        s_ref[i] = jnp.exp(G_last[i]) * S[i] + lax.dot_general(kd.astype(BF16), U16[i], TN_DIMS,
                                                               preferred_element_type=F32)
    for i in heads:
        z = z_ref[:, i * dv:(i + 1) * dv].astype(F32)
        o_ref[:, i * dv:(i + 1) * dv] = (_rms_rows(o[i], nw_ref[...]) * (z * _sigmoid(z))).astype(o_ref.dtype)

    @pl.when(n == n_chunks - 1)
    def _():
        sout_ref[0] = s_ref[...]


def gdn_chunks(conv, proj, g_rows, b_rows, norm_w, s0_all, layer, n_layers, s_prev, *, row0, nseq, T, C,
               hv, hqk, dk, dv, qk_dim, z_col0, out_rows):
    N = T // C
    rep = hv // hqk
    hb = min(GDN_HEADS_PER_STEP, hv)
    hq = hb // rep
    has_s0 = s0_all is not None
    kb0 = qk_dim // (hq * dk)
    vb0 = 2 * qk_dim // (hb * dv)
    zb0 = z_col0 // (hb * dv)
    rz0 = row0 // C
    in_specs = [pl.BlockSpec((C, hq * dk), lambda b, h, n: (b * N + n, h)),
                pl.BlockSpec((C, hq * dk), lambda b, h, n: (b * N + n, kb0 + h)),
                pl.BlockSpec((C, hb * dv), lambda b, h, n: (b * N + n, vb0 + h)),
                pl.BlockSpec((C, hb * dv), lambda b, h, n: (rz0 + b * N + n, zb0 + h)),
                pl.BlockSpec((1, hb, N, C), lambda b, h, n: (b, h, 0, 0)),
                pl.BlockSpec((1, hb, N, C), lambda b, h, n: (b, h, 0, 0)),
                pl.BlockSpec((1, dv), lambda b, h, n: (0, 0))]
    args = [conv, conv, conv, proj, g_rows, b_rows, norm_w.reshape(1, dv)]
    if has_s0:
        in_specs.append(pl.BlockSpec((None, 1, hb, dk, dv), lambda b, h, n: (layer, b, h, 0, 0)))
        args.append(s0_all)
    aliases = {}
    if s_prev is not None:
        aliases[len(args)] = 1
        in_specs.append(pl.BlockSpec(memory_space=pl.ANY))
        args.append(s_prev)
    return pl.pallas_call(
        functools.partial(_gdn_chunk_kernel, C=C, hq=hq, rep=rep, dk=dk, dv=dv, n_chunks=N,
                          has_s0=has_s0, has_prev=s_prev is not None),
        grid=(nseq, hv // hb, N),
        in_specs=in_specs,
        out_specs=[pl.BlockSpec((C, hb * dv), lambda b, h, n: (b * N + n, h)),
                   pl.BlockSpec((None, 1, hb, dk, dv), lambda b, h, n: (layer, b, h, 0, 0))],
        out_shape=[jax.ShapeDtypeStruct((out_rows, hv * dv), BF16),
                   jax.ShapeDtypeStruct((n_layers, nseq, hv, dk, dv), F32)],
        input_output_aliases=aliases,
        scratch_shapes=[pltpu.VMEM((hb, dk, dv), F32)],
        compiler_params=_params("parallel", "parallel", "arbitrary"),
        name="gdn_chunks",
    )(*args)


def _cumsum_kernel(x_ref, o_ref, carry_ref, *, tb):
    @pl.when(pl.program_id(0) == 0)
    def _():
        carry_ref[...] = jnp.zeros_like(carry_ref)

    row = lax.broadcasted_iota(jnp.int32, (tb, tb), 0)
    col = lax.broadcasted_iota(jnp.int32, (tb, tb), 1)
    tril = jnp.where(row >= col, 1.0, 0.0).astype(F32)
    c = jnp.dot(tril, x_ref[...], precision=HIGHEST, preferred_element_type=F32) + carry_ref[...]
    o_ref[...] = c
    carry_ref[...] = c[tb - 1:tb, :]


def cumsum_time(x):
    nseq, L, H = x.shape
    S = nseq * H
    tb = _pick(L, 256, 8)
    out = pl.pallas_call(
        functools.partial(_cumsum_kernel, tb=tb),
        grid=(L // tb,),
        in_specs=[pl.BlockSpec((tb, S), lambda t: (t, 0))],
        out_specs=pl.BlockSpec((tb, S), lambda t: (t, 0)),
        out_shape=jax.ShapeDtypeStruct((L, S), F32),
        scratch_shapes=[pltpu.VMEM((1, S), F32)],
        compiler_params=_params("arbitrary"),
        name="fox_cumsum",
    )(x.transpose(1, 0, 2).reshape(L, S))
    return out.reshape(L, nseq, H)


def _fox_prompt_kernel(q_ref, k_ref, v_ref, og_ref, cq_ref, ckt_ref, o_ref, acc_ref, m_ref, cqr_ref, q2_ref,
                       *, bq, H, dh, scale, hg):
    i = pl.program_id(1)
    j = pl.program_id(2)
    rep = bq // dh

    @pl.when(j == 0)
    def _():
        m_ref[...] = jnp.full(m_ref.shape, -jnp.inf, F32)
        acc_ref[...] = jnp.zeros_like(acc_ref)
        cqs = cq_ref[...] * LOG2E
        for h in range(H):
            cqr_ref[h] = jnp.broadcast_to(cqs[:, h:h + 1], (bq, dh))
        q2_ref[...] = (q_ref[...].astype(F32) * (scale * LOG2E)).astype(BF16)

    def block(masked):
        if masked:
            mask = lax.broadcasted_iota(jnp.int32, (bq, bq), 1) <= lax.broadcasted_iota(jnp.int32, (bq, bq), 0)
        ones16 = jnp.ones((bq, dh), BF16)
        ck2 = ckt_ref[...] * LOG2E

        def qk(h):
            hs = slice(h * dh, (h + 1) * dh)
            return lax.dot_general(q2_ref[:, hs], k_ref[:, hs], NT_DIMS, preferred_element_type=F32)

        groups = [range(g * hg, (g + 1) * hg) for g in range(H // hg)]
        s = {h: qk(h) for h in groups[0]}
        for gi, grp in enumerate(groups):
            if gi + 1 < len(groups):
                for h in groups[gi + 1]:
                    s[h] = qk(h)
            p16, alpha2 = {}, {}
            for h in grp:
                t = s.pop(h) - ck2[h:h + 1, :]
                if masked:
                    t = jnp.where(mask, t, -jnp.inf)
                cq2 = cqr_ref[h]
                m_prev = m_ref[h]
                m_new = jnp.maximum(m_prev, jnp.max(t, axis=1, keepdims=True) + cq2)
                m_ref[h] = m_new
                alpha = jnp.exp2(m_prev - m_new)
                c = cq2 - m_new
                p16[h] = jnp.exp2(t + jnp.concatenate([c] * rep, axis=1)).astype(BF16)
                alpha2[h] = jnp.concatenate([alpha, alpha], axis=1)
            for h in grp:
                hs = slice(h * dh, (h + 1) * dh)
                v_ext = jnp.concatenate([v_ref[:, hs], ones16], axis=1)
                acc_ref[h] = alpha2[h] * acc_ref[h] + jnp.dot(p16[h], v_ext, preferred_element_type=F32)

    @pl.when(j < i)
    def _():
        block(False)

    @pl.when(j == i)
    def _():
        block(True)
        for h in range(H):
            hs = slice(h * dh, (h + 1) * dh)
            a = acc_ref[h]
            o_ref[:, hs] = ((a[:, :dh] / a[:, dh:]) * _sigmoid(og_ref[:, hs].astype(F32))).astype(o_ref.dtype)


def fox_prompt_attention(proj, cum, cum_t, *, nseq, T, H, dh, out_rows):
    bq = _pick(T, 256, 128)
    nq = T // bq
    D = H * dh
    hg = _pick(H, 2, 1)
    kv = lambda col: (lambda b, i, j: (b * nq + jnp.minimum(j, i), col))
    return pl.pallas_call(
        functools.partial(_fox_prompt_kernel, bq=bq, H=H, dh=dh, scale=dh ** -0.5, hg=hg),
        grid=(nseq, nq, nq),
        in_specs=[pl.BlockSpec((bq, D), lambda b, i, j: (b * nq + i, 0)),
                  pl.BlockSpec((bq, D), kv(1)),
                  pl.BlockSpec((bq, D), kv(2)),
                  pl.BlockSpec((bq, D), lambda b, i, j: (b * nq + i, 3)),
                  pl.BlockSpec((bq, H), lambda b, i, j: (b * nq + i, 0)),
                  pl.BlockSpec((H, bq), lambda b, i, j: (0, b * nq + jnp.minimum(j, i)))],
        out_specs=pl.BlockSpec((bq, D), lambda b, i, j: (b * nq + i, 0)),
        out_shape=jax.ShapeDtypeStruct((out_rows, D), BF16),
        scratch_shapes=[pltpu.VMEM((H, bq, 2 * dh), F32), pltpu.VMEM((H, bq, dh), F32),
                        pltpu.VMEM((H, bq, dh), F32), pltpu.VMEM((bq, D), BF16)],
        compiler_params=_params("parallel", "parallel", "arbitrary"),
        name="fox_prompt_attention",
    )(proj, proj, proj, proj, cum, cum_t)


def _fox_sample_kernel(q_ref, kn_ref, vn_ref, og_ref, kc_ref, vc_ref, cq_ref, ck_ref, o_ref,
                       *, P, T, G, dh, scale):
    R, PG = T * G, P * G
    gs = G.bit_length() - 1
    q = (q_ref[0, 0].astype(F32) * scale).astype(BF16)
    kc = kc_ref[0].reshape(PG, dh).astype(BF16)
    vc = vc_ref[0].reshape(PG, dh).astype(BF16)
    s_p = lax.dot_general(q, kc, NT_DIMS, preferred_element_type=F32)
    s_n = lax.dot_general(q, kn_ref[0, 0], NT_DIMS, preferred_element_type=F32)
    cq = cq_ref[0, 0]
    ck = ck_ref[0, 0]
    same_p = ((lax.broadcasted_iota(jnp.int32, (R, PG), 0) & (G - 1))
              == (lax.broadcasted_iota(jnp.int32, (R, PG), 1) & (G - 1)))
    rn = lax.broadcasted_iota(jnp.int32, (R, R), 0)
    cn = lax.broadcasted_iota(jnp.int32, (R, R), 1)
    same_n = jnp.logical_and((rn & (G - 1)) == (cn & (G - 1)), (cn >> gs) <= (rn >> gs))
    s_p = jnp.where(same_p, s_p + (cq - ck[:, :PG]), -jnp.inf)
    s_n = jnp.where(same_n, s_n + (cq - ck[:, PG:]), -jnp.inf)
    m = jnp.maximum(jnp.max(s_p, axis=1, keepdims=True), jnp.max(s_n, axis=1, keepdims=True))
    p_p = jnp.exp(s_p - m)
    p_n = jnp.exp(s_n - m)
    l = jnp.sum(p_p, axis=1, keepdims=True) + jnp.sum(p_n, axis=1, keepdims=True)
    o = (jnp.dot(p_p.astype(BF16), vc, preferred_element_type=F32)
         + jnp.dot(p_n.astype(BF16), vn_ref[0, 0], preferred_element_type=F32))
    o_ref[0, 0] = ((o / l) * _sigmoid(og_ref[0, 0].astype(F32))).astype(o_ref.dtype)


def fox_sample_attention(proj, k_cache_all, v_cache_all, layer, cum, *, row0, nseq, T, P, H, dh):
    G = SUBLANES if H % SUBLANES == 0 else H
    assert G & (G - 1) == 0
    NG = H // G
    R = T * G
    new = proj[row0:row0 + nseq * T].reshape(nseq, T, 4, NG, G, dh).transpose(2, 0, 3, 1, 4, 5)
    new = new.reshape(4, nseq, NG, R, dh)
    ck = cum.reshape(nseq, P + T, NG, G).transpose(0, 2, 1, 3).reshape(nseq, NG, 1, (P + T) * G)
    cq = cum[:, P:].reshape(nseq, T, NG, G).transpose(0, 2, 1, 3).reshape(nseq, NG, R, 1)
    part = lambda a: pl.BlockSpec((None, 1, 1, R, dh), lambda b, g: (a, b, g, 0, 0))
    cache = pl.BlockSpec((None, 1, P, G, dh), lambda b, g: (layer, b, 0, g, 0))
    out = pl.pallas_call(
        functools.partial(_fox_sample_kernel, P=P, T=T, G=G, dh=dh, scale=dh ** -0.5),
        grid=(nseq, NG),
        in_specs=[part(0), part(1), part(2), part(3), cache, cache,
                  pl.BlockSpec((1, 1, R, 1), lambda b, g: (b, g, 0, 0)),
                  pl.BlockSpec((1, 1, 1, (P + T) * G), lambda b, g: (b, g, 0, 0))],
        out_specs=pl.BlockSpec((1, 1, R, dh), lambda b, g: (b, g, 0, 0)),
        out_shape=jax.ShapeDtypeStruct((nseq, NG, R, dh), BF16),
        compiler_params=_params("parallel", "parallel"),
        name="fox_sample_attention",
    )(new, new, new, new, k_cache_all, v_cache_all, cq, ck)
    return out.reshape(nseq, NG, T, G, dh).transpose(0, 2, 1, 3, 4).reshape(nseq * T, H * dh)


def kernel(x_prompt, x_sample, state_gdn, state_gdn_conv, cache_fox_k, cache_fox_v, cache_fox_logf,
           norm_mix, norm_mlp, norm_final, gdn_w_in, gdn_conv_w, gdn_a_log, gdn_dt_bias, gdn_norm_w,
           gdn_w_out, fox_w_in, fox_b_f, fox_w_out, mlp_w_up, mlp_w_down):
    Bp, Tp, D = x_prompt.shape
    Bs, Ts, _ = x_sample.shape
    depth = norm_mix.shape[0]
    _, _, hv, dk, dv = state_gdn.shape
    conv_dim = state_gdn_conv.shape[-1]
    cw1 = state_gdn_conv.shape[2]
    v_dim = hv * dv
    qk_dim = (conv_dim - v_dim) // 2
    hqk = qk_dim // dk
    _, _, P, H, dh = cache_fox_k.shape
    fox_dim = H * dh
    d_ff = mlp_w_up.shape[-1]
    Mp, Ms = Bp * Tp, Bs * Ts
    Cp = min(GDN_CHUNK, Tp)

    x = jnp.concatenate([x_prompt.reshape(Mp, D), x_sample.reshape(Ms, D)], axis=0)
    gdn_w_in_t = jnp.swapaxes(gdn_w_in, 1, 2)
    fox_w_in_t = jnp.swapaxes(fox_w_in, 1, 2)
    S_p, S_s, buf_p, buf_s = None, None, [], []
    n_gdn = state_gdn.shape[0]
    k_p, k_s, v_p, v_s, lf_p, lf_s = [], [], [], [], [], []

    for i in range(depth):
        j = i // 2
        if i % 2 == 0:
            n_main = conv_dim + v_dim
            proj = norm_proj(x, norm_mix[i], gdn_w_in_t, j, n_main, w_t=True, name="gdn_in_proj")
            beta, g = gdn_gates(x, norm_mix[i], gdn_w_in_t, j, n_main, gdn_a_log[j], gdn_dt_bias[j])
            conv_p = gdn_conv(proj, gdn_conv_w, None, j, row0=0, nseq=Bp, T=Tp,
                              conv_dim=conv_dim, qk_dim=qk_dim, dk=dk)
            conv_s = gdn_conv(proj, gdn_conv_w, state_gdn_conv, j, row0=Mp, nseq=Bs, T=Ts,
                              conv_dim=conv_dim, qk_dim=qk_dim, dk=dk)

            def rows(a, B, T, C):
                return a.reshape(B, T // C, C, hv).transpose(0, 3, 1, 2)

            common = dict(hv=hv, hqk=hqk, dk=dk, dv=dv, qk_dim=qk_dim, z_col0=conv_dim)
            o, S_p = gdn_chunks(conv_p, proj, rows(g[:Mp], Bp, Tp, Cp), rows(beta[:Mp], Bp, Tp, Cp),
                                gdn_norm_w[j], None, j, n_gdn, S_p, row0=0, nseq=Bp, T=Tp, C=Cp,
                                out_rows=Mp + Ms, **common)
            o_s, S_s = gdn_chunks(conv_s, proj, rows(g[Mp:], Bs, Ts, Ts), rows(beta[Mp:], Bs, Ts, Ts),
                                  gdn_norm_w[j], state_gdn, j, n_gdn, S_s, row0=Mp, nseq=Bs, T=Ts, C=Ts,
                                  out_rows=Ms, **common)
            o = lax.dynamic_update_slice(o, o_s, (Mp, 0))
            x = matmul_res(o, gdn_w_out, j, x, name="gdn_out_proj")
            buf_p.append(jnp.stack([proj[(b + 1) * Tp - cw1:(b + 1) * Tp, :conv_dim]
                                    for b in range(Bp)]).astype(F32))
            qkv_s = proj[Mp:, :conv_dim].reshape(Bs, Ts, conv_dim).astype(F32)
            buf_s.append(jnp.concatenate([state_gdn_conv[j], qkv_s], axis=1)[:, -cw1:])
        else:
            proj = norm_proj(x, norm_mix[i], fox_w_in_t, j, 4 * fox_dim, w_t=True, name="fox_in_proj")
            logf = fox_logf(x, norm_mix[i], fox_w_in_t, j, 4 * fox_dim, fox_b_f[j])
            cum_p = cumsum_time(logf[:Mp].reshape(Bp, Tp, H))
            o = fox_prompt_attention(proj, cum_p.transpose(1, 0, 2).reshape(Mp, H),
                                     cum_p.transpose(2, 1, 0).reshape(H, Mp), nseq=Bp, T=Tp, H=H, dh=dh,
                                     out_rows=Mp + Ms)
            lf_new = logf[Mp:].reshape(Bs, Ts, H)
            lf_all = jnp.concatenate([cache_fox_logf[j], lf_new], axis=1)
            cum_s = cumsum_time(lf_all).transpose(1, 0, 2)
            o_s = fox_sample_attention(proj, cache_fox_k, cache_fox_v, j, cum_s,
                                       row0=Mp, nseq=Bs, T=Ts, P=P, H=H, dh=dh)
            o = lax.dynamic_update_slice(o, o_s, (Mp, 0))
            x = matmul_res(o, fox_w_out, j, x, name="fox_out_proj")
            kv = proj[:, fox_dim:3 * fox_dim].reshape(Mp + Ms, 2, H, dh)
            k_p.append(kv[:Mp, 0].astype(F32).reshape(Bp, Tp, H, dh))
            k_s.append(kv[Mp:, 0].astype(F32).reshape(Bs, Ts, H, dh))
            v_p.append(kv[:Mp, 1].astype(F32).reshape(Bp, Tp, H, dh))
            v_s.append(kv[Mp:, 1].astype(F32).reshape(Bs, Ts, H, dh))
            lf_p.append(logf[:Mp].reshape(Bp, Tp, H))
            lf_s.append(lf_new)
        hmid = norm_proj(x, norm_mlp[i], mlp_w_up, i, d_ff, relu2=True, name="mlp_up")
        x = matmul_res(hmid, mlp_w_down, i, x, name="mlp_down")

    y_p = rmsnorm_rows(x, norm_final, 0, Mp)
    y_s = rmsnorm_rows(x, norm_final, Mp, Ms)
    st = jnp.stack
    return (y_p.reshape(Bp, Tp, D), y_s.reshape(Bs, Ts, D),
            S_p, st(buf_p), st(k_p), st(v_p), st(lf_p),
            S_s, st(buf_s), st(k_s), st(v_s), st(lf_s))
```

```python
import functools

import jax
import jax.numpy as jnp
from jax import lax
from jax.experimental import pallas as pl
from jax.experimental.pallas import tpu as pltpu

F32 = jnp.float32
BF16 = jnp.bfloat16
EPS = 1e-6
GDN_CHUNK = 64
CONV_HALO_ROWS = 16
INV_BASE_BLOCK = 8
VMEM_LIMIT_BYTES = 58 * 1024 * 1024
LOG2E = 1.4426950408889634
SUBLANES = 8
HIGHEST = lax.Precision.HIGHEST
NT_DIMS = (((1,), (1,)), ((), ()))
TN_DIMS = (((0,), (0,)), ((), ()))


def _pick(n, pref, align):
    best = None
    for d in range(align, min(n, pref) + 1, align):
        if n % d == 0:
            best = d
    return n if best is None else best


def _params(*sem):
    return pltpu.CompilerParams(dimension_semantics=sem, vmem_limit_bytes=VMEM_LIMIT_BYTES)


def _sigmoid(x):
    return 1.0 / (1.0 + jnp.exp(-x))


def _softplus(x):
    return jnp.maximum(x, 0.0) + jnp.log1p(jnp.exp(-jnp.abs(x)))


def _rms_rows(x, w):
    ms = jnp.mean(x * x, axis=-1, keepdims=True)
    return x * lax.rsqrt(ms + EPS) * w


def _wdot(xn, w_ref, w_t):
    w = w_ref[...].astype(BF16)
    if w_t:
        return lax.dot_general(xn, w, NT_DIMS, preferred_element_type=F32)
    return jnp.dot(xn, w, preferred_element_type=F32)


def _norm_proj_kernel(x_ref, nw_ref, w_ref, o_ref, xn_ref, *, relu2, w_t):
    @pl.when(pl.program_id(1) == 0)
    def _():
        tm = x_ref.shape[0]
        rows = _pick(tm, 272, 16)

        def body(r, carry):
            sl = pl.ds(pl.multiple_of(r * rows, rows), rows)
            xn_ref[sl, :] = _rms_rows(x_ref[sl, :], nw_ref[...]).astype(BF16)
            return carry

        lax.fori_loop(0, tm // rows, body, 0)

    acc = _wdot(xn_ref[...], w_ref, w_t)
    if relu2:
        acc = jnp.square(jnp.maximum(acc, 0.0))
    o_ref[...] = acc.astype(o_ref.dtype)


def norm_proj(x, nw, w_all, layer, n_cols, *, relu2=False, w_t=False, name="norm_proj"):
    M, K = x.shape
    tm = _pick(M, 1088, 16)
    tn = _pick(n_cols, 1024, 128)
    if w_t:
        w_spec = pl.BlockSpec((None, tn, K), lambda i, j: (layer, j, 0))
    else:
        w_spec = pl.BlockSpec((None, K, tn), lambda i, j: (layer, 0, j))
    return pl.pallas_call(
        functools.partial(_norm_proj_kernel, relu2=relu2, w_t=w_t),
        grid=(M // tm, n_cols // tn),
        in_specs=[pl.BlockSpec((tm, K), lambda i, j: (i, 0)),
                  pl.BlockSpec((1, K), lambda i, j: (0, 0)),
                  w_spec],
        out_specs=pl.BlockSpec((tm, tn), lambda i, j: (i, j)),
        out_shape=jax.ShapeDtypeStruct((M, n_cols), BF16),
        scratch_shapes=[pltpu.VMEM((tm, K), BF16)],
        compiler_params=_params("parallel", "arbitrary"),
        name=name,
    )(x, nw.reshape(1, K), w_all)


LANES = 128


def _gdn_gate_kernel(x_ref, nw_ref, w_ref, alog_ref, dtb_ref, beta_ref, g_ref, *, hv):
    xn = _rms_rows(x_ref[...], nw_ref[...]).astype(BF16)
    acc = _wdot(xn, w_ref, True)
    beta_ref[...] = _sigmoid(acc[:, :hv])
    g_ref[...] = -jnp.exp(alog_ref[...]) * _softplus(acc[:, hv:2 * hv] + dtb_ref[...])


def gdn_gates(x, nw, wt_all, layer, col0, a_log, dt_bias):
    M, K = x.shape
    H = a_log.shape[0]
    assert col0 % LANES == 0 and 2 * H <= LANES
    tm = _pick(M, 544, 8)
    row = lambda i: (i, 0)
    fix = lambda i: (0, 0)
    return pl.pallas_call(
        functools.partial(_gdn_gate_kernel, hv=H),
        grid=(M // tm,),
        in_specs=[pl.BlockSpec((tm, K), row), pl.BlockSpec((1, K), fix),
                  pl.BlockSpec((None, LANES, K), lambda i: (layer, col0 // LANES, 0)),
                  pl.BlockSpec((1, H), fix), pl.BlockSpec((1, H), fix)],
        out_specs=[pl.BlockSpec((tm, H), row), pl.BlockSpec((tm, H), row)],
        out_shape=[jax.ShapeDtypeStruct((M, H), F32)] * 2,
        compiler_params=_params("parallel"),
        name="gdn_gates",
    )(x, nw.reshape(1, K), wt_all, a_log.reshape(1, H), dt_bias.reshape(1, H))


def _fox_logf_kernel(x_ref, nw_ref, w_ref, bf_ref, lf_ref, *, H):
    xn = _rms_rows(x_ref[...], nw_ref[...]).astype(BF16)
    acc = _wdot(xn, w_ref, True)
    f = acc[:, :H] + bf_ref[...]
    lf_ref[...] = -_softplus(-f)


def fox_logf(x, nw, wt_all, layer, col0, b_f):
    M, K = x.shape
    H = b_f.shape[0]
    assert col0 % LANES == 0 and H <= LANES
    tm = _pick(M, 544, 8)
    row = lambda i: (i, 0)
    fix = lambda i: (0, 0)
    return pl.pallas_call(
        functools.partial(_fox_logf_kernel, H=H),
        grid=(M // tm,),
        in_specs=[pl.BlockSpec((tm, K), row), pl.BlockSpec((1, K), fix),
                  pl.BlockSpec((None, LANES, K), lambda i: (layer, col0 // LANES, 0)),
                  pl.BlockSpec((1, H), fix)],
        out_specs=pl.BlockSpec((tm, H), row),
        out_shape=jax.ShapeDtypeStruct((M, H), F32),
        compiler_params=_params("parallel"),
        name="fox_logf",
    )(x, nw.reshape(1, K), wt_all, b_f.reshape(1, H))


def _matmul_res_kernel(a_ref, w_ref, r_ref, o_ref, acc_ref, *, nk):
    part = jnp.dot(a_ref[...], w_ref[...].astype(BF16), preferred_element_type=F32)
    if nk == 1:
        o_ref[...] = r_ref[...] + part
        return
    k = pl.program_id(2)

    @pl.when(k == 0)
    def _():
        acc_ref[...] = part

    @pl.when(k > 0)
    def _():
        acc_ref[...] += part

    @pl.when(k == nk - 1)
    def _():
        o_ref[...] = r_ref[...] + acc_ref[...]


def matmul_res(a, w_all, layer, res, *, name="matmul_res"):
    M, K = a.shape
    N = w_all.shape[-1]
    tm = _pick(M, 1088, 16)
    tn = _pick(N, 512, 128)
    tk = _pick(K, 4096, 128)
    nk = K // tk
    return pl.pallas_call(
        functools.partial(_matmul_res_kernel, nk=nk),
        grid=(M // tm, N // tn, nk),
        in_specs=[pl.BlockSpec((tm, tk), lambda i, j, k: (i, k)),
                  pl.BlockSpec((None, tk, tn), lambda i, j, k: (layer, k, j)),
                  pl.BlockSpec((tm, tn), lambda i, j, k: (i, j))],
        out_specs=pl.BlockSpec((tm, tn), lambda i, j, k: (i, j)),
        out_shape=jax.ShapeDtypeStruct((M, N), F32),
        scratch_shapes=[pltpu.VMEM((tm, tn), F32)],
        compiler_params=_params("parallel", "parallel", "arbitrary"),
        name=name,
    )(a, w_all, res)


def _rmsnorm_kernel(x_ref, nw_ref, o_ref):
    o_ref[...] = _rms_rows(x_ref[...], nw_ref[...])


def rmsnorm_rows(x, nw, row0, n_rows):
    K = x.shape[1]
    tm = _pick(n_rows, 512, 8)
    rb0 = row0 // tm
    return pl.pallas_call(
        _rmsnorm_kernel,
        grid=(n_rows // tm,),
        in_specs=[pl.BlockSpec((tm, K), lambda i: (rb0 + i, 0)), pl.BlockSpec((1, K), lambda i: (0, 0))],
        out_specs=pl.BlockSpec((tm, K), lambda i: (i, 0)),
        out_shape=jax.ShapeDtypeStruct((n_rows, K), F32),
        compiler_params=_params("parallel"),
        name="final_rmsnorm",
    )(x, nw.reshape(1, K))


def _gdn_chunk_kernel(x_ref, xp_ref, cw_ref, z_ref, g_ref, b_ref, nw_ref, *rest,
                      C, hq, rep, dk, dv, n_chunks, has_s0, has_prev):
    rest = list(rest)
    buf_ref = rest.pop(0) if has_s0 else None
    s0_ref = rest.pop(0) if has_s0 else None
    if has_prev:
        rest.pop(0)
    o_ref, sout_ref, s_ref = rest
    n = pl.program_id(2)
    C2 = 2 * C
    cw = cw_ref.shape[0]
    qk_dim = hq * dk
    pair = 2 * dk
    assert dv == dk and cw - 1 <= CONV_HALO_ROWS <= C

    @pl.when(n == 0)
    def _():
        if has_s0:
            s_ref[...] = s0_ref[0]
        else:
            s_ref[...] = jnp.zeros_like(s_ref)

    K = CONV_HALO_ROWS + C
    sr = lax.broadcasted_iota(jnp.int32, ((cw - 1) * C, K), 0)
    sc = lax.broadcasted_iota(jnp.int32, ((cw - 1) * C, K), 1)
    cshift = C.bit_length() - 1
    sel = jnp.where(sc == CONV_HALO_ROWS + (sr & (C - 1)) - ((sr >> cshift) + 1), 1.0, 0.0).astype(BF16)
    first = n == 0
    trow = lax.broadcasted_iota(jnp.int32, (C, pair), 0)
    conv_cache = {}

    def conv_pair(p):
        if p in conv_cache:
            return conv_cache[p]
        cs = slice(p * pair, (p + 1) * pair)
        x_cur = x_ref[:, cs]
        x_prev = xp_ref[C - CONV_HALO_ROWS:C, cs]
        x_prev = jnp.where(first, jnp.zeros_like(x_prev), x_prev)
        sh = jnp.dot(sel, jnp.concatenate([x_prev, x_cur], axis=0), preferred_element_type=F32)
        acc = x_cur.astype(F32) * cw_ref[cw - 1:cw, cs]
        for s in range(1, cw):
            acc = acc + sh[(s - 1) * C:s * C] * cw_ref[cw - 1 - s:cw - s, cs]
        if has_s0:
            corr = jnp.zeros((C, pair), F32)
            for t in range(cw - 1):
                r = sum(cw_ref[cw - 1 - s:cw - s, cs] * buf_ref[0, cw - 1 - (s - t):cw - (s - t), cs]
                        for s in range(t + 1, cw))
                corr = jnp.where(trow == t, r, corr)
            acc = acc + jnp.where(first, corr, 0.0)
        y = acc * _sigmoid(acc)
        conv_cache[p] = y
        return y

    def conv(c0, l2_scale=None):
        y = conv_pair(c0 // pair)[:, (c0 % pair):(c0 % pair) + dk]
        if l2_scale is None:
            return y
        return y * (lax.rsqrt(jnp.sum(y * y, axis=-1, keepdims=True) + EPS) * l2_scale)

    row = lax.broadcasted_iota(jnp.int32, (C, C2), 0)
    lane = lax.broadcasted_iota(jnp.int32, (C, C2), 1)
    right = lane >= C
    col = jnp.where(right, lane - C, lane)
    incl = row >= col
    strict = row > col
    diag = lane == row
    eye_right = jnp.where(lane == row + C, 1.0, 0.0).astype(F32)
    tr = lax.broadcasted_iota(jnp.int32, (C, C2), 0)
    triu2 = jnp.where(tr <= col, 1.0, 0.0).astype(F32)
    zeros_z = jnp.zeros((C, C2), BF16)
    zeros_r = jnp.zeros((C, dk + dv), BF16)

    def to_col(r2):
        return jnp.sum(jnp.where(diag, r2, 0.0), axis=1, keepdims=True)

    diag_c = (lax.broadcasted_iota(jnp.int32, (C, C), 0) == lax.broadcasted_iota(jnp.int32, (C, C), 1))

    def to_col_c(r):
        return jnp.sum(jnp.where(diag_c, r, 0.0), axis=1, keepdims=True)

    heads = range(hq * rep)
    q = [conv(a * dk, dk ** -0.5) for a in range(hq)]
    k = [conv(qk_dim + a * dk, 1.0) for a in range(hq)]
    q16 = [x.astype(BF16) for x in q]
    kk16 = [jnp.concatenate([x.astype(BF16)] * 2, axis=0) for x in k]
    G2, G_col, b_col, dec_incl, gamma, G_last, kb, vb = [], [], [], [], [], [], [], []
    for i in heads:
        g_row = g_ref[0, i, pl.ds(n, 1), :]
        b_row = b_ref[0, i, pl.ds(n, 1), :]
        G2.append(jnp.dot(g_row, triu2, precision=HIGHEST, preferred_element_type=F32))
        G_col.append(to_col(G2[i]))
        b_col.append(to_col_c(b_row))
        dec_incl.append(jnp.exp(jnp.where(incl, G_col[i] - G2[i], -jnp.inf)))
        gamma.append(jnp.exp(G_col[i]))
        G_last.append(G2[i][:, C - 1:C])
        kb.append(k[i // rep] * b_col[i])
        vb.append(conv(2 * qk_dim + i * dv) * b_col[i])
    gram = [lax.dot_general(jnp.concatenate([kb[i].astype(BF16), q16[i // rep]], axis=0), kk16[i // rep],
                            NT_DIMS, preferred_element_type=F32) for i in heads]
    Lf = [gram[i][:C] * dec_incl[i] for i in heads]
    base = min(INV_BASE_BLOCK, C)
    sh = base.bit_length() - 1
    in_base = jnp.logical_and(strict, (row >> sh) == (col >> sh))
    Z = [jnp.where(right, eye_right, -jnp.where(in_base, Lf[i], 0.0)) for i in heads]
    for _ in range((base - 1).bit_length()):
        Z16 = [z.astype(BF16) for z in Z]
        Z = [jnp.dot(Z16[i], jnp.concatenate([Z16[i], zeros_z], axis=0), preferred_element_type=F32)
             + jnp.where(right, Z[i], 0.0) for i in heads]
    b = base
    while b < C:
        sh = b.bit_length() - 1
        lower_left = jnp.logical_and(jnp.logical_and(right, (row >> (sh + 1)) == (col >> (sh + 1))),
                                     jnp.logical_and(((row >> sh) & 1) == 1, ((col >> sh) & 1) == 0))
        Z16 = [z.astype(BF16) for z in Z]
        W16 = [jnp.dot(jnp.where(lower_left, Lf[i], 0.0).astype(BF16),
                       jnp.concatenate([zeros_z, Z16[i]], axis=0),
                       preferred_element_type=F32).astype(BF16) for i in heads]
        Z = [Z[i] - jnp.dot(Z16[i], jnp.concatenate([zeros_z, W16[i]], axis=0), preferred_element_type=F32)
             for i in heads]
        b *= 2
    wu = []
    for i in heads:
        rhs = jnp.concatenate([(kb[i] * gamma[i]).astype(BF16), vb[i].astype(BF16)], axis=1)
        wu.append(jnp.dot(Z[i].astype(BF16), jnp.concatenate([zeros_r, rhs], axis=0),
                          preferred_element_type=F32))
    S = [s_ref[i] for i in heads]
    S16 = [x.astype(BF16) for x in S]
    ws_qs = [jnp.dot(jnp.concatenate([wu[i][:, :dk].astype(BF16), (q[i // rep] * gamma[i]).astype(BF16)], axis=0),
                     S16[i], preferred_element_type=F32) for i in heads]
    U16 = [(wu[i][:, dk:] - ws_qs[i][:C]).astype(BF16) for i in heads]
    A16 = [(gram[i][C:] * dec_incl[i])[:, :C].astype(BF16) for i in heads]
    o = [ws_qs[i][C:] + jnp.dot(A16[i], U16[i], preferred_element_type=F32) for i in heads]
    for i in heads:
        kd = k[i // rep] * jnp.exp(G_last[i] - G_col[i])
        s_ref[i] = jnp.exp(G_last[i]) * S[i] + lax.dot_general(kd.astype(BF16), U16[i], TN_DIMS,
                                                               preferred_element_type=F32)
    for i in heads:
        z = z_ref[:, i * dv:(i + 1) * dv].astype(F32)
        o_ref[:, i * dv:(i + 1) * dv] = (_rms_rows(o[i], nw_ref[...]) * (z * _sigmoid(z))).astype(o_ref.dtype)

    @pl.when(n == n_chunks - 1)
    def _():
        sout_ref[0] = s_ref[...]


def gdn_chunks(proj, conv_w_all, buf_all, g_rows, b_rows, norm_w, s0_all, layer, n_layers, s_prev, *, row0,
               nseq, T, C, hv, hqk, dk, dv, out_rows):
    N = T // C
    cw = conv_w_all.shape[1]
    conv_dim = 2 * hqk * dk + hv * dv
    assert conv_dim % (hv * dv) == 0
    has_s0 = s0_all is not None
    rb0 = row0 // C
    in_specs = [pl.BlockSpec((C, conv_dim), lambda b, h, n: (rb0 + b * N + n, 0)),
                pl.BlockSpec((C, conv_dim), lambda b, h, n: (rb0 + b * N + jnp.maximum(n - 1, 0), 0)),
                pl.BlockSpec((None, cw, conv_dim), lambda b, h, n: (layer, 0, 0)),
                pl.BlockSpec((C, hv * dv), lambda b, h, n: (rb0 + b * N + n, conv_dim // (hv * dv))),
                pl.BlockSpec((1, hv, N, C), lambda b, h, n: (b, 0, 0, 0)),
                pl.BlockSpec((1, hv, N, C), lambda b, h, n: (b, 0, 0, 0)),
                pl.BlockSpec((1, dv), lambda b, h, n: (0, 0))]
    args = [proj, proj, conv_w_all, proj, g_rows, b_rows, norm_w.reshape(1, dv)]
    if has_s0:
        in_specs += [pl.BlockSpec((None, 1, cw - 1, conv_dim), lambda b, h, n: (layer, b, 0, 0)),
                     pl.BlockSpec((None, 1, hv, dk, dv), lambda b, h, n: (layer, b, 0, 0, 0))]
        args += [buf_all, s0_all]
    aliases = {}
    if s_prev is not None:
        aliases[len(args)] = 1
        in_specs.append(pl.BlockSpec(memory_space=pl.ANY))
        args.append(s_prev)
    return pl.pallas_call(
        functools.partial(_gdn_chunk_kernel, C=C, hq=hqk, rep=hv // hqk, dk=dk, dv=dv, n_chunks=N,
                          has_s0=has_s0, has_prev=s_prev is not None),
        grid=(nseq, 1, N),
        in_specs=in_specs,
        out_specs=[pl.BlockSpec((C, hv * dv), lambda b, h, n: (b * N + n, 0)),
                   pl.BlockSpec((None, 1, hv, dk, dv), lambda b, h, n: (layer, b, 0, 0, 0))],
        out_shape=[jax.ShapeDtypeStruct((out_rows, hv * dv), BF16),
                   jax.ShapeDtypeStruct((n_layers, nseq, hv, dk, dv), F32)],
        input_output_aliases=aliases,
        scratch_shapes=[pltpu.VMEM((hv, dk, dv), F32)],
        compiler_params=_params("parallel", "parallel", "arbitrary"),
        name="gdn_chunks",
    )(*args)


def _cumsum_kernel(x_ref, o_ref, carry_ref, *, tb):
    @pl.when(pl.program_id(0) == 0)
    def _():
        carry_ref[...] = jnp.zeros_like(carry_ref)

    row = lax.broadcasted_iota(jnp.int32, (tb, tb), 0)
    col = lax.broadcasted_iota(jnp.int32, (tb, tb), 1)
    tril = jnp.where(row >= col, 1.0, 0.0).astype(F32)
    c = jnp.dot(tril, x_ref[...], precision=HIGHEST, preferred_element_type=F32) + carry_ref[...]
    o_ref[...] = c
    carry_ref[...] = c[tb - 1:tb, :]


def cumsum_time(x):
    nseq, L, H = x.shape
    S = nseq * H
    tb = _pick(L, 256, 8)
    out = pl.pallas_call(
        functools.partial(_cumsum_kernel, tb=tb),
        grid=(L // tb,),
        in_specs=[pl.BlockSpec((tb, S), lambda t: (t, 0))],
        out_specs=pl.BlockSpec((tb, S), lambda t: (t, 0)),
        out_shape=jax.ShapeDtypeStruct((L, S), F32),
        scratch_shapes=[pltpu.VMEM((1, S), F32)],
        compiler_params=_params("arbitrary"),
        name="fox_cumsum",
    )(x.transpose(1, 0, 2).reshape(L, S))
    return out.reshape(L, nseq, H)


def _fox_prompt_kernel(q_ref, k_ref, v_ref, og_ref, cq_ref, ckt_ref, o_ref, acc_ref, m_ref, cqr_ref, q2_ref,
                       *, bq, H, dh, scale, hg):
    i = pl.program_id(1)
    j = pl.program_id(2)
    rep = bq // dh

    @pl.when(j == 0)
    def _():
        m_ref[...] = jnp.full(m_ref.shape, -jnp.inf, F32)
        acc_ref[...] = jnp.zeros_like(acc_ref)
        cqs = cq_ref[...] * LOG2E
        for h in range(H):
            cqr_ref[h] = jnp.broadcast_to(cqs[:, h:h + 1], (bq, dh))
        q2_ref[...] = (q_ref[...].astype(F32) * (scale * LOG2E)).astype(BF16)

    def block(masked):
        if masked:
            mask = lax.broadcasted_iota(jnp.int32, (bq, bq), 1) <= lax.broadcasted_iota(jnp.int32, (bq, bq), 0)
        ones16 = jnp.ones((bq, dh), BF16)
        ck2 = ckt_ref[...] * LOG2E

        def qk(h):
            hs = slice(h * dh, (h + 1) * dh)
            return lax.dot_general(q2_ref[:, hs], k_ref[:, hs], NT_DIMS, preferred_element_type=F32)

        groups = [range(g * hg, (g + 1) * hg) for g in range(H // hg)]
        s = {h: qk(h) for h in groups[0]}
        for gi, grp in enumerate(groups):
            if gi + 1 < len(groups):
                for h in groups[gi + 1]:
                    s[h] = qk(h)
            p16, alpha2 = {}, {}
            for h in grp:
                t = s.pop(h) - ck2[h:h + 1, :]
                if masked:
                    t = jnp.where(mask, t, -jnp.inf)
                cq2 = cqr_ref[h]
                m_prev = m_ref[h]
                m_new = jnp.maximum(m_prev, jnp.max(t, axis=1, keepdims=True) + cq2)
                m_ref[h] = m_new
                alpha = jnp.exp2(m_prev - m_new)
                c = cq2 - m_new
                p16[h] = jnp.exp2(t + jnp.concatenate([c] * rep, axis=1)).astype(BF16)
                alpha2[h] = jnp.concatenate([alpha, alpha], axis=1)
            for h in grp:
                hs = slice(h * dh, (h + 1) * dh)
                v_ext = jnp.concatenate([v_ref[:, hs], ones16], axis=1)
                acc_ref[h] = alpha2[h] * acc_ref[h] + jnp.dot(p16[h], v_ext, preferred_element_type=F32)

    @pl.when(j < i)
    def _():
        block(False)

    @pl.when(j == i)
    def _():
        block(True)
        for h in range(H):
            hs = slice(h * dh, (h + 1) * dh)
            a = acc_ref[h]
            o_ref[:, hs] = ((a[:, :dh] / a[:, dh:]) * _sigmoid(og_ref[:, hs].astype(F32))).astype(o_ref.dtype)


def fox_prompt_attention(proj, cum, cum_t, *, nseq, T, H, dh, out_rows):
    bq = _pick(T, 256, 128)
    nq = T // bq
    D = H * dh
    hg = _pick(H, 2, 1)
    kv = lambda col: (lambda b, i, j: (b * nq + jnp.minimum(j, i), col))
    return pl.pallas_call(
        functools.partial(_fox_prompt_kernel, bq=bq, H=H, dh=dh, scale=dh ** -0.5, hg=hg),
        grid=(nseq, nq, nq),
        in_specs=[pl.BlockSpec((bq, D), lambda b, i, j: (b * nq + i, 0)),
                  pl.BlockSpec((bq, D), kv(1)),
                  pl.BlockSpec((bq, D), kv(2)),
                  pl.BlockSpec((bq, D), lambda b, i, j: (b * nq + i, 3)),
                  pl.BlockSpec((bq, H), lambda b, i, j: (b * nq + i, 0)),
                  pl.BlockSpec((H, bq), lambda b, i, j: (0, b * nq + jnp.minimum(j, i)))],
        out_specs=pl.BlockSpec((bq, D), lambda b, i, j: (b * nq + i, 0)),
        out_shape=jax.ShapeDtypeStruct((out_rows, D), BF16),
        scratch_shapes=[pltpu.VMEM((H, bq, 2 * dh), F32), pltpu.VMEM((H, bq, dh), F32),
                        pltpu.VMEM((H, bq, dh), F32), pltpu.VMEM((bq, D), BF16)],
        compiler_params=_params("parallel", "parallel", "arbitrary"),
        name="fox_prompt_attention",
    )(proj, proj, proj, proj, cum, cum_t)


def _fox_sample_kernel(q_ref, kn_ref, vn_ref, og_ref, kc_ref, vc_ref, cq_ref, ck_ref, o_ref,
                       *, P, T, G, dh, scale):
    R, PG = T * G, P * G
    gs = G.bit_length() - 1
    q = (q_ref[0, 0].astype(F32) * scale).astype(BF16)
    kc = kc_ref[0].reshape(PG, dh).astype(BF16)
    vc = vc_ref[0].reshape(PG, dh).astype(BF16)
    s_p = lax.dot_general(q, kc, NT_DIMS, preferred_element_type=F32)
    s_n = lax.dot_general(q, kn_ref[0, 0], NT_DIMS, preferred_element_type=F32)
    cq = cq_ref[0, 0]
    ck = ck_ref[0, 0]
    same_p = ((lax.broadcasted_iota(jnp.int32, (R, PG), 0) & (G - 1))
              == (lax.broadcasted_iota(jnp.int32, (R, PG), 1) & (G - 1)))
    rn = lax.broadcasted_iota(jnp.int32, (R, R), 0)
    cn = lax.broadcasted_iota(jnp.int32, (R, R), 1)
    same_n = jnp.logical_and((rn & (G - 1)) == (cn & (G - 1)), (cn >> gs) <= (rn >> gs))
    s_p = jnp.where(same_p, s_p + (cq - ck[:, :PG]), -jnp.inf)
    s_n = jnp.where(same_n, s_n + (cq - ck[:, PG:]), -jnp.inf)
    m = jnp.maximum(jnp.max(s_p, axis=1, keepdims=True), jnp.max(s_n, axis=1, keepdims=True))
    p_p = jnp.exp(s_p - m)
    p_n = jnp.exp(s_n - m)
    l = jnp.sum(p_p, axis=1, keepdims=True) + jnp.sum(p_n, axis=1, keepdims=True)
    o = (jnp.dot(p_p.astype(BF16), vc, preferred_element_type=F32)
         + jnp.dot(p_n.astype(BF16), vn_ref[0, 0], preferred_element_type=F32))
    o_ref[0, 0] = ((o / l) * _sigmoid(og_ref[0, 0].astype(F32))).astype(o_ref.dtype)


def fox_sample_attention(proj, k_cache_all, v_cache_all, layer, cum, *, row0, nseq, T, P, H, dh):
    G = SUBLANES if H % SUBLANES == 0 else H
    assert G & (G - 1) == 0
    NG = H // G
    R = T * G
    new = proj[row0:row0 + nseq * T].reshape(nseq, T, 4, NG, G, dh).transpose(2, 0, 3, 1, 4, 5)
    new = new.reshape(4, nseq, NG, R, dh)
    ck = cum.reshape(nseq, P + T, NG, G).transpose(0, 2, 1, 3).reshape(nseq, NG, 1, (P + T) * G)
    cq = cum[:, P:].reshape(nseq, T, NG, G).transpose(0, 2, 1, 3).reshape(nseq, NG, R, 1)
    part = lambda a: pl.BlockSpec((None, 1, 1, R, dh), lambda b, g: (a, b, g, 0, 0))
    cache = pl.BlockSpec((None, 1, P, G, dh), lambda b, g: (layer, b, 0, g, 0))
    out = pl.pallas_call(
        functools.partial(_fox_sample_kernel, P=P, T=T, G=G, dh=dh, scale=dh ** -0.5),
        grid=(nseq, NG),
        in_specs=[part(0), part(1), part(2), part(3), cache, cache,
                  pl.BlockSpec((1, 1, R, 1), lambda b, g: (b, g, 0, 0)),
                  pl.BlockSpec((1, 1, 1, (P + T) * G), lambda b, g: (b, g, 0, 0))],
        out_specs=pl.BlockSpec((1, 1, R, dh), lambda b, g: (b, g, 0, 0)),
        out_shape=jax.ShapeDtypeStruct((nseq, NG, R, dh), BF16),
        compiler_params=_params("parallel", "parallel"),
        name="fox_sample_attention",
    )(new, new, new, new, k_cache_all, v_cache_all, cq, ck)
    return out.reshape(nseq, NG, T, G, dh).transpose(0, 2, 1, 3, 4).reshape(nseq * T, H * dh)


def kernel(x_prompt, x_sample, state_gdn, state_gdn_conv, cache_fox_k, cache_fox_v, cache_fox_logf,
           norm_mix, norm_mlp, norm_final, gdn_w_in, gdn_conv_w, gdn_a_log, gdn_dt_bias, gdn_norm_w,
           gdn_w_out, fox_w_in, fox_b_f, fox_w_out, mlp_w_up, mlp_w_down):
    Bp, Tp, D = x_prompt.shape
    Bs, Ts, _ = x_sample.shape
    depth = norm_mix.shape[0]
    _, _, hv, dk, dv = state_gdn.shape
    conv_dim = state_gdn_conv.shape[-1]
    cw1 = state_gdn_conv.shape[2]
    v_dim = hv * dv
    qk_dim = (conv_dim - v_dim) // 2
    hqk = qk_dim // dk
    _, _, P, H, dh = cache_fox_k.shape
    fox_dim = H * dh
    d_ff = mlp_w_up.shape[-1]
    Mp, Ms = Bp * Tp, Bs * Ts
    Cp = min(GDN_CHUNK, Tp)

    x = jnp.concatenate([x_prompt.reshape(Mp, D), x_sample.reshape(Ms, D)], axis=0)
    gdn_w_in_t = jnp.swapaxes(gdn_w_in, 1, 2)
    fox_w_in_t = jnp.swapaxes(fox_w_in, 1, 2)
    S_p, S_s, buf_p, buf_s = None, None, [], []
    n_gdn = state_gdn.shape[0]
    k_p, k_s, v_p, v_s, lf_p, lf_s = [], [], [], [], [], []

    for i in range(depth):
        j = i // 2
        if i % 2 == 0:
            n_main = conv_dim + v_dim
            proj = norm_proj(x, norm_mix[i], gdn_w_in_t, j, n_main, w_t=True, name="gdn_in_proj")
            beta, g = gdn_gates(x, norm_mix[i], gdn_w_in_t, j, n_main, gdn_a_log[j], gdn_dt_bias[j])

            def rows(a, B, T, C):
                return a.reshape(B, T // C, C, hv).transpose(0, 3, 1, 2)

            common = dict(hv=hv, hqk=hqk, dk=dk, dv=dv)
            o, S_p = gdn_chunks(proj, gdn_conv_w, None, rows(g[:Mp], Bp, Tp, Cp), rows(beta[:Mp], Bp, Tp, Cp),
                                gdn_norm_w[j], None, j, n_gdn, S_p, row0=0, nseq=Bp, T=Tp, C=Cp,
                                out_rows=Mp + Ms, **common)
            o_s, S_s = gdn_chunks(proj, gdn_conv_w, state_gdn_conv, rows(g[Mp:], Bs, Ts, Ts),
                                  rows(beta[Mp:], Bs, Ts, Ts), gdn_norm_w[j], state_gdn, j, n_gdn, S_s,
                                  row0=Mp, nseq=Bs, T=Ts, C=Ts, out_rows=Ms, **common)
            o = lax.dynamic_update_slice(o, o_s, (Mp, 0))
            x = matmul_res(o, gdn_w_out, j, x, name="gdn_out_proj")
            buf_p.append(jnp.stack([proj[(b + 1) * Tp - cw1:(b + 1) * Tp, :conv_dim]
                                    for b in range(Bp)]).astype(F32))
            qkv_s = proj[Mp:, :conv_dim].reshape(Bs, Ts, conv_dim).astype(F32)
            buf_s.append(jnp.concatenate([state_gdn_conv[j], qkv_s], axis=1)[:, -cw1:])
        else:
            proj = norm_proj(x, norm_mix[i], fox_w_in_t, j, 4 * fox_dim, w_t=True, name="fox_in_proj")
            logf = fox_logf(x, norm_mix[i], fox_w_in_t, j, 4 * fox_dim, fox_b_f[j])
            cum_p = cumsum_time(logf[:Mp].reshape(Bp, Tp, H))
            o = fox_prompt_attention(proj, cum_p.transpose(1, 0, 2).reshape(Mp, H),
                                     cum_p.transpose(2, 1, 0).reshape(H, Mp), nseq=Bp, T=Tp, H=H, dh=dh,
                                     out_rows=Mp + Ms)
            lf_new = logf[Mp:].reshape(Bs, Ts, H)
            lf_all = jnp.concatenate([cache_fox_logf[j], lf_new], axis=1)
            cum_s = cumsum_time(lf_all).transpose(1, 0, 2)
            o_s = fox_sample_attention(proj, cache_fox_k, cache_fox_v, j, cum_s,
                                       row0=Mp, nseq=Bs, T=Ts, P=P, H=H, dh=dh)
            o = lax.dynamic_update_slice(o, o_s, (Mp, 0))
            x = matmul_res(o, fox_w_out, j, x, name="fox_out_proj")
            kv = proj[:, fox_dim:3 * fox_dim].reshape(Mp + Ms, 2, H, dh)
            k_p.append(kv[:Mp, 0].astype(F32).reshape(Bp, Tp, H, dh))
            k_s.append(kv[Mp:, 0].astype(F32).reshape(Bs, Ts, H, dh))
            v_p.append(kv[:Mp, 1].astype(F32).reshape(Bp, Tp, H, dh))
            v_s.append(kv[Mp:, 1].astype(F32).reshape(Bs, Ts, H, dh))
            lf_p.append(logf[:Mp].reshape(Bp, Tp, H))
            lf_s.append(lf_new)
        hmid = norm_proj(x, norm_mlp[i], mlp_w_up, i, d_ff, relu2=True, name="mlp_up")
        x = matmul_res(hmid, mlp_w_down, i, x, name="mlp_down")

    y_p = rmsnorm_rows(x, norm_final, 0, Mp)
    y_s = rmsnorm_rows(x, norm_final, Mp, Ms)
    st = jnp.stack
    return (y_p.reshape(Bp, Tp, D), y_s.reshape(Bs, Ts, D),
            S_p, st(buf_p), st(k_p), st(v_p), st(lf_p),
            S_s, st(buf_s), st(k_s), st(v_s), st(lf_s))
```

```python
import functools

import jax
import jax.numpy as jnp
from jax import lax
from jax.experimental import pallas as pl
from jax.experimental.pallas import tpu as pltpu

F32 = jnp.float32
BF16 = jnp.bfloat16
EPS = 1e-6
GDN_CHUNK = 64
FOX_KEY_BLOCKS = 1
GDN_HEAD_GROUP = 32
CONV_HALO_ROWS = 16
INV_BASE_BLOCK = 8
VMEM_LIMIT_BYTES = 58 * 1024 * 1024
LOG2E = 1.4426950408889634
SUBLANES = 8
HIGHEST = lax.Precision.HIGHEST
NT_DIMS = (((1,), (1,)), ((), ()))
TN_DIMS = (((0,), (0,)), ((), ()))


def _pick(n, pref, align):
    best = None
    for d in range(align, min(n, pref) + 1, align):
        if n % d == 0:
            best = d
    return n if best is None else best


def _params(*sem):
    return pltpu.CompilerParams(dimension_semantics=sem, vmem_limit_bytes=VMEM_LIMIT_BYTES)


def _sigmoid(x):
    return 1.0 / (1.0 + jnp.exp(-x))


def _softplus(x):
    return jnp.maximum(x, 0.0) + jnp.log1p(jnp.exp(-jnp.abs(x)))


def _rms_rows(x, w):
    ms = jnp.mean(x * x, axis=-1, keepdims=True)
    return x * lax.rsqrt(ms + EPS) * w


def _wdot(xn, w_ref, w_t):
    w = w_ref[...].astype(BF16)
    if w_t:
        return lax.dot_general(xn, w, NT_DIMS, preferred_element_type=F32)
    return jnp.dot(xn, w, preferred_element_type=F32)


LANES = 128


def _gdn_gate_outputs(acc, alog_ref, dtb_ref, beta_ref, g_ref):
    hv = beta_ref.shape[1]
    beta_ref[...] = _sigmoid(acc[:, :hv])
    g_ref[...] = -jnp.exp(alog_ref[...]) * _softplus(acc[:, hv:2 * hv] + dtb_ref[...])


def _fox_gate_outputs(acc, bf_ref, lf_ref):
    f = acc[:, :lf_ref.shape[1]] + bf_ref[...]
    lf_ref[...] = -_softplus(-f)


GATES = {"gdn": (_gdn_gate_outputs, 2, 2), "fox": (_fox_gate_outputs, 1, 1)}


def _norm_proj_kernel(x_ref, nw_ref, w_ref, *rest, relu2, w_t, gate):
    if gate is None:
        o_ref, xn_ref = rest
    else:
        epilogue, n_par, n_out = GATES[gate]
        gw_ref, par_refs = rest[0], rest[1:1 + n_par]
        o_ref, gate_refs, xn_ref = rest[1 + n_par], rest[2 + n_par:2 + n_par + n_out], rest[-1]

    @pl.when(pl.program_id(1) == 0)
    def _():
        tm = x_ref.shape[0]
        rows = _pick(tm, 272, 16)

        def body(r, carry):
            sl = pl.ds(pl.multiple_of(r * rows, rows), rows)
            xn_ref[sl, :] = _rms_rows(x_ref[sl, :], nw_ref[...]).astype(BF16)
            return carry

        lax.fori_loop(0, tm // rows, body, 0)
        if gate is not None:
            epilogue(_wdot(xn_ref[...], gw_ref, True), *par_refs, *gate_refs)

    acc = _wdot(xn_ref[...], w_ref, w_t)
    if relu2:
        acc = jnp.square(jnp.maximum(acc, 0.0))
    o_ref[...] = acc.astype(o_ref.dtype)


def norm_proj(x, nw, w_all, layer, n_cols, *, relu2=False, w_t=False, gate=None, gate_params=(),
              name="norm_proj"):
    M, K = x.shape
    tm = _pick(M, 1088, 16)
    tn = _pick(n_cols, 1024, 128)
    if w_t:
        w_spec = pl.BlockSpec((None, tn, K), lambda i, j: (layer, j, 0))
    else:
        w_spec = pl.BlockSpec((None, K, tn), lambda i, j: (layer, 0, j))
    in_specs = [pl.BlockSpec((tm, K), lambda i, j: (i, 0)), pl.BlockSpec((1, K), lambda i, j: (0, 0)), w_spec]
    args = [x, nw.reshape(1, K), w_all]
    out_specs = [pl.BlockSpec((tm, tn), lambda i, j: (i, j))]
    out_shape = [jax.ShapeDtypeStruct((M, n_cols), BF16)]
    if gate is not None:
        assert w_t and n_cols % LANES == 0
        H = gate_params[0].shape[0]
        assert GATES[gate][2] * H <= LANES
        in_specs.append(pl.BlockSpec((None, LANES, K), lambda i, j: (layer, n_cols // LANES, 0)))
        args.append(w_all)
        for p in gate_params:
            in_specs.append(pl.BlockSpec((1, H), lambda i, j: (0, 0)))
            args.append(p.reshape(1, H))
        for _ in range(GATES[gate][2]):
            out_specs.append(pl.BlockSpec((tm, H), lambda i, j: (i, 0)))
            out_shape.append(jax.ShapeDtypeStruct((M, H), F32))
    out = pl.pallas_call(
        functools.partial(_norm_proj_kernel, relu2=relu2, w_t=w_t, gate=gate),
        grid=(M // tm, n_cols // tn),
        in_specs=in_specs,
        out_specs=out_specs,
        out_shape=out_shape,
        scratch_shapes=[pltpu.VMEM((tm, K), BF16)],
        compiler_params=_params("parallel", "arbitrary"),
        name=name,
    )(*args)
    return out[0] if gate is None else out


def _matmul_res_kernel(a_ref, w_ref, r_ref, o_ref, acc_ref, *, nk):
    part = jnp.dot(a_ref[...], w_ref[...].astype(BF16), preferred_element_type=F32)
    if nk == 1:
        o_ref[...] = r_ref[...] + part
        return
    k = pl.program_id(2)

    @pl.when(k == 0)
    def _():
        acc_ref[...] = part

    @pl.when(k > 0)
    def _():
        acc_ref[...] += part

    @pl.when(k == nk - 1)
    def _():
        o_ref[...] = r_ref[...] + acc_ref[...]


def matmul_res(a, w_all, layer, res, *, name="matmul_res"):
    M, K = a.shape
    N = w_all.shape[-1]
    tm = _pick(M, 1088, 16)
    tn = _pick(N, 512, 128)
    tk = _pick(K, 4096, 128)
    nk = K // tk
    return pl.pallas_call(
        functools.partial(_matmul_res_kernel, nk=nk),
        grid=(M // tm, N // tn, nk),
        in_specs=[pl.BlockSpec((tm, tk), lambda i, j, k: (i, k)),
                  pl.BlockSpec((None, tk, tn), lambda i, j, k: (layer, k, j)),
                  pl.BlockSpec((tm, tn), lambda i, j, k: (i, j))],
        out_specs=pl.BlockSpec((tm, tn), lambda i, j, k: (i, j)),
        out_shape=jax.ShapeDtypeStruct((M, N), F32),
        scratch_shapes=[pltpu.VMEM((tm, tn), F32)],
        compiler_params=_params("parallel", "parallel", "arbitrary"),
        name=name,
    )(a, w_all, res)


def _rmsnorm_kernel(x_ref, nw_ref, o_ref):
    o_ref[...] = _rms_rows(x_ref[...], nw_ref[...])


def rmsnorm_rows(x, nw, row0, n_rows):
    K = x.shape[1]
    tm = _pick(n_rows, 512, 8)
    rb0 = row0 // tm
    return pl.pallas_call(
        _rmsnorm_kernel,
        grid=(n_rows // tm,),
        in_specs=[pl.BlockSpec((tm, K), lambda i: (rb0 + i, 0)), pl.BlockSpec((1, K), lambda i: (0, 0))],
        out_specs=pl.BlockSpec((tm, K), lambda i: (i, 0)),
        out_shape=jax.ShapeDtypeStruct((n_rows, K), F32),
        compiler_params=_params("parallel"),
        name="final_rmsnorm",
    )(x, nw.reshape(1, K))


def _gdn_chunk_kernel(x_ref, xp_ref, cw_ref, z_ref, g_ref, gt_ref, bt_ref, nw_ref, *rest,
                      C, hq, rep, dk, dv, n_chunks, has_s0, has_prev):
    rest = list(rest)
    buf_ref = rest.pop(0) if has_s0 else None
    s0_ref = rest.pop(0) if has_s0 else None
    if has_prev:
        rest.pop(0)
    o_ref, sout_ref, s_ref = rest
    n = pl.program_id(2)
    C2 = 2 * C
    cw = cw_ref.shape[0]
    qk_dim = hq * dk
    pair = 2 * dk
    assert dv == dk and cw - 1 <= CONV_HALO_ROWS <= C

    @pl.when(n == 0)
    def _():
        if has_s0:
            s_ref[...] = s0_ref[0]
        else:
            s_ref[...] = jnp.zeros_like(s_ref)

    K = CONV_HALO_ROWS + C
    sr = lax.broadcasted_iota(jnp.int32, ((cw - 1) * C, K), 0)
    sc = lax.broadcasted_iota(jnp.int32, ((cw - 1) * C, K), 1)
    cshift = C.bit_length() - 1
    sel = jnp.where(sc == CONV_HALO_ROWS + (sr & (C - 1)) - ((sr >> cshift) + 1), 1.0, 0.0).astype(BF16)
    first = n == 0
    trow = lax.broadcasted_iota(jnp.int32, (C, pair), 0)
    conv_cache = {}

    def conv_pair(p):
        if p in conv_cache:
            return conv_cache[p]
        cs = slice(p * pair, (p + 1) * pair)
        x_cur = x_ref[:, cs]
        x_prev = xp_ref[C - CONV_HALO_ROWS:C, cs]
        x_prev = jnp.where(first, jnp.zeros_like(x_prev), x_prev)
        sh = jnp.dot(sel, jnp.concatenate([x_prev, x_cur], axis=0), preferred_element_type=F32)
        acc = x_cur.astype(F32) * cw_ref[cw - 1:cw, cs]
        for s in range(1, cw):
            acc = acc + sh[(s - 1) * C:s * C] * cw_ref[cw - 1 - s:cw - s, cs]
        if has_s0:
            corr = jnp.zeros((C, pair), F32)
            for t in range(cw - 1):
                r = sum(cw_ref[cw - 1 - s:cw - s, cs] * buf_ref[0, cw - 1 - (s - t):cw - (s - t), cs]
                        for s in range(t + 1, cw))
                corr = jnp.where(trow == t, r, corr)
            acc = acc + jnp.where(first, corr, 0.0)
        y = acc * _sigmoid(acc)
        conv_cache[p] = y
        return y

    def conv(c0, l2_scale=None):
        y = conv_pair(c0 // pair)[:, (c0 % pair):(c0 % pair) + dk]
        if l2_scale is None:
            return y
        return y * (lax.rsqrt(jnp.sum(y * y, axis=-1, keepdims=True) + EPS) * l2_scale)

    row = lax.broadcasted_iota(jnp.int32, (C, C2), 0)
    lane = lax.broadcasted_iota(jnp.int32, (C, C2), 1)
    right = lane >= C
    col = jnp.where(right, lane - C, lane)
    incl = row >= col
    strict = row > col
    eye_right = jnp.where(lane == row + C, 1.0, 0.0).astype(F32)
    triu2 = jnp.where(row <= col, 1.0, 0.0).astype(F32)
    tril = jnp.where(lax.broadcasted_iota(jnp.int32, (C, C), 0) >= lax.broadcasted_iota(jnp.int32, (C, C), 1),
                     1.0, 0.0).astype(F32)
    zeros_z = jnp.zeros((C, C2), BF16)
    zeros_r = jnp.zeros((C, dk + dv), BF16)
    G2_all = jnp.dot(g_ref[0, n], triu2, precision=HIGHEST, preferred_element_type=F32)
    Gc_all = jnp.dot(tril, gt_ref[...], precision=HIGHEST, preferred_element_type=F32)
    beta_all = bt_ref[...]

    def head_group(heads):
        qa = sorted({i // rep for i in heads})
        q = {a: conv(a * dk, dk ** -0.5) for a in qa}
        k = {a: conv(qk_dim + a * dk, 1.0) for a in qa}
        q16 = {a: q[a].astype(BF16) for a in qa}
        kk16 = {a: jnp.concatenate([k[a].astype(BF16)] * 2, axis=0) for a in qa}
        G2, G_col, b_col, dec_incl, gamma, G_last, kb, vb = {}, {}, {}, {}, {}, {}, {}, {}
        for i in heads:
            G2[i] = G2_all[i:i + 1, :]
            G_col[i] = Gc_all[:, i:i + 1]
            b_col[i] = beta_all[:, i:i + 1]
            dec_incl[i] = jnp.exp(jnp.where(incl, G_col[i] - G2[i], -jnp.inf))
            gamma[i] = jnp.exp(G_col[i])
            G_last[i] = G2[i][:, C - 1:C]
            kb[i] = k[i // rep] * b_col[i]
            vb[i] = conv(2 * qk_dim + i * dv) * b_col[i]
        gram = {i: lax.dot_general(jnp.concatenate([kb[i].astype(BF16), q16[i // rep]], axis=0), kk16[i // rep],
                                   NT_DIMS, preferred_element_type=F32) for i in heads}
        Lf = {i: gram[i][:C] * dec_incl[i] for i in heads}
        base = min(INV_BASE_BLOCK, C)
        sh = base.bit_length() - 1
        in_base = jnp.logical_and(strict, (row >> sh) == (col >> sh))
        Z = {i: jnp.where(right, eye_right, -jnp.where(in_base, Lf[i], 0.0)) for i in heads}
        for _ in range((base - 1).bit_length()):
            Z16 = {i: Z[i].astype(BF16) for i in heads}
            Z = {i: jnp.dot(Z16[i], jnp.concatenate([Z16[i], zeros_z], axis=0), preferred_element_type=F32)
                 + jnp.where(right, Z[i], 0.0) for i in heads}
        b = base
        while b < C:
            sh = b.bit_length() - 1
            lower_left = jnp.logical_and(jnp.logical_and(right, (row >> (sh + 1)) == (col >> (sh + 1))),
                                         jnp.logical_and(((row >> sh) & 1) == 1, ((col >> sh) & 1) == 0))
            Z16 = {i: Z[i].astype(BF16) for i in heads}
            W16 = {i: jnp.dot(jnp.where(lower_left, Lf[i], 0.0).astype(BF16),
                              jnp.concatenate([zeros_z, Z16[i]], axis=0),
                              preferred_element_type=F32).astype(BF16) for i in heads}
            Z = {i: Z[i] - jnp.dot(Z16[i], jnp.concatenate([zeros_z, W16[i]], axis=0),
                                   preferred_element_type=F32) for i in heads}
            b *= 2
        wu = {}
        for i in heads:
            rhs = jnp.concatenate([(kb[i] * gamma[i]).astype(BF16), vb[i].astype(BF16)], axis=1)
            wu[i] = jnp.dot(Z[i].astype(BF16), jnp.concatenate([zeros_r, rhs], axis=0),
                            preferred_element_type=F32)
        S = {i: s_ref[i] for i in heads}
        S16 = {i: S[i].astype(BF16) for i in heads}
        ws_qs = {i: jnp.dot(jnp.concatenate([wu[i][:, :dk].astype(BF16),
                                             (q[i // rep] * gamma[i]).astype(BF16)], axis=0),
                            S16[i], preferred_element_type=F32) for i in heads}
        U16 = {i: (wu[i][:, dk:] - ws_qs[i][:C]).astype(BF16) for i in heads}
        A16 = {i: (gram[i][C:] * dec_incl[i])[:, :C].astype(BF16) for i in heads}
        o = {i: ws_qs[i][C:] + jnp.dot(A16[i], U16[i], preferred_element_type=F32) for i in heads}
        for i in heads:
            kd = k[i // rep] * jnp.exp(G_last[i] - G_col[i])
            s_ref[i] = jnp.exp(G_last[i]) * S[i] + lax.dot_general(kd.astype(BF16), U16[i], TN_DIMS,
                                                                   preferred_element_type=F32)
        for i in heads:
            z = z_ref[:, i * dv:(i + 1) * dv].astype(F32)
            o_ref[:, i * dv:(i + 1) * dv] = (_rms_rows(o[i], nw_ref[...])
                                             * (z * _sigmoid(z))).astype(o_ref.dtype)

    n_heads = hq * rep
    for h0 in range(0, n_heads, GDN_HEAD_GROUP):
        head_group(range(h0, min(h0 + GDN_HEAD_GROUP, n_heads)))

    @pl.when(n == n_chunks - 1)
    def _():
        sout_ref[0] = s_ref[...]


def gdn_chunks(proj, conv_w_all, buf_all, g, beta, norm_w, s0_all, layer, n_layers, s_prev, *, row0,
               nseq, T, C, hv, hqk, dk, dv, out_rows):
    N = T // C
    cw = conv_w_all.shape[1]
    conv_dim = 2 * hqk * dk + hv * dv
    assert conv_dim % (hv * dv) == 0
    has_s0 = s0_all is not None
    rb0 = row0 // C
    g_rows = g[row0:row0 + nseq * T].reshape(nseq, N, C, hv).transpose(0, 1, 3, 2)
    tok = pl.BlockSpec((C, hv), lambda b, h, n: (rb0 + b * N + n, 0))
    in_specs = [pl.BlockSpec((C, conv_dim), lambda b, h, n: (rb0 + b * N + n, 0)),
                pl.BlockSpec((C, conv_dim), lambda b, h, n: (rb0 + b * N + jnp.maximum(n - 1, 0), 0)),
                pl.BlockSpec((None, cw, conv_dim), lambda b, h, n: (layer, 0, 0)),
                pl.BlockSpec((C, hv * dv), lambda b, h, n: (rb0 + b * N + n, conv_dim // (hv * dv))),
                pl.BlockSpec((1, N, hv, C), lambda b, h, n: (b, 0, 0, 0)),
                tok, tok,
                pl.BlockSpec((1, dv), lambda b, h, n: (0, 0))]
    args = [proj, proj, conv_w_all, proj, g_rows, g, beta, norm_w.reshape(1, dv)]
    if has_s0:
        in_specs += [pl.BlockSpec((None, 1, cw - 1, conv_dim), lambda b, h, n: (layer, b, 0, 0)),
                     pl.BlockSpec((None, 1, hv, dk, dv), lambda b, h, n: (layer, b, 0, 0, 0))]
        args += [buf_all, s0_all]
    aliases = {}
    if s_prev is not None:
        aliases[len(args)] = 1
        in_specs.append(pl.BlockSpec(memory_space=pl.ANY))
        args.append(s_prev)
    return pl.pallas_call(
        functools.partial(_gdn_chunk_kernel, C=C, hq=hqk, rep=hv // hqk, dk=dk, dv=dv, n_chunks=N,
                          has_s0=has_s0, has_prev=s_prev is not None),
        grid=(nseq, 1, N),
        in_specs=in_specs,
        out_specs=[pl.BlockSpec((C, hv * dv), lambda b, h, n: (b * N + n, 0)),
                   pl.BlockSpec((None, 1, hv, dk, dv), lambda b, h, n: (layer, b, 0, 0, 0))],
        out_shape=[jax.ShapeDtypeStruct((out_rows, hv * dv), BF16),
                   jax.ShapeDtypeStruct((n_layers, nseq, hv, dk, dv), F32)],
        input_output_aliases=aliases,
        scratch_shapes=[pltpu.VMEM((hv, dk, dv), F32)],
        compiler_params=_params("parallel", "parallel", "arbitrary"),
        name="gdn_chunks",
    )(*args)


def _cumsum_kernel(x_ref, o_ref, carry_ref, *, tb):
    @pl.when(pl.program_id(0) == 0)
    def _():
        carry_ref[...] = jnp.zeros_like(carry_ref)

    row = lax.broadcasted_iota(jnp.int32, (tb, tb), 0)
    col = lax.broadcasted_iota(jnp.int32, (tb, tb), 1)
    tril = jnp.where(row >= col, 1.0, 0.0).astype(F32)
    c = jnp.dot(tril, x_ref[...], precision=HIGHEST, preferred_element_type=F32) + carry_ref[...]
    o_ref[...] = c
    carry_ref[...] = c[tb - 1:tb, :]


def cumsum_time(x):
    nseq, L, H = x.shape
    S = nseq * H
    tb = _pick(L, 256, 8)
    out = pl.pallas_call(
        functools.partial(_cumsum_kernel, tb=tb),
        grid=(L // tb,),
        in_specs=[pl.BlockSpec((tb, S), lambda t: (t, 0))],
        out_specs=pl.BlockSpec((tb, S), lambda t: (t, 0)),
        out_shape=jax.ShapeDtypeStruct((L, S), F32),
        scratch_shapes=[pltpu.VMEM((1, S), F32)],
        compiler_params=_params("arbitrary"),
        name="fox_cumsum",
    )(x.transpose(1, 0, 2).reshape(L, S))
    return out.reshape(L, nseq, H)


def _fox_prompt_kernel(q_ref, k_ref, v_ref, og_ref, cq_ref, ckt_ref, o_ref, acc_ref, m_ref, cqr_ref, q2_ref,
                       *, bq, bk, H, dh, scale, hg):
    i = pl.program_id(1)
    j = pl.program_id(2)
    rep = bk // dh
    r = bk // bq
    j_last = i // r

    @pl.when(j == 0)
    def _():
        m_ref[...] = jnp.full(m_ref.shape, -jnp.inf, F32)
        acc_ref[...] = jnp.zeros_like(acc_ref)
        cqs = cq_ref[...] * LOG2E
        for h in range(H):
            cqr_ref[h] = jnp.broadcast_to(cqs[:, h:h + 1], (bq, dh))
        q2_ref[...] = (q_ref[...].astype(F32) * (scale * LOG2E)).astype(BF16)

    def block(masked):
        if masked:
            mask = (lax.broadcasted_iota(jnp.int32, (bq, bk), 1)
                    <= lax.broadcasted_iota(jnp.int32, (bq, bk), 0) + (i - r * j_last) * bq)
        ones16 = jnp.ones((bk, dh), BF16)
        ck2 = ckt_ref[...] * LOG2E

        def qk(h):
            hs = slice(h * dh, (h + 1) * dh)
            return lax.dot_general(q2_ref[:, hs], k_ref[:, hs], NT_DIMS, preferred_element_type=F32)

        groups = [range(g * hg, (g + 1) * hg) for g in range(H // hg)]
        s = {h: qk(h) for h in groups[0]}
        for gi, grp in enumerate(groups):
            if gi + 1 < len(groups):
                for h in groups[gi + 1]:
                    s[h] = qk(h)
            p16, alpha2 = {}, {}
            for h in grp:
                t = s.pop(h) - ck2[h:h + 1, :]
                if masked:
                    t = jnp.where(mask, t, -jnp.inf)
                cq2 = cqr_ref[h]
                m_prev = m_ref[h]
                m_new = jnp.maximum(m_prev, jnp.max(t, axis=1, keepdims=True) + cq2)
                m_ref[h] = m_new
                alpha = jnp.exp2(m_prev - m_new)
                c = cq2 - m_new
                p16[h] = jnp.exp2(t + jnp.concatenate([c] * rep, axis=1)).astype(BF16)
                alpha2[h] = jnp.concatenate([alpha, alpha], axis=1)
            for h in grp:
                hs = slice(h * dh, (h + 1) * dh)
                v_ext = jnp.concatenate([v_ref[:, hs], ones16], axis=1)
                acc_ref[h] = alpha2[h] * acc_ref[h] + jnp.dot(p16[h], v_ext, preferred_element_type=F32)

    @pl.when(j < j_last)
    def _():
        block(False)

    @pl.when(j == j_last)
    def _():
        block(True)
        for h in range(H):
            hs = slice(h * dh, (h + 1) * dh)
            a = acc_ref[h]
            o_ref[:, hs] = ((a[:, :dh] / a[:, dh:]) * _sigmoid(og_ref[:, hs].astype(F32))).astype(o_ref.dtype)


def fox_prompt_attention(proj, cum, cum_t, *, nseq, T, H, dh, out_rows):
    bq = _pick(T, 256, 128)
    bk = _pick(T, FOX_KEY_BLOCKS * bq, bq)
    nq, nk, r = T // bq, T // bk, bk // bq
    D = H * dh
    hg = _pick(H, 2, 1)
    kv = lambda col: (lambda b, i, j: (b * nk + jnp.minimum(j, i // r), col))
    return pl.pallas_call(
        functools.partial(_fox_prompt_kernel, bq=bq, bk=bk, H=H, dh=dh, scale=dh ** -0.5, hg=hg),
        grid=(nseq, nq, nk),
        in_specs=[pl.BlockSpec((bq, D), lambda b, i, j: (b * nq + i, 0)),
                  pl.BlockSpec((bk, D), kv(1)),
                  pl.BlockSpec((bk, D), kv(2)),
                  pl.BlockSpec((bq, D), lambda b, i, j: (b * nq + i, 3)),
                  pl.BlockSpec((bq, H), lambda b, i, j: (b * nq + i, 0)),
                  pl.BlockSpec((H, bk), lambda b, i, j: (0, b * nk + jnp.minimum(j, i // r)))],
        out_specs=pl.BlockSpec((bq, D), lambda b, i, j: (b * nq + i, 0)),
        out_shape=jax.ShapeDtypeStruct((out_rows, D), BF16),
        scratch_shapes=[pltpu.VMEM((H, bq, 2 * dh), F32), pltpu.VMEM((H, bq, dh), F32),
                        pltpu.VMEM((H, bq, dh), F32), pltpu.VMEM((bq, D), BF16)],
        compiler_params=_params("parallel", "parallel", "arbitrary"),
        name="fox_prompt_attention",
    )(proj, proj, proj, proj, cum, cum_t)


def _fox_sample_kernel(q_ref, kn_ref, vn_ref, og_ref, kc_ref, vc_ref, cq_ref, ck_ref, o_ref,
                       *, P, T, G, dh, scale):
    R, PG = T * G, P * G
    gs = G.bit_length() - 1
    q = (q_ref[0, 0].astype(F32) * scale).astype(BF16)
    kc = kc_ref[0].reshape(PG, dh).astype(BF16)
    vc = vc_ref[0].reshape(PG, dh).astype(BF16)
    s_p = lax.dot_general(q, kc, NT_DIMS, preferred_element_type=F32)
    s_n = lax.dot_general(q, kn_ref[0, 0], NT_DIMS, preferred_element_type=F32)
    cq = cq_ref[0, 0]
    ck = ck_ref[0, 0]
    same_p = ((lax.broadcasted_iota(jnp.int32, (R, PG), 0) & (G - 1))
              == (lax.broadcasted_iota(jnp.int32, (R, PG), 1) & (G - 1)))
    rn = lax.broadcasted_iota(jnp.int32, (R, R), 0)
    cn = lax.broadcasted_iota(jnp.int32, (R, R), 1)
    same_n = jnp.logical_and((rn & (G - 1)) == (cn & (G - 1)), (cn >> gs) <= (rn >> gs))
    s_p = jnp.where(same_p, s_p + (cq - ck[:, :PG]), -jnp.inf)
    s_n = jnp.where(same_n, s_n + (cq - ck[:, PG:]), -jnp.inf)
    m = jnp.maximum(jnp.max(s_p, axis=1, keepdims=True), jnp.max(s_n, axis=1, keepdims=True))
    p_p = jnp.exp(s_p - m)
    p_n = jnp.exp(s_n - m)
    l = jnp.sum(p_p, axis=1, keepdims=True) + jnp.sum(p_n, axis=1, keepdims=True)
    o = (jnp.dot(p_p.astype(BF16), vc, preferred_element_type=F32)
         + jnp.dot(p_n.astype(BF16), vn_ref[0, 0], preferred_element_type=F32))
    o_ref[0, 0] = ((o / l) * _sigmoid(og_ref[0, 0].astype(F32))).astype(o_ref.dtype)


def fox_sample_attention(proj, k_cache_all, v_cache_all, layer, cum, *, row0, nseq, T, P, H, dh):
    G = SUBLANES if H % SUBLANES == 0 else H
    assert G & (G - 1) == 0
    NG = H // G
    R = T * G
    new = proj[row0:row0 + nseq * T].reshape(nseq, T, 4, NG, G, dh).transpose(2, 0, 3, 1, 4, 5)
    new = new.reshape(4, nseq, NG, R, dh)
    ck = cum.reshape(nseq, P + T, NG, G).transpose(0, 2, 1, 3).reshape(nseq, NG, 1, (P + T) * G)
    cq = cum[:, P:].reshape(nseq, T, NG, G).transpose(0, 2, 1, 3).reshape(nseq, NG, R, 1)
    part = lambda a: pl.BlockSpec((None, 1, 1, R, dh), lambda b, g: (a, b, g, 0, 0))
    cache = pl.BlockSpec((None, 1, P, G, dh), lambda b, g: (layer, b, 0, g, 0))
    out = pl.pallas_call(
        functools.partial(_fox_sample_kernel, P=P, T=T, G=G, dh=dh, scale=dh ** -0.5),
        grid=(nseq, NG),
        in_specs=[part(0), part(1), part(2), part(3), cache, cache,
                  pl.BlockSpec((1, 1, R, 1), lambda b, g: (b, g, 0, 0)),
                  pl.BlockSpec((1, 1, 1, (P + T) * G), lambda b, g: (b, g, 0, 0))],
        out_specs=pl.BlockSpec((1, 1, R, dh), lambda b, g: (b, g, 0, 0)),
        out_shape=jax.ShapeDtypeStruct((nseq, NG, R, dh), BF16),
        compiler_params=_params("parallel", "parallel"),
        name="fox_sample_attention",
    )(new, new, new, new, k_cache_all, v_cache_all, cq, ck)
    return out.reshape(nseq, NG, T, G, dh).transpose(0, 2, 1, 3, 4).reshape(nseq * T, H * dh)


def kernel(x_prompt, x_sample, state_gdn, state_gdn_conv, cache_fox_k, cache_fox_v, cache_fox_logf,
           norm_mix, norm_mlp, norm_final, gdn_w_in, gdn_conv_w, gdn_a_log, gdn_dt_bias, gdn_norm_w,
           gdn_w_out, fox_w_in, fox_b_f, fox_w_out, mlp_w_up, mlp_w_down):
    Bp, Tp, D = x_prompt.shape
    Bs, Ts, _ = x_sample.shape
    depth = norm_mix.shape[0]
    _, _, hv, dk, dv = state_gdn.shape
    conv_dim = state_gdn_conv.shape[-1]
    cw1 = state_gdn_conv.shape[2]
    v_dim = hv * dv
    qk_dim = (conv_dim - v_dim) // 2
    hqk = qk_dim // dk
    _, _, P, H, dh = cache_fox_k.shape
    fox_dim = H * dh
    d_ff = mlp_w_up.shape[-1]
    Mp, Ms = Bp * Tp, Bs * Ts
    Cp = min(GDN_CHUNK, Tp)

    x = jnp.concatenate([x_prompt.reshape(Mp, D), x_sample.reshape(Ms, D)], axis=0)
    gdn_w_in_t = jnp.swapaxes(gdn_w_in, 1, 2)
    fox_w_in_t = jnp.swapaxes(fox_w_in, 1, 2)
    S_p, S_s, buf_p, buf_s = None, None, [], []
    n_gdn = state_gdn.shape[0]
    k_p, k_s, v_p, v_s, lf_p, lf_s = [], [], [], [], [], []

    for i in range(depth):
        j = i // 2
        if i % 2 == 0:
            n_main = conv_dim + v_dim
            proj, beta, g = norm_proj(x, norm_mix[i], gdn_w_in_t, j, n_main, w_t=True, gate="gdn",
                                      gate_params=(gdn_a_log[j], gdn_dt_bias[j]), name="gdn_in_proj")

            common = dict(hv=hv, hqk=hqk, dk=dk, dv=dv)
            o, S_p = gdn_chunks(proj, gdn_conv_w, None, g, beta, gdn_norm_w[j], None, j, n_gdn, S_p,
                                row0=0, nseq=Bp, T=Tp, C=Cp, out_rows=Mp + Ms, **common)
            o_s, S_s = gdn_chunks(proj, gdn_conv_w, state_gdn_conv, g, beta, gdn_norm_w[j], state_gdn, j,
                                  n_gdn, S_s, row0=Mp, nseq=Bs, T=Ts, C=Ts, out_rows=Ms, **common)
            o = lax.dynamic_update_slice(o, o_s, (Mp, 0))
            x = matmul_res(o, gdn_w_out, j, x, name="gdn_out_proj")
            buf_p.append(jnp.stack([proj[(b + 1) * Tp - cw1:(b + 1) * Tp, :conv_dim]
                                    for b in range(Bp)]).astype(F32))
            qkv_s = proj[Mp:, :conv_dim].reshape(Bs, Ts, conv_dim).astype(F32)
            buf_s.append(jnp.concatenate([state_gdn_conv[j], qkv_s], axis=1)[:, -cw1:])
        else:
            proj, logf = norm_proj(x, norm_mix[i], fox_w_in_t, j, 4 * fox_dim, w_t=True, gate="fox",
                                   gate_params=(fox_b_f[j],), name="fox_in_proj")
            cum_p = cumsum_time(logf[:Mp].reshape(Bp, Tp, H))
            o = fox_prompt_attention(proj, cum_p.transpose(1, 0, 2).reshape(Mp, H),
                                     cum_p.transpose(2, 1, 0).reshape(H, Mp), nseq=Bp, T=Tp, H=H, dh=dh,
                                     out_rows=Mp + Ms)
            lf_new = logf[Mp:].reshape(Bs, Ts, H)
            lf_all = jnp.concatenate([cache_fox_logf[j], lf_new], axis=1)
            cum_s = cumsum_time(lf_all).transpose(1, 0, 2)
            o_s = fox_sample_attention(proj, cache_fox_k, cache_fox_v, j, cum_s,
                                       row0=Mp, nseq=Bs, T=Ts, P=P, H=H, dh=dh)
            o = lax.dynamic_update_slice(o, o_s, (Mp, 0))
            x = matmul_res(o, fox_w_out, j, x, name="fox_out_proj")
            kv = proj[:, fox_dim:3 * fox_dim].reshape(Mp + Ms, 2, H, dh)
            k_p.append(kv[:Mp, 0].astype(F32).reshape(Bp, Tp, H, dh))
            k_s.append(kv[Mp:, 0].astype(F32).reshape(Bs, Ts, H, dh))
            v_p.append(kv[:Mp, 1].astype(F32).reshape(Bp, Tp, H, dh))
            v_s.append(kv[Mp:, 1].astype(F32).reshape(Bs, Ts, H, dh))
            lf_p.append(logf[:Mp].reshape(Bp, Tp, H))
            lf_s.append(lf_new)
        hmid = norm_proj(x, norm_mlp[i], mlp_w_up, i, d_ff, relu2=True, name="mlp_up")
        x = matmul_res(hmid, mlp_w_down, i, x, name="mlp_down")

    y_p = rmsnorm_rows(x, norm_final, 0, Mp)
    y_s = rmsnorm_rows(x, norm_final, Mp, Ms)
    st = jnp.stack
    return (y_p.reshape(Bp, Tp, D), y_s.reshape(Bs, Ts, D),
            S_p, st(buf_p), st(k_p), st(v_p), st(lf_p),
            S_s, st(buf_s), st(k_s), st(v_s), st(lf_s))
```

```python
import functools

import jax
import jax.numpy as jnp
from jax import lax
from jax.experimental import pallas as pl
from jax.experimental.pallas import tpu as pltpu

F32 = jnp.float32
BF16 = jnp.bfloat16
EPS = 1e-6
GDN_CHUNK = 64
FOX_KEY_BLOCKS = 1
GDN_HEAD_GROUP = 32
CONV_HALO_ROWS = 16
INV_BASE_BLOCK = 8
VMEM_LIMIT_BYTES = 58 * 1024 * 1024
LOG2E = 1.4426950408889634
SUBLANES = 8
BF16_ROWS = 16
LANES = 128
ROW_TILE = 1088
IN_COL_TILE = 1024
OUT_COL_TILE = 512
K_TILE = 4096
NORM_CHUNK_ROWS = 272
NORM_ROW_TILE = 512
TIME_BLOCK = 256
FOX_HEAD_GROUP = 2
HIGHEST = lax.Precision.HIGHEST
NT_DIMS = (((1,), (1,)), ((), ()))
TN_DIMS = (((0,), (0,)), ((), ()))


def _pick(n, pref, align):
    best = None
    for d in range(align, min(n, pref) + 1, align):
        if n % d == 0:
            best = d
    return n if best is None else best


def _params(*sem):
    return pltpu.CompilerParams(dimension_semantics=sem, vmem_limit_bytes=VMEM_LIMIT_BYTES)


def _sigmoid(x):
    return 1.0 / (1.0 + jnp.exp(-x))


def _softplus(x):
    return jnp.maximum(x, 0.0) + jnp.log1p(jnp.exp(-jnp.abs(x)))


def _rms_rows(x, w):
    ms = jnp.mean(x * x, axis=-1, keepdims=True)
    return x * lax.rsqrt(ms + EPS) * w


def _wdot(xn, w_ref, w_t):
    w = w_ref[...].astype(BF16)
    if w_t:
        return lax.dot_general(xn, w, NT_DIMS, preferred_element_type=F32)
    return jnp.dot(xn, w, preferred_element_type=F32)


def _gdn_gate_outputs(acc, alog_ref, dtb_ref, beta_ref, g_ref):
    hv = beta_ref.shape[1]
    beta_ref[...] = _sigmoid(acc[:, :hv])
    g_ref[...] = -jnp.exp(alog_ref[...]) * _softplus(acc[:, hv:2 * hv] + dtb_ref[...])


def _fox_gate_outputs(acc, bf_ref, lf_ref):
    f = acc[:, :lf_ref.shape[1]] + bf_ref[...]
    lf_ref[...] = -_softplus(-f)


GATES = {"gdn": (_gdn_gate_outputs, 2, 2), "fox": (_fox_gate_outputs, 1, 1)}


def _norm_proj_kernel(x_ref, nw_ref, w_ref, *rest, relu2, w_t, gate):
    if gate is None:
        o_ref, xn_ref = rest
    else:
        epilogue, n_par, n_out = GATES[gate]
        gw_ref, par_refs = rest[0], rest[1:1 + n_par]
        o_ref, gate_refs, xn_ref = rest[1 + n_par], rest[2 + n_par:2 + n_par + n_out], rest[-1]

    def tile(xn, w16):
        if w_t:
            acc = lax.dot_general(xn, w16, NT_DIMS, preferred_element_type=F32)
        else:
            acc = jnp.dot(xn, w16, preferred_element_type=F32)
        if relu2:
            acc = jnp.square(jnp.maximum(acc, 0.0))
        return acc.astype(o_ref.dtype)

    j = pl.program_id(1)

    @pl.when(j == 0)
    def _():
        tm = x_ref.shape[0]
        rows = _pick(tm, NORM_CHUNK_ROWS, BF16_ROWS)
        w16 = w_ref[...].astype(BF16)
        for r in range(tm // rows):
            sl = slice(r * rows, (r + 1) * rows)
            xn = _rms_rows(x_ref[sl, :], nw_ref[...]).astype(BF16)
            xn_ref[sl, :] = xn
            o_ref[sl, :] = tile(xn, w16)
        if gate is not None:
            epilogue(_wdot(xn_ref[...], gw_ref, True), *par_refs, *gate_refs)

    @pl.when(j > 0)
    def _():
        o_ref[...] = tile(xn_ref[...], w_ref[...].astype(BF16))


def norm_proj(x, nw, w_all, layer, n_cols, *, relu2=False, w_t=False, gate=None, gate_params=(),
              name="norm_proj"):
    M, K = x.shape
    tm = _pick(M, ROW_TILE, BF16_ROWS)
    tn = _pick(n_cols, IN_COL_TILE, LANES)
    if w_t:
        w_spec = pl.BlockSpec((None, tn, K), lambda i, j: (layer, j, 0))
    else:
        w_spec = pl.BlockSpec((None, K, tn), lambda i, j: (layer, 0, j))
    in_specs = [pl.BlockSpec((tm, K), lambda i, j: (i, 0)), pl.BlockSpec((1, K), lambda i, j: (0, 0)), w_spec]
    args = [x, nw.reshape(1, K), w_all]
    out_specs = [pl.BlockSpec((tm, tn), lambda i, j: (i, j))]
    out_shape = [jax.ShapeDtypeStruct((M, n_cols), BF16)]
    if gate is not None:
        assert w_t and n_cols % LANES == 0
        H = gate_params[0].shape[0]
        assert GATES[gate][2] * H <= LANES
        in_specs.append(pl.BlockSpec((None, LANES, K), lambda i, j: (layer, n_cols // LANES, 0)))
        args.append(w_all)
        for p in gate_params:
            in_specs.append(pl.BlockSpec((1, H), lambda i, j: (0, 0)))
            args.append(p.reshape(1, H))
        for _ in range(GATES[gate][2]):
            out_specs.append(pl.BlockSpec((tm, H), lambda i, j: (i, 0)))
            out_shape.append(jax.ShapeDtypeStruct((M, H), F32))
    out = pl.pallas_call(
        functools.partial(_norm_proj_kernel, relu2=relu2, w_t=w_t, gate=gate),
        grid=(M // tm, n_cols // tn),
        in_specs=in_specs,
        out_specs=out_specs,
        out_shape=out_shape,
        scratch_shapes=[pltpu.VMEM((tm, K), BF16)],
        compiler_params=_params("parallel", "arbitrary"),
        name=name,
    )(*args)
    return out[0] if gate is None else out


def _matmul_res_kernel(a_ref, w_ref, r_ref, o_ref, acc_ref, *, nk):
    part = jnp.dot(a_ref[...], w_ref[...].astype(BF16), preferred_element_type=F32)
    if nk == 1:
        o_ref[...] = r_ref[...] + part
        return
    k = pl.program_id(2)

    @pl.when(k == 0)
    def _():
        acc_ref[...] = part

    @pl.when(k > 0)
    def _():
        acc_ref[...] += part

    @pl.when(k == nk - 1)
    def _():
        o_ref[...] = r_ref[...] + acc_ref[...]


def matmul_res(a, w_all, layer, res, *, name="matmul_res"):
    M, K = a.shape
    N = w_all.shape[-1]
    tm = _pick(M, ROW_TILE, BF16_ROWS)
    tn = _pick(N, OUT_COL_TILE, LANES)
    tk = _pick(K, K_TILE, LANES)
    nk = K // tk
    return pl.pallas_call(
        functools.partial(_matmul_res_kernel, nk=nk),
        grid=(M // tm, N // tn, nk),
        in_specs=[pl.BlockSpec((tm, tk), lambda i, j, k: (i, k)),
                  pl.BlockSpec((None, tk, tn), lambda i, j, k: (layer, k, j)),
                  pl.BlockSpec((tm, tn), lambda i, j, k: (i, j))],
        out_specs=pl.BlockSpec((tm, tn), lambda i, j, k: (i, j)),
        out_shape=jax.ShapeDtypeStruct((M, N), F32),
        scratch_shapes=[pltpu.VMEM((tm, tn), F32)],
        compiler_params=_params("parallel", "parallel", "arbitrary"),
        name=name,
    )(a, w_all, res)


def _rmsnorm_kernel(x_ref, nw_ref, o_ref):
    o_ref[...] = _rms_rows(x_ref[...], nw_ref[...])


def rmsnorm_rows(x, nw, row0, n_rows):
    K = x.shape[1]
    tm = _pick(n_rows, NORM_ROW_TILE, SUBLANES)
    rb0 = row0 // tm
    return pl.pallas_call(
        _rmsnorm_kernel,
        grid=(n_rows // tm,),
        in_specs=[pl.BlockSpec((tm, K), lambda i: (rb0 + i, 0)), pl.BlockSpec((1, K), lambda i: (0, 0))],
        out_specs=pl.BlockSpec((tm, K), lambda i: (i, 0)),
        out_shape=jax.ShapeDtypeStruct((n_rows, K), F32),
        compiler_params=_params("parallel"),
        name="final_rmsnorm",
    )(x, nw.reshape(1, K))


def _gdn_chunk_kernel(x_ref, xp_ref, cw_ref, z_ref, g_ref, gt_ref, bt_ref, nw_ref, *rest,
                      C, hq, rep, dk, dv, n_chunks, has_s0, has_prev):
    rest = list(rest)
    buf_ref = rest.pop(0) if has_s0 else None
    s0_ref = rest.pop(0) if has_s0 else None
    if has_prev:
        rest.pop(0)
    o_ref, sout_ref, s_ref = rest
    n = pl.program_id(2)
    C2 = 2 * C
    cw = cw_ref.shape[0]
    qk_dim = hq * dk
    pair = 2 * dk
    assert dv == dk and cw - 1 <= CONV_HALO_ROWS <= C

    @pl.when(n == 0)
    def _():
        if has_s0:
            s_ref[...] = s0_ref[0]
        else:
            s_ref[...] = jnp.zeros_like(s_ref)

    K = CONV_HALO_ROWS + C
    sr = lax.broadcasted_iota(jnp.int32, ((cw - 1) * C, K), 0)
    sc = lax.broadcasted_iota(jnp.int32, ((cw - 1) * C, K), 1)
    cshift = C.bit_length() - 1
    sel = jnp.where(sc == CONV_HALO_ROWS + (sr & (C - 1)) - ((sr >> cshift) + 1), 1.0, 0.0).astype(BF16)
    first = n == 0
    trow = lax.broadcasted_iota(jnp.int32, (C, pair), 0)
    conv_cache = {}

    def conv_pair(p):
        if p in conv_cache:
            return conv_cache[p]
        cs = slice(p * pair, (p + 1) * pair)
        x_cur = x_ref[:, cs]
        x_prev = xp_ref[C - CONV_HALO_ROWS:C, cs]
        x_prev = jnp.where(first, jnp.zeros_like(x_prev), x_prev)
        sh = jnp.dot(sel, jnp.concatenate([x_prev, x_cur], axis=0), preferred_element_type=F32)
        acc = x_cur.astype(F32) * cw_ref[cw - 1:cw, cs]
        for s in range(1, cw):
            acc = acc + sh[(s - 1) * C:s * C] * cw_ref[cw - 1 - s:cw - s, cs]
        if has_s0:
            corr = jnp.zeros((C, pair), F32)
            for t in range(cw - 1):
                r = sum(cw_ref[cw - 1 - s:cw - s, cs] * buf_ref[0, cw - 1 - (s - t):cw - (s - t), cs]
                        for s in range(t + 1, cw))
                corr = jnp.where(trow == t, r, corr)
            acc = acc + jnp.where(first, corr, 0.0)
        y = acc * _sigmoid(acc)
        conv_cache[p] = y
        return y

    def conv(c0, l2_scale=None):
        y = conv_pair(c0 // pair)[:, (c0 % pair):(c0 % pair) + dk]
        if l2_scale is None:
            return y
        return y * (lax.rsqrt(jnp.sum(y * y, axis=-1, keepdims=True) + EPS) * l2_scale)

    row = lax.broadcasted_iota(jnp.int32, (C, C2), 0)
    lane = lax.broadcasted_iota(jnp.int32, (C, C2), 1)
    right = lane >= C
    col = jnp.where(right, lane - C, lane)
    incl = row >= col
    strict = row > col
    eye_right = jnp.where(lane == row + C, 1.0, 0.0).astype(F32)
    triu2 = jnp.where(row <= col, 1.0, 0.0).astype(F32)
    tril = jnp.where(lax.broadcasted_iota(jnp.int32, (C, C), 0) >= lax.broadcasted_iota(jnp.int32, (C, C), 1),
                     1.0, 0.0).astype(F32)
    zeros_z = jnp.zeros((C, C2), BF16)
    zeros_r = jnp.zeros((C, dk + dv), BF16)
    G2_all = jnp.dot(g_ref[0, n], triu2, precision=HIGHEST, preferred_element_type=F32)
    Gc_all = jnp.dot(tril, gt_ref[...], precision=HIGHEST, preferred_element_type=F32)
    beta_all = bt_ref[...]

    def head_group(heads):
        qa = sorted({i // rep for i in heads})
        q = {a: conv(a * dk, dk ** -0.5) for a in qa}
        k = {a: conv(qk_dim + a * dk, 1.0) for a in qa}
        q16 = {a: q[a].astype(BF16) for a in qa}
        kk16 = {a: jnp.concatenate([k[a].astype(BF16)] * 2, axis=0) for a in qa}
        G2, G_col, b_col, dec_incl, gamma, G_last, kb, vb = {}, {}, {}, {}, {}, {}, {}, {}
        for i in heads:
            G2[i] = G2_all[i:i + 1, :]
            G_col[i] = Gc_all[:, i:i + 1]
            b_col[i] = beta_all[:, i:i + 1]
            dec_incl[i] = jnp.exp(jnp.where(incl, G_col[i] - G2[i], -jnp.inf))
            gamma[i] = jnp.exp(G_col[i])
            G_last[i] = G2[i][:, C - 1:C]
            kb[i] = k[i // rep] * b_col[i]
            vb[i] = conv(2 * qk_dim + i * dv) * b_col[i]
        gram = {i: lax.dot_general(jnp.concatenate([kb[i].astype(BF16), q16[i // rep]], axis=0), kk16[i // rep],
                                   NT_DIMS, preferred_element_type=F32) for i in heads}
        Lf = {i: gram[i][:C] * dec_incl[i] for i in heads}
        base = min(INV_BASE_BLOCK, C)
        sh = base.bit_length() - 1
        in_base = jnp.logical_and(strict, (row >> sh) == (col >> sh))
        Z = {i: jnp.where(right, eye_right, -jnp.where(in_base, Lf[i], 0.0)) for i in heads}
        for _ in range((base - 1).bit_length()):
            Z16 = {i: Z[i].astype(BF16) for i in heads}
            Z = {i: jnp.dot(Z16[i], jnp.concatenate([Z16[i], zeros_z], axis=0), preferred_element_type=F32)
                 + jnp.where(right, Z[i], 0.0) for i in heads}
        b = base
        while b < C:
            sh = b.bit_length() - 1
            lower_left = jnp.logical_and(jnp.logical_and(right, (row >> (sh + 1)) == (col >> (sh + 1))),
                                         jnp.logical_and(((row >> sh) & 1) == 1, ((col >> sh) & 1) == 0))
            Z16 = {i: Z[i].astype(BF16) for i in heads}
            W16 = {i: jnp.dot(jnp.where(lower_left, Lf[i], 0.0).astype(BF16),
                              jnp.concatenate([zeros_z, Z16[i]], axis=0),
                              preferred_element_type=F32).astype(BF16) for i in heads}
            Z = {i: Z[i] - jnp.dot(Z16[i], jnp.concatenate([zeros_z, W16[i]], axis=0),
                                   preferred_element_type=F32) for i in heads}
            b *= 2
        wu = {}
        for i in heads:
            rhs = jnp.concatenate([(kb[i] * gamma[i]).astype(BF16), vb[i].astype(BF16)], axis=1)
            wu[i] = jnp.dot(Z[i].astype(BF16), jnp.concatenate([zeros_r, rhs], axis=0),
                            preferred_element_type=F32)
        S = {i: s_ref[i] for i in heads}
        S16 = {i: S[i].astype(BF16) for i in heads}
        ws_qs = {i: jnp.dot(jnp.concatenate([wu[i][:, :dk].astype(BF16),
                                             (q[i // rep] * gamma[i]).astype(BF16)], axis=0),
                            S16[i], preferred_element_type=F32) for i in heads}
        U16 = {i: (wu[i][:, dk:] - ws_qs[i][:C]).astype(BF16) for i in heads}
        A16 = {i: (gram[i][C:] * dec_incl[i])[:, :C].astype(BF16) for i in heads}
        o = {i: ws_qs[i][C:] + jnp.dot(A16[i], U16[i], preferred_element_type=F32) for i in heads}
        for i in heads:
            kd = k[i // rep] * jnp.exp(G_last[i] - G_col[i])
            s_ref[i] = jnp.exp(G_last[i]) * S[i] + lax.dot_general(kd.astype(BF16), U16[i], TN_DIMS,
                                                                   preferred_element_type=F32)
        for i in heads:
            z = z_ref[:, i * dv:(i + 1) * dv].astype(F32)
            o_ref[:, i * dv:(i + 1) * dv] = (_rms_rows(o[i], nw_ref[...])
                                             * (z * _sigmoid(z))).astype(o_ref.dtype)

    n_heads = hq * rep
    for h0 in range(0, n_heads, GDN_HEAD_GROUP):
        head_group(range(h0, min(h0 + GDN_HEAD_GROUP, n_heads)))

    @pl.when(n == n_chunks - 1)
    def _():
        sout_ref[0] = s_ref[...]


def gdn_chunks(proj, conv_w_all, buf_all, g, beta, norm_w, s0_all, layer, n_layers, s_prev, *, row0,
               nseq, T, C, hv, hqk, dk, dv, out_rows):
    N = T // C
    cw = conv_w_all.shape[1]
    conv_dim = 2 * hqk * dk + hv * dv
    assert conv_dim % (hv * dv) == 0
    has_s0 = s0_all is not None
    rb0 = row0 // C
    g_rows = g[row0:row0 + nseq * T].reshape(nseq, N, C, hv).transpose(0, 1, 3, 2)
    tok = pl.BlockSpec((C, hv), lambda b, h, n: (rb0 + b * N + n, 0))
    in_specs = [pl.BlockSpec((C, conv_dim), lambda b, h, n: (rb0 + b * N + n, 0)),
                pl.BlockSpec((C, conv_dim), lambda b, h, n: (rb0 + b * N + jnp.maximum(n - 1, 0), 0)),
                pl.BlockSpec((None, cw, conv_dim), lambda b, h, n: (layer, 0, 0)),
                pl.BlockSpec((C, hv * dv), lambda b, h, n: (rb0 + b * N + n, conv_dim // (hv * dv))),
                pl.BlockSpec((1, N, hv, C), lambda b, h, n: (b, 0, 0, 0)),
                tok, tok,
                pl.BlockSpec((1, dv), lambda b, h, n: (0, 0))]
    args = [proj, proj, conv_w_all, proj, g_rows, g, beta, norm_w.reshape(1, dv)]
    if has_s0:
        in_specs += [pl.BlockSpec((None, 1, cw - 1, conv_dim), lambda b, h, n: (layer, b, 0, 0)),
                     pl.BlockSpec((None, 1, hv, dk, dv), lambda b, h, n: (layer, b, 0, 0, 0))]
        args += [buf_all, s0_all]
    aliases = {}
    if s_prev is not None:
        aliases[len(args)] = 1
        in_specs.append(pl.BlockSpec(memory_space=pl.ANY))
        args.append(s_prev)
    return pl.pallas_call(
        functools.partial(_gdn_chunk_kernel, C=C, hq=hqk, rep=hv // hqk, dk=dk, dv=dv, n_chunks=N,
                          has_s0=has_s0, has_prev=s_prev is not None),
        grid=(nseq, 1, N),
        in_specs=in_specs,
        out_specs=[pl.BlockSpec((C, hv * dv), lambda b, h, n: (b * N + n, 0)),
                   pl.BlockSpec((None, 1, hv, dk, dv), lambda b, h, n: (layer, b, 0, 0, 0))],
        out_shape=[jax.ShapeDtypeStruct((out_rows, hv * dv), BF16),
                   jax.ShapeDtypeStruct((n_layers, nseq, hv, dk, dv), F32)],
        input_output_aliases=aliases,
        scratch_shapes=[pltpu.VMEM((hv, dk, dv), F32)],
        compiler_params=_params("parallel", "parallel", "arbitrary"),
        name="gdn_chunks",
    )(*args)


def _cumsum_kernel(x_ref, o_ref, carry_ref, *, tb):
    @pl.when(pl.program_id(0) == 0)
    def _():
        carry_ref[...] = jnp.zeros_like(carry_ref)

    row = lax.broadcasted_iota(jnp.int32, (tb, tb), 0)
    col = lax.broadcasted_iota(jnp.int32, (tb, tb), 1)
    tril = jnp.where(row >= col, 1.0, 0.0).astype(F32)
    c = jnp.dot(tril, x_ref[...], precision=HIGHEST, preferred_element_type=F32) + carry_ref[...]
    o_ref[...] = c
    carry_ref[...] = c[tb - 1:tb, :]


def cumsum_time(x):
    nseq, L, H = x.shape
    S = nseq * H
    tb = _pick(L, TIME_BLOCK, SUBLANES)
    out = pl.pallas_call(
        functools.partial(_cumsum_kernel, tb=tb),
        grid=(L // tb,),
        in_specs=[pl.BlockSpec((tb, S), lambda t: (t, 0))],
        out_specs=pl.BlockSpec((tb, S), lambda t: (t, 0)),
        out_shape=jax.ShapeDtypeStruct((L, S), F32),
        scratch_shapes=[pltpu.VMEM((1, S), F32)],
        compiler_params=_params("arbitrary"),
        name="fox_cumsum",
    )(x.transpose(1, 0, 2).reshape(L, S))
    return out.reshape(L, nseq, H)


def _fox_prompt_kernel(q_ref, k_ref, v_ref, og_ref, cq_ref, ckt_ref, o_ref, acc_ref, m_ref, cqr_ref, q2_ref,
                       *, bq, bk, H, dh, scale, hg):
    i = pl.program_id(1)
    j = pl.program_id(2)
    rep = bk // dh
    r = bk // bq
    j_last = i // r

    @pl.when(j == 0)
    def _():
        m_ref[...] = jnp.full(m_ref.shape, -jnp.inf, F32)
        acc_ref[...] = jnp.zeros_like(acc_ref)
        cqs = cq_ref[...] * LOG2E
        for h in range(H):
            cqr_ref[h] = jnp.broadcast_to(cqs[:, h:h + 1], (bq, dh))
        q2_ref[...] = (q_ref[...].astype(F32) * (scale * LOG2E)).astype(BF16)

    def block(masked):
        if masked:
            mask = (lax.broadcasted_iota(jnp.int32, (bq, bk), 1)
                    <= lax.broadcasted_iota(jnp.int32, (bq, bk), 0) + (i - r * j_last) * bq)
        ones16 = jnp.ones((bk, dh), BF16)
        ck2 = ckt_ref[...] * LOG2E

        def qk(h):
            hs = slice(h * dh, (h + 1) * dh)
            return lax.dot_general(q2_ref[:, hs], k_ref[:, hs], NT_DIMS, preferred_element_type=F32)

        groups = [range(g * hg, (g + 1) * hg) for g in range(H // hg)]
        s = {h: qk(h) for h in groups[0]}
        for gi, grp in enumerate(groups):
            if gi + 1 < len(groups):
                for h in groups[gi + 1]:
                    s[h] = qk(h)
            p16, alpha2 = {}, {}
            for h in grp:
                t = s.pop(h) - ck2[h:h + 1, :]
                if masked:
                    t = jnp.where(mask, t, -jnp.inf)
                cq2 = cqr_ref[h]
                m_prev = m_ref[h]
                m_new = jnp.maximum(m_prev, jnp.max(t, axis=1, keepdims=True) + cq2)
                m_ref[h] = m_new
                alpha = jnp.exp2(m_prev - m_new)
                c = cq2 - m_new
                p16[h] = jnp.exp2(t + jnp.concatenate([c] * rep, axis=1)).astype(BF16)
                alpha2[h] = jnp.concatenate([alpha, alpha], axis=1)
            for h in grp:
                hs = slice(h * dh, (h + 1) * dh)
                v_ext = jnp.concatenate([v_ref[:, hs], ones16], axis=1)
                acc_ref[h] = alpha2[h] * acc_ref[h] + jnp.dot(p16[h], v_ext, preferred_element_type=F32)

    @pl.when(j < j_last)
    def _():
        block(False)

    @pl.when(j == j_last)
    def _():
        block(True)
        for h in range(H):
            hs = slice(h * dh, (h + 1) * dh)
            a = acc_ref[h]
            o_ref[:, hs] = ((a[:, :dh] / a[:, dh:]) * _sigmoid(og_ref[:, hs].astype(F32))).astype(o_ref.dtype)


def fox_prompt_attention(proj, cum, cum_t, *, nseq, T, H, dh, out_rows):
    bq = _pick(T, TIME_BLOCK, LANES)
    bk = _pick(T, FOX_KEY_BLOCKS * bq, bq)
    nq, nk, r = T // bq, T // bk, bk // bq
    D = H * dh
    hg = _pick(H, FOX_HEAD_GROUP, 1)
    kv = lambda col: (lambda b, i, j: (b * nk + jnp.minimum(j, i // r), col))
    return pl.pallas_call(
        functools.partial(_fox_prompt_kernel, bq=bq, bk=bk, H=H, dh=dh, scale=dh ** -0.5, hg=hg),
        grid=(nseq, nq, nk),
        in_specs=[pl.BlockSpec((bq, D), lambda b, i, j: (b * nq + i, 0)),
                  pl.BlockSpec((bk, D), kv(1)),
                  pl.BlockSpec((bk, D), kv(2)),
                  pl.BlockSpec((bq, D), lambda b, i, j: (b * nq + i, 3)),
                  pl.BlockSpec((bq, H), lambda b, i, j: (b * nq + i, 0)),
                  pl.BlockSpec((H, bk), lambda b, i, j: (0, b * nk + jnp.minimum(j, i // r)))],
        out_specs=pl.BlockSpec((bq, D), lambda b, i, j: (b * nq + i, 0)),
        out_shape=jax.ShapeDtypeStruct((out_rows, D), BF16),
        scratch_shapes=[pltpu.VMEM((H, bq, 2 * dh), F32), pltpu.VMEM((H, bq, dh), F32),
                        pltpu.VMEM((H, bq, dh), F32), pltpu.VMEM((bq, D), BF16)],
        compiler_params=_params("parallel", "parallel", "arbitrary"),
        name="fox_prompt_attention",
    )(proj, proj, proj, proj, cum, cum_t)


def _fox_sample_kernel(q_ref, kn_ref, vn_ref, og_ref, kc_ref, vc_ref, cq_ref, ck_ref, o_ref,
                       *, P, T, G, dh, scale):
    R, PG = T * G, P * G
    gs = G.bit_length() - 1
    q = (q_ref[0, 0].astype(F32) * scale).astype(BF16)
    kc = kc_ref[0].reshape(PG, dh).astype(BF16)
    vc = vc_ref[0].reshape(PG, dh).astype(BF16)
    s_p = lax.dot_general(q, kc, NT_DIMS, preferred_element_type=F32)
    s_n = lax.dot_general(q, kn_ref[0, 0], NT_DIMS, preferred_element_type=F32)
    cq = cq_ref[0, 0]
    ck = ck_ref[0, 0]
    same_p = ((lax.broadcasted_iota(jnp.int32, (R, PG), 0) & (G - 1))
              == (lax.broadcasted_iota(jnp.int32, (R, PG), 1) & (G - 1)))
    rn = lax.broadcasted_iota(jnp.int32, (R, R), 0)
    cn = lax.broadcasted_iota(jnp.int32, (R, R), 1)
    same_n = jnp.logical_and((rn & (G - 1)) == (cn & (G - 1)), (cn >> gs) <= (rn >> gs))
    s_p = jnp.where(same_p, s_p + (cq - ck[:, :PG]), -jnp.inf)
    s_n = jnp.where(same_n, s_n + (cq - ck[:, PG:]), -jnp.inf)
    m = jnp.maximum(jnp.max(s_p, axis=1, keepdims=True), jnp.max(s_n, axis=1, keepdims=True))
    p_p = jnp.exp(s_p - m)
    p_n = jnp.exp(s_n - m)
    l = jnp.sum(p_p, axis=1, keepdims=True) + jnp.sum(p_n, axis=1, keepdims=True)
    o = (jnp.dot(p_p.astype(BF16), vc, preferred_element_type=F32)
         + jnp.dot(p_n.astype(BF16), vn_ref[0, 0], preferred_element_type=F32))
    o_ref[0, 0] = ((o / l) * _sigmoid(og_ref[0, 0].astype(F32))).astype(o_ref.dtype)


def fox_sample_attention(proj, k_cache_all, v_cache_all, layer, cum, *, row0, nseq, T, P, H, dh):
    G = SUBLANES if H % SUBLANES == 0 else H
    assert G & (G - 1) == 0
    NG = H // G
    R = T * G
    new = proj[row0:row0 + nseq * T].reshape(nseq, T, 4, NG, G, dh).transpose(2, 0, 3, 1, 4, 5)
    new = new.reshape(4, nseq, NG, R, dh)
    ck = cum.reshape(nseq, P + T, NG, G).transpose(0, 2, 1, 3).reshape(nseq, NG, 1, (P + T) * G)
    cq = cum[:, P:].reshape(nseq, T, NG, G).transpose(0, 2, 1, 3).reshape(nseq, NG, R, 1)
    part = lambda a: pl.BlockSpec((None, 1, 1, R, dh), lambda b, g: (a, b, g, 0, 0))
    cache = pl.BlockSpec((None, 1, P, G, dh), lambda b, g: (layer, b, 0, g, 0))
    out = pl.pallas_call(
        functools.partial(_fox_sample_kernel, P=P, T=T, G=G, dh=dh, scale=dh ** -0.5),
        grid=(nseq, NG),
        in_specs=[part(0), part(1), part(2), part(3), cache, cache,
                  pl.BlockSpec((1, 1, R, 1), lambda b, g: (b, g, 0, 0)),
                  pl.BlockSpec((1, 1, 1, (P + T) * G), lambda b, g: (b, g, 0, 0))],
        out_specs=pl.BlockSpec((1, 1, R, dh), lambda b, g: (b, g, 0, 0)),
        out_shape=jax.ShapeDtypeStruct((nseq, NG, R, dh), BF16),
        compiler_params=_params("parallel", "parallel"),
        name="fox_sample_attention",
    )(new, new, new, new, k_cache_all, v_cache_all, cq, ck)
    return out.reshape(nseq, NG, T, G, dh).transpose(0, 2, 1, 3, 4).reshape(nseq * T, H * dh)


def kernel(x_prompt, x_sample, state_gdn, state_gdn_conv, cache_fox_k, cache_fox_v, cache_fox_logf,
           norm_mix, norm_mlp, norm_final, gdn_w_in, gdn_conv_w, gdn_a_log, gdn_dt_bias, gdn_norm_w,
           gdn_w_out, fox_w_in, fox_b_f, fox_w_out, mlp_w_up, mlp_w_down):
    Bp, Tp, D = x_prompt.shape
    Bs, Ts, _ = x_sample.shape
    depth = norm_mix.shape[0]
    _, _, hv, dk, dv = state_gdn.shape
    conv_dim = state_gdn_conv.shape[-1]
    cw1 = state_gdn_conv.shape[2]
    v_dim = hv * dv
    qk_dim = (conv_dim - v_dim) // 2
    hqk = qk_dim // dk
    _, _, P, H, dh = cache_fox_k.shape
    fox_dim = H * dh
    d_ff = mlp_w_up.shape[-1]
    Mp, Ms = Bp * Tp, Bs * Ts
    Cp = min(GDN_CHUNK, Tp)

    x = jnp.concatenate([x_prompt.reshape(Mp, D), x_sample.reshape(Ms, D)], axis=0)
    gdn_w_in_t = jnp.swapaxes(gdn_w_in, 1, 2)
    fox_w_in_t = jnp.swapaxes(fox_w_in, 1, 2)
    S_p, S_s, buf_p, buf_s = None, None, [], []
    n_gdn = state_gdn.shape[0]
    k_p, k_s, v_p, v_s, lf_p, lf_s = [], [], [], [], [], []

    for i in range(depth):
        j = i // 2
        if i % 2 == 0:
            n_main = conv_dim + v_dim
            proj, beta, g = norm_proj(x, norm_mix[i], gdn_w_in_t, j, n_main, w_t=True, gate="gdn",
                                      gate_params=(gdn_a_log[j], gdn_dt_bias[j]), name="gdn_in_proj")

            common = dict(hv=hv, hqk=hqk, dk=dk, dv=dv)
            o, S_p = gdn_chunks(proj, gdn_conv_w, None, g, beta, gdn_norm_w[j], None, j, n_gdn, S_p,
                                row0=0, nseq=Bp, T=Tp, C=Cp, out_rows=Mp + Ms, **common)
            o_s, S_s = gdn_chunks(proj, gdn_conv_w, state_gdn_conv, g, beta, gdn_norm_w[j], state_gdn, j,
                                  n_gdn, S_s, row0=Mp, nseq=Bs, T=Ts, C=Ts, out_rows=Ms, **common)
            o = lax.dynamic_update_slice(o, o_s, (Mp, 0))
            x = matmul_res(o, gdn_w_out, j, x, name="gdn_out_proj")
            buf_p.append(jnp.stack([proj[(b + 1) * Tp - cw1:(b + 1) * Tp, :conv_dim]
                                    for b in range(Bp)]).astype(F32))
            qkv_s = proj[Mp:, :conv_dim].reshape(Bs, Ts, conv_dim).astype(F32)
            buf_s.append(jnp.concatenate([state_gdn_conv[j], qkv_s], axis=1)[:, -cw1:])
        else:
            proj, logf = norm_proj(x, norm_mix[i], fox_w_in_t, j, 4 * fox_dim, w_t=True, gate="fox",
                                   gate_params=(fox_b_f[j],), name="fox_in_proj")
            cum_p = cumsum_time(logf[:Mp].reshape(Bp, Tp, H))
            o = fox_prompt_attention(proj, cum_p.transpose(1, 0, 2).reshape(Mp, H),
                                     cum_p.transpose(2, 1, 0).reshape(H, Mp), nseq=Bp, T=Tp, H=H, dh=dh,
                                     out_rows=Mp + Ms)
            lf_new = logf[Mp:].reshape(Bs, Ts, H)
            lf_all = jnp.concatenate([cache_fox_logf[j], lf_new], axis=1)
            cum_s = cumsum_time(lf_all).transpose(1, 0, 2)
            o_s = fox_sample_attention(proj, cache_fox_k, cache_fox_v, j, cum_s,
                                       row0=Mp, nseq=Bs, T=Ts, P=P, H=H, dh=dh)
            o = lax.dynamic_update_slice(o, o_s, (Mp, 0))
            x = matmul_res(o, fox_w_out, j, x, name="fox_out_proj")
            kv = proj[:, fox_dim:3 * fox_dim].reshape(Mp + Ms, 2, H, dh)
            k_p.append(kv[:Mp, 0].astype(F32).reshape(Bp, Tp, H, dh))
            k_s.append(kv[Mp:, 0].astype(F32).reshape(Bs, Ts, H, dh))
            v_p.append(kv[:Mp, 1].astype(F32).reshape(Bp, Tp, H, dh))
            v_s.append(kv[Mp:, 1].astype(F32).reshape(Bs, Ts, H, dh))
            lf_p.append(logf[:Mp].reshape(Bp, Tp, H))
            lf_s.append(lf_new)
        hmid = norm_proj(x, norm_mlp[i], mlp_w_up, i, d_ff, relu2=True, name="mlp_up")
        x = matmul_res(hmid, mlp_w_down, i, x, name="mlp_down")

    y_p = rmsnorm_rows(x, norm_final, 0, Mp)
    y_s = rmsnorm_rows(x, norm_final, Mp, Ms)
    st = jnp.stack
    return (y_p.reshape(Bp, Tp, D), y_s.reshape(Bs, Ts, D),
            S_p, st(buf_p), st(k_p), st(v_p), st(lf_p),
            S_s, st(buf_s), st(k_s), st(v_s), st(lf_s))
```

```python
import functools

import jax
import jax.numpy as jnp
from jax import lax
from jax.experimental import pallas as pl
from jax.experimental.pallas import tpu as pltpu

F32 = jnp.float32
BF16 = jnp.bfloat16
EPS = 1e-6
GDN_CHUNK = 64
FOX_KEY_BLOCKS = 1
GDN_HEAD_GROUP = 32
CONV_HALO_ROWS = 16
INV_BASE_BLOCK = 8
VMEM_LIMIT_BYTES = 60 * 1024 * 1024
LOG2E = 1.4426950408889634
SUBLANES = 8
BF16_ROWS = 16
LANES = 128
ROW_TILE = 1088
IN_COL_TILE = 1024
OUT_COL_TILE = 512
K_TILE = 4096
NORM_CHUNK_ROWS = 272
NORM_ROW_TILE = 512
TIME_BLOCK = 256
FOX_HEAD_GROUP = 2
HIGHEST = lax.Precision.HIGHEST
NT_DIMS = (((1,), (1,)), ((), ()))
TN_DIMS = (((0,), (0,)), ((), ()))


def _pick(n, pref, align):
    best = None
    for d in range(align, min(n, pref) + 1, align):
        if n % d == 0:
            best = d
    return n if best is None else best


def _params(*sem):
    return pltpu.CompilerParams(dimension_semantics=sem, vmem_limit_bytes=VMEM_LIMIT_BYTES)


def _sigmoid(x):
    return 1.0 / (1.0 + jnp.exp(-x))


def _softplus(x):
    return jnp.maximum(x, 0.0) + jnp.log1p(jnp.exp(-jnp.abs(x)))


def _rms_rows(x, w):
    ms = jnp.mean(x * x, axis=-1, keepdims=True)
    return x * lax.rsqrt(ms + EPS) * w


def _wdot(xn, w_ref, w_t):
    w = w_ref[...].astype(BF16)
    if w_t:
        return lax.dot_general(xn, w, NT_DIMS, preferred_element_type=F32)
    return jnp.dot(xn, w, preferred_element_type=F32)


def _gdn_gate_outputs(acc, alog_ref, dtb_ref, beta_ref, g_ref):
    hv = beta_ref.shape[1]
    beta_ref[...] = _sigmoid(acc[:, :hv])
    g_ref[...] = -jnp.exp(alog_ref[...]) * _softplus(acc[:, hv:2 * hv] + dtb_ref[...])


def _fox_gate_outputs(acc, bf_ref, lf_ref):
    f = acc[:, :lf_ref.shape[1]] + bf_ref[...]
    lf_ref[...] = -_softplus(-f)


GATES = {"gdn": (_gdn_gate_outputs, 2, 2), "fox": (_fox_gate_outputs, 1, 1)}


def _norm_proj_kernel(x_ref, nw_ref, w_ref, *rest, relu2, w_t, gate):
    if gate is None:
        o_ref, xn_ref = rest
    else:
        epilogue, n_par, n_out = GATES[gate]
        gw_ref, par_refs = rest[0], rest[1:1 + n_par]
        o_ref, gate_refs, xn_ref = rest[1 + n_par], rest[2 + n_par:2 + n_par + n_out], rest[-1]

    def tile(xn, w16):
        if w_t:
            acc = lax.dot_general(xn, w16, NT_DIMS, preferred_element_type=F32)
        else:
            acc = jnp.dot(xn, w16, preferred_element_type=F32)
        if relu2:
            acc = jnp.square(jnp.maximum(acc, 0.0))
        return acc.astype(o_ref.dtype)

    j = pl.program_id(1)

    @pl.when(j == 0)
    def _():
        tm = x_ref.shape[0]
        rows = _pick(tm, NORM_CHUNK_ROWS, BF16_ROWS)
        w16 = w_ref[...].astype(BF16)
        for r in range(tm // rows):
            sl = slice(r * rows, (r + 1) * rows)
            xn = _rms_rows(x_ref[sl, :], nw_ref[...]).astype(BF16)
            xn_ref[sl, :] = xn
            o_ref[sl, :] = tile(xn, w16)
        if gate is not None:
            epilogue(_wdot(xn_ref[...], gw_ref, True), *par_refs, *gate_refs)

    @pl.when(j > 0)
    def _():
        o_ref[...] = tile(xn_ref[...], w_ref[...].astype(BF16))


def norm_proj(x, nw, w_all, layer, n_cols, *, relu2=False, w_t=False, gate=None, gate_params=(),
              name="norm_proj"):
    M, K = x.shape
    tm = _pick(M, ROW_TILE, BF16_ROWS)
    tn = _pick(n_cols, IN_COL_TILE, LANES)
    if w_t:
        w_spec = pl.BlockSpec((None, tn, K), lambda i, j: (layer, j, 0))
    else:
        w_spec = pl.BlockSpec((None, K, tn), lambda i, j: (layer, 0, j))
    in_specs = [pl.BlockSpec((tm, K), lambda i, j: (i, 0)), pl.BlockSpec((1, K), lambda i, j: (0, 0)), w_spec]
    args = [x, nw.reshape(1, K), w_all]
    out_specs = [pl.BlockSpec((tm, tn), lambda i, j: (i, j))]
    out_shape = [jax.ShapeDtypeStruct((M, n_cols), BF16)]
    if gate is not None:
        assert w_t and n_cols % LANES == 0
        H = gate_params[0].shape[0]
        assert GATES[gate][2] * H <= LANES
        in_specs.append(pl.BlockSpec((None, LANES, K), lambda i, j: (layer, n_cols // LANES, 0)))
        args.append(w_all)
        for p in gate_params:
            in_specs.append(pl.BlockSpec((1, H), lambda i, j: (0, 0)))
            args.append(p.reshape(1, H))
        for _ in range(GATES[gate][2]):
            out_specs.append(pl.BlockSpec((tm, H), lambda i, j: (i, 0)))
            out_shape.append(jax.ShapeDtypeStruct((M, H), F32))
    out = pl.pallas_call(
        functools.partial(_norm_proj_kernel, relu2=relu2, w_t=w_t, gate=gate),
        grid=(M // tm, n_cols // tn),
        in_specs=in_specs,
        out_specs=out_specs,
        out_shape=out_shape,
        scratch_shapes=[pltpu.VMEM((tm, K), BF16)],
        compiler_params=_params("parallel", "arbitrary"),
        name=name,
    )(*args)
    return out[0] if gate is None else out


def _matmul_res_kernel(a_ref, w_ref, r_ref, o_ref, *acc, nk):
    part = jnp.dot(a_ref[...], w_ref[...].astype(BF16), preferred_element_type=F32)
    if nk == 1:
        o_ref[...] = r_ref[...] + part
        return
    (acc_ref,) = acc
    k = pl.program_id(1)
    j = pl.program_id(2)

    @pl.when(k == 0)
    def _():
        acc_ref[j] = part

    if nk > 2:
        @pl.when(jnp.logical_and(k > 0, k < nk - 1))
        def _():
            acc_ref[j] += part

    @pl.when(k == nk - 1)
    def _():
        o_ref[...] = r_ref[...] + (acc_ref[j] + part)


def matmul_res(a, w_all, layer, res, *, name="matmul_res"):
    M, K = a.shape
    N = w_all.shape[-1]
    tm = _pick(M, ROW_TILE, BF16_ROWS)
    tn = _pick(N, OUT_COL_TILE, LANES)
    tk = _pick(K, K_TILE, LANES)
    nk = K // tk
    out_idx = lambda i, k, j: (i, jnp.where(k == nk - 1, j, 0))
    return pl.pallas_call(
        functools.partial(_matmul_res_kernel, nk=nk),
        grid=(M // tm, nk, N // tn),
        in_specs=[pl.BlockSpec((tm, tk), lambda i, k, j: (i, k)),
                  pl.BlockSpec((None, tk, tn), lambda i, k, j: (layer, k, j)),
                  pl.BlockSpec((tm, tn), out_idx)],
        out_specs=pl.BlockSpec((tm, tn), out_idx),
        out_shape=jax.ShapeDtypeStruct((M, N), F32),
        scratch_shapes=[pltpu.VMEM((N // tn, tm, tn), F32)] if nk > 1 else [],
        compiler_params=_params("parallel", "arbitrary", "arbitrary"),
        name=name,
    )(a, w_all, res)


def _rmsnorm_kernel(x_ref, nw_ref, o_ref):
    o_ref[...] = _rms_rows(x_ref[...], nw_ref[...])


def rmsnorm_rows(x, nw, row0, n_rows):
    K = x.shape[1]
    tm = _pick(n_rows, NORM_ROW_TILE, SUBLANES)
    rb0 = row0 // tm
    return pl.pallas_call(
        _rmsnorm_kernel,
        grid=(n_rows // tm,),
        in_specs=[pl.BlockSpec((tm, K), lambda i: (rb0 + i, 0)), pl.BlockSpec((1, K), lambda i: (0, 0))],
        out_specs=pl.BlockSpec((tm, K), lambda i: (i, 0)),
        out_shape=jax.ShapeDtypeStruct((n_rows, K), F32),
        compiler_params=_params("parallel"),
        name="final_rmsnorm",
    )(x, nw.reshape(1, K))


def _gdn_chunk_kernel(x_ref, xp_ref, cw_ref, z_ref, g_ref, gt_ref, bt_ref, nw_ref, *rest,
                      C, hq, rep, dk, dv, n_chunks, has_s0, has_prev):
    rest = list(rest)
    buf_ref = rest.pop(0) if has_s0 else None
    s0_ref = rest.pop(0) if has_s0 else None
    if has_prev:
        rest.pop(0)
    o_ref, sout_ref, s_ref = rest
    n = pl.program_id(2)
    C2 = 2 * C
    cw = cw_ref.shape[0]
    qk_dim = hq * dk
    pair = 2 * dk
    assert dv == dk and cw - 1 <= CONV_HALO_ROWS <= C

    @pl.when(n == 0)
    def _():
        if has_s0:
            s_ref[...] = s0_ref[0]
        else:
            s_ref[...] = jnp.zeros_like(s_ref)

    K = CONV_HALO_ROWS + C
    sr = lax.broadcasted_iota(jnp.int32, ((cw - 1) * C, K), 0)
    sc = lax.broadcasted_iota(jnp.int32, ((cw - 1) * C, K), 1)
    cshift = C.bit_length() - 1
    sel = jnp.where(sc == CONV_HALO_ROWS + (sr & (C - 1)) - ((sr >> cshift) + 1), 1.0, 0.0).astype(BF16)
    first = n == 0
    trow = lax.broadcasted_iota(jnp.int32, (C, pair), 0)
    conv_cache = {}

    def conv_pair(p):
        if p in conv_cache:
            return conv_cache[p]
        cs = slice(p * pair, (p + 1) * pair)
        x_cur = x_ref[:, cs]
        x_prev = xp_ref[C - CONV_HALO_ROWS:C, cs]
        x_prev = jnp.where(first, jnp.zeros_like(x_prev), x_prev)
        sh = jnp.dot(sel, jnp.concatenate([x_prev, x_cur], axis=0), preferred_element_type=F32)
        acc = x_cur.astype(F32) * cw_ref[cw - 1:cw, cs]
        for s in range(1, cw):
            acc = acc + sh[(s - 1) * C:s * C] * cw_ref[cw - 1 - s:cw - s, cs]
        if has_s0:
            corr = jnp.zeros((C, pair), F32)
            for t in range(cw - 1):
                r = sum(cw_ref[cw - 1 - s:cw - s, cs] * buf_ref[0, cw - 1 - (s - t):cw - (s - t), cs]
                        for s in range(t + 1, cw))
                corr = jnp.where(trow == t, r, corr)
            acc = acc + jnp.where(first, corr, 0.0)
        y = acc * _sigmoid(acc)
        conv_cache[p] = y
        return y

    def conv(c0, l2_scale=None):
        y = conv_pair(c0 // pair)[:, (c0 % pair):(c0 % pair) + dk]
        if l2_scale is None:
            return y
        return y * (lax.rsqrt(jnp.sum(y * y, axis=-1, keepdims=True) + EPS) * l2_scale)

    row = lax.broadcasted_iota(jnp.int32, (C, C2), 0)
    lane = lax.broadcasted_iota(jnp.int32, (C, C2), 1)
    right = lane >= C
    col = jnp.where(right, lane - C, lane)
    incl = row >= col
    strict = row > col
    eye_right = jnp.where(lane == row + C, 1.0, 0.0).astype(F32)
    triu2 = jnp.where(row <= col, 1.0, 0.0).astype(F32)
    tril = jnp.where(lax.broadcasted_iota(jnp.int32, (C, C), 0) >= lax.broadcasted_iota(jnp.int32, (C, C), 1),
                     1.0, 0.0).astype(F32)
    zeros_z = jnp.zeros((C, C2), BF16)
    zeros_r = jnp.zeros((C, dk + dv), BF16)
    G2_all = jnp.dot(g_ref[0, n], triu2, precision=HIGHEST, preferred_element_type=F32)
    Gc_all = jnp.dot(tril, gt_ref[...], precision=HIGHEST, preferred_element_type=F32)
    beta_all = bt_ref[...]

    def head_group(heads):
        qa = sorted({i // rep for i in heads})
        q = {a: conv(a * dk, dk ** -0.5) for a in qa}
        k = {a: conv(qk_dim + a * dk, 1.0) for a in qa}
        q16 = {a: q[a].astype(BF16) for a in qa}
        kk16 = {a: jnp.concatenate([k[a].astype(BF16)] * 2, axis=0) for a in qa}
        G2, G_col, b_col, dec_incl, gamma, G_last, kb, vb = {}, {}, {}, {}, {}, {}, {}, {}
        for i in heads:
            G2[i] = G2_all[i:i + 1, :]
            G_col[i] = Gc_all[:, i:i + 1]
            b_col[i] = beta_all[:, i:i + 1]
            dec_incl[i] = jnp.exp(jnp.where(incl, G_col[i] - G2[i], -jnp.inf))
            gamma[i] = jnp.exp(G_col[i])
            G_last[i] = G2[i][:, C - 1:C]
            kb[i] = k[i // rep] * b_col[i]
            vb[i] = conv(2 * qk_dim + i * dv) * b_col[i]
        gram = {i: lax.dot_general(jnp.concatenate([kb[i].astype(BF16), q16[i // rep]], axis=0), kk16[i // rep],
                                   NT_DIMS, preferred_element_type=F32) for i in heads}
        Lf = {i: gram[i][:C] * dec_incl[i] for i in heads}
        base = min(INV_BASE_BLOCK, C)
        sh = base.bit_length() - 1
        in_base = jnp.logical_and(strict, (row >> sh) == (col >> sh))
        Z = {i: jnp.where(right, eye_right, -jnp.where(in_base, Lf[i], 0.0)) for i in heads}
        for _ in range((base - 1).bit_length()):
            Z16 = {i: Z[i].astype(BF16) for i in heads}
            Z = {i: jnp.dot(Z16[i], jnp.concatenate([Z16[i], zeros_z], axis=0), preferred_element_type=F32)
                 + jnp.where(right, Z[i], 0.0) for i in heads}
        b = base
        while b < C:
            sh = b.bit_length() - 1
            lower_left = jnp.logical_and(jnp.logical_and(right, (row >> (sh + 1)) == (col >> (sh + 1))),
                                         jnp.logical_and(((row >> sh) & 1) == 1, ((col >> sh) & 1) == 0))
            Z16 = {i: Z[i].astype(BF16) for i in heads}
            W16 = {i: jnp.dot(jnp.where(lower_left, Lf[i], 0.0).astype(BF16),
                              jnp.concatenate([zeros_z, Z16[i]], axis=0),
                              preferred_element_type=F32).astype(BF16) for i in heads}
            Z = {i: Z[i] - jnp.dot(Z16[i], jnp.concatenate([zeros_z, W16[i]], axis=0),
                                   preferred_element_type=F32) for i in heads}
            b *= 2
        wu = {}
        for i in heads:
            rhs = jnp.concatenate([(kb[i] * gamma[i]).astype(BF16), vb[i].astype(BF16)], axis=1)
            wu[i] = jnp.dot(Z[i].astype(BF16), jnp.concatenate([zeros_r, rhs], axis=0),
                            preferred_element_type=F32)
        S = {i: s_ref[i] for i in heads}
        S16 = {i: S[i].astype(BF16) for i in heads}
        ws_qs = {i: jnp.dot(jnp.concatenate([wu[i][:, :dk].astype(BF16),
                                             (q[i // rep] * gamma[i]).astype(BF16)], axis=0),
                            S16[i], preferred_element_type=F32) for i in heads}
        U16 = {i: (wu[i][:, dk:] - ws_qs[i][:C]).astype(BF16) for i in heads}
        A16 = {i: (gram[i][C:] * dec_incl[i])[:, :C].astype(BF16) for i in heads}
        o = {i: ws_qs[i][C:] + jnp.dot(A16[i], U16[i], preferred_element_type=F32) for i in heads}
        for i in heads:
            kd = k[i // rep] * jnp.exp(G_last[i] - G_col[i])
            s_ref[i] = jnp.exp(G_last[i]) * S[i] + lax.dot_general(kd.astype(BF16), U16[i], TN_DIMS,
                                                                   preferred_element_type=F32)
        for i in heads:
            z = z_ref[:, i * dv:(i + 1) * dv].astype(F32)
            o_ref[:, i * dv:(i + 1) * dv] = (_rms_rows(o[i], nw_ref[...])
                                             * (z * _sigmoid(z))).astype(o_ref.dtype)

    n_heads = hq * rep
    for h0 in range(0, n_heads, GDN_HEAD_GROUP):
        head_group(range(h0, min(h0 + GDN_HEAD_GROUP, n_heads)))

    @pl.when(n == n_chunks - 1)
    def _():
        sout_ref[0] = s_ref[...]


def gdn_chunks(proj, conv_w_all, buf_all, g, beta, norm_w, s0_all, layer, n_layers, s_prev, *, row0,
               nseq, T, C, hv, hqk, dk, dv, out_rows):
    N = T // C
    cw = conv_w_all.shape[1]
    conv_dim = 2 * hqk * dk + hv * dv
    assert conv_dim % (hv * dv) == 0
    has_s0 = s0_all is not None
    rb0 = row0 // C
    g_rows = g[row0:row0 + nseq * T].reshape(nseq, N, C, hv).transpose(0, 1, 3, 2)
    tok = pl.BlockSpec((C, hv), lambda b, h, n: (rb0 + b * N + n, 0))
    in_specs = [pl.BlockSpec((C, conv_dim), lambda b, h, n: (rb0 + b * N + n, 0)),
                pl.BlockSpec((C, conv_dim), lambda b, h, n: (rb0 + b * N + jnp.maximum(n - 1, 0), 0)),
                pl.BlockSpec((None, cw, conv_dim), lambda b, h, n: (layer, 0, 0)),
                pl.BlockSpec((C, hv * dv), lambda b, h, n: (rb0 + b * N + n, conv_dim // (hv * dv))),
                pl.BlockSpec((1, N, hv, C), lambda b, h, n: (b, 0, 0, 0)),
                tok, tok,
                pl.BlockSpec((1, dv), lambda b, h, n: (0, 0))]
    args = [proj, proj, conv_w_all, proj, g_rows, g, beta, norm_w.reshape(1, dv)]
    if has_s0:
        in_specs += [pl.BlockSpec((None, 1, cw - 1, conv_dim), lambda b, h, n: (layer, b, 0, 0)),
                     pl.BlockSpec((None, 1, hv, dk, dv), lambda b, h, n: (layer, b, 0, 0, 0))]
        args += [buf_all, s0_all]
    aliases = {}
    if s_prev is not None:
        aliases[len(args)] = 1
        in_specs.append(pl.BlockSpec(memory_space=pl.ANY))
        args.append(s_prev)
    return pl.pallas_call(
        functools.partial(_gdn_chunk_kernel, C=C, hq=hqk, rep=hv // hqk, dk=dk, dv=dv, n_chunks=N,
                          has_s0=has_s0, has_prev=s_prev is not None),
        grid=(nseq, 1, N),
        in_specs=in_specs,
        out_specs=[pl.BlockSpec((C, hv * dv), lambda b, h, n: (b * N + n, 0)),
                   pl.BlockSpec((None, 1, hv, dk, dv), lambda b, h, n: (layer, b, 0, 0, 0))],
        out_shape=[jax.ShapeDtypeStruct((out_rows, hv * dv), BF16),
                   jax.ShapeDtypeStruct((n_layers, nseq, hv, dk, dv), F32)],
        input_output_aliases=aliases,
        scratch_shapes=[pltpu.VMEM((hv, dk, dv), F32)],
        compiler_params=_params("parallel", "parallel", "arbitrary"),
        name="gdn_chunks",
    )(*args)


def _cumsum_kernel(x_ref, o_ref, carry_ref, *, tb):
    @pl.when(pl.program_id(0) == 0)
    def _():
        carry_ref[...] = jnp.zeros_like(carry_ref)

    row = lax.broadcasted_iota(jnp.int32, (tb, tb), 0)
    col = lax.broadcasted_iota(jnp.int32, (tb, tb), 1)
    tril = jnp.where(row >= col, 1.0, 0.0).astype(F32)
    c = jnp.dot(tril, x_ref[...], precision=HIGHEST, preferred_element_type=F32) + carry_ref[...]
    o_ref[...] = c
    carry_ref[...] = c[tb - 1:tb, :]


def cumsum_time(x):
    nseq, L, H = x.shape
    S = nseq * H
    tb = _pick(L, TIME_BLOCK, SUBLANES)
    out = pl.pallas_call(
        functools.partial(_cumsum_kernel, tb=tb),
        grid=(L // tb,),
        in_specs=[pl.BlockSpec((tb, S), lambda t: (t, 0))],
        out_specs=pl.BlockSpec((tb, S), lambda t: (t, 0)),
        out_shape=jax.ShapeDtypeStruct((L, S), F32),
        scratch_shapes=[pltpu.VMEM((1, S), F32)],
        compiler_params=_params("arbitrary"),
        name="fox_cumsum",
    )(x.transpose(1, 0, 2).reshape(L, S))
    return out.reshape(L, nseq, H)


def _fox_prompt_kernel(q_ref, k_ref, v_ref, og_ref, cq_ref, ckt_ref, o_ref, acc_ref, m_ref, cqr_ref, q2_ref,
                       *, bq, bk, H, dh, scale, hg):
    i = pl.program_id(1)
    j = pl.program_id(2)
    rep = bk // dh
    r = bk // bq
    j_last = i // r

    @pl.when(j == 0)
    def _():
        m_ref[...] = jnp.full(m_ref.shape, -jnp.inf, F32)
        acc_ref[...] = jnp.zeros_like(acc_ref)
        cqs = cq_ref[...] * LOG2E
        for h in range(H):
            cqr_ref[h] = jnp.broadcast_to(cqs[:, h:h + 1], (bq, dh))
        q2_ref[...] = (q_ref[...].astype(F32) * (scale * LOG2E)).astype(BF16)

    def block(masked):
        if masked:
            mask = (lax.broadcasted_iota(jnp.int32, (bq, bk), 1)
                    <= lax.broadcasted_iota(jnp.int32, (bq, bk), 0) + (i - r * j_last) * bq)
        ones16 = jnp.ones((bk, dh), BF16)
        ck2 = ckt_ref[...] * LOG2E

        def qk(h):
            hs = slice(h * dh, (h + 1) * dh)
            return lax.dot_general(q2_ref[:, hs], k_ref[:, hs], NT_DIMS, preferred_element_type=F32)

        groups = [range(g * hg, (g + 1) * hg) for g in range(H // hg)]
        s = {h: qk(h) for h in groups[0]}
        for gi, grp in enumerate(groups):
            if gi + 1 < len(groups):
                for h in groups[gi + 1]:
                    s[h] = qk(h)
            p16, alpha2 = {}, {}
            for h in grp:
                t = s.pop(h) - ck2[h:h + 1, :]
                if masked:
                    t = jnp.where(mask, t, -jnp.inf)
                cq2 = cqr_ref[h]
                m_prev = m_ref[h]
                m_new = jnp.maximum(m_prev, jnp.max(t, axis=1, keepdims=True) + cq2)
                m_ref[h] = m_new
                alpha = jnp.exp2(m_prev - m_new)
                c = cq2 - m_new
                p16[h] = jnp.exp2(t + jnp.concatenate([c] * rep, axis=1)).astype(BF16)
                alpha2[h] = jnp.concatenate([alpha, alpha], axis=1)
            for h in grp:
                hs = slice(h * dh, (h + 1) * dh)
                v_ext = jnp.concatenate([v_ref[:, hs], ones16], axis=1)
                acc_ref[h] = alpha2[h] * acc_ref[h] + jnp.dot(p16[h], v_ext, preferred_element_type=F32)

    @pl.when(j < j_last)
    def _():
        block(False)

    @pl.when(j == j_last)
    def _():
        block(True)
        for h in range(H):
            hs = slice(h * dh, (h + 1) * dh)
            a = acc_ref[h]
            o_ref[:, hs] = ((a[:, :dh] / a[:, dh:]) * _sigmoid(og_ref[:, hs].astype(F32))).astype(o_ref.dtype)


def fox_prompt_attention(proj, cum, cum_t, *, nseq, T, H, dh, out_rows):
    bq = _pick(T, TIME_BLOCK, LANES)
    bk = _pick(T, FOX_KEY_BLOCKS * bq, bq)
    nq, nk, r = T // bq, T // bk, bk // bq
    D = H * dh
    hg = _pick(H, FOX_HEAD_GROUP, 1)
    kv = lambda col: (lambda b, i, j: (b * nk + jnp.minimum(j, i // r), col))
    return pl.pallas_call(
        functools.partial(_fox_prompt_kernel, bq=bq, bk=bk, H=H, dh=dh, scale=dh ** -0.5, hg=hg),
        grid=(nseq, nq, nk),
        in_specs=[pl.BlockSpec((bq, D), lambda b, i, j: (b * nq + i, 0)),
                  pl.BlockSpec((bk, D), kv(1)),
                  pl.BlockSpec((bk, D), kv(2)),
                  pl.BlockSpec((bq, D), lambda b, i, j: (b * nq + i, 3)),
                  pl.BlockSpec((bq, H), lambda b, i, j: (b * nq + i, 0)),
                  pl.BlockSpec((H, bk), lambda b, i, j: (0, b * nk + jnp.minimum(j, i // r)))],
        out_specs=pl.BlockSpec((bq, D), lambda b, i, j: (b * nq + i, 0)),
        out_shape=jax.ShapeDtypeStruct((out_rows, D), BF16),
        scratch_shapes=[pltpu.VMEM((H, bq, 2 * dh), F32), pltpu.VMEM((H, bq, dh), F32),
                        pltpu.VMEM((H, bq, dh), F32), pltpu.VMEM((bq, D), BF16)],
        compiler_params=_params("parallel", "parallel", "arbitrary"),
        name="fox_prompt_attention",
    )(proj, proj, proj, proj, cum, cum_t)


def _fox_sample_kernel(q_ref, kn_ref, vn_ref, og_ref, kc_ref, vc_ref, cq_ref, ck_ref, o_ref,
                       *, P, T, G, dh, scale):
    R, PG = T * G, P * G
    gs = G.bit_length() - 1
    q = (q_ref[0, 0].astype(F32) * scale).astype(BF16)
    kc = kc_ref[0].reshape(PG, dh).astype(BF16)
    vc = vc_ref[0].reshape(PG, dh).astype(BF16)
    s_p = lax.dot_general(q, kc, NT_DIMS, preferred_element_type=F32)
    s_n = lax.dot_general(q, kn_ref[0, 0], NT_DIMS, preferred_element_type=F32)
    cq = cq_ref[0, 0]
    ck = ck_ref[0, 0]
    same_p = ((lax.broadcasted_iota(jnp.int32, (R, PG), 0) & (G - 1))
              == (lax.broadcasted_iota(jnp.int32, (R, PG), 1) & (G - 1)))
    rn = lax.broadcasted_iota(jnp.int32, (R, R), 0)
    cn = lax.broadcasted_iota(jnp.int32, (R, R), 1)
    same_n = jnp.logical_and((rn & (G - 1)) == (cn & (G - 1)), (cn >> gs) <= (rn >> gs))
    s_p = jnp.where(same_p, s_p + (cq - ck[:, :PG]), -jnp.inf)
    s_n = jnp.where(same_n, s_n + (cq - ck[:, PG:]), -jnp.inf)
    m = jnp.maximum(jnp.max(s_p, axis=1, keepdims=True), jnp.max(s_n, axis=1, keepdims=True))
    p_p = jnp.exp(s_p - m)
    p_n = jnp.exp(s_n - m)
    l = jnp.sum(p_p, axis=1, keepdims=True) + jnp.sum(p_n, axis=1, keepdims=True)
    o = (jnp.dot(p_p.astype(BF16), vc, preferred_element_type=F32)
         + jnp.dot(p_n.astype(BF16), vn_ref[0, 0], preferred_element_type=F32))
    o_ref[0, 0] = ((o / l) * _sigmoid(og_ref[0, 0].astype(F32))).astype(o_ref.dtype)


def fox_sample_attention(proj, k_cache_all, v_cache_all, layer, cum, *, row0, nseq, T, P, H, dh):
    G = SUBLANES if H % SUBLANES == 0 else H
    assert G & (G - 1) == 0
    NG = H // G
    R = T * G
    new = proj[row0:row0 + nseq * T].reshape(nseq, T, 4, NG, G, dh).transpose(2, 0, 3, 1, 4, 5)
    new = new.reshape(4, nseq, NG, R, dh)
    ck = cum.reshape(nseq, P + T, NG, G).transpose(0, 2, 1, 3).reshape(nseq, NG, 1, (P + T) * G)
    cq = cum[:, P:].reshape(nseq, T, NG, G).transpose(0, 2, 1, 3).reshape(nseq, NG, R, 1)
    part = lambda a: pl.BlockSpec((None, 1, 1, R, dh), lambda b, g: (a, b, g, 0, 0))
    cache = pl.BlockSpec((None, 1, P, G, dh), lambda b, g: (layer, b, 0, g, 0))
    out = pl.pallas_call(
        functools.partial(_fox_sample_kernel, P=P, T=T, G=G, dh=dh, scale=dh ** -0.5),
        grid=(nseq, NG),
        in_specs=[part(0), part(1), part(2), part(3), cache, cache,
                  pl.BlockSpec((1, 1, R, 1), lambda b, g: (b, g, 0, 0)),
                  pl.BlockSpec((1, 1, 1, (P + T) * G), lambda b, g: (b, g, 0, 0))],
        out_specs=pl.BlockSpec((1, 1, R, dh), lambda b, g: (b, g, 0, 0)),
        out_shape=jax.ShapeDtypeStruct((nseq, NG, R, dh), BF16),
        compiler_params=_params("parallel", "parallel"),
        name="fox_sample_attention",
    )(new, new, new, new, k_cache_all, v_cache_all, cq, ck)
    return out.reshape(nseq, NG, T, G, dh).transpose(0, 2, 1, 3, 4).reshape(nseq * T, H * dh)


def kernel(x_prompt, x_sample, state_gdn, state_gdn_conv, cache_fox_k, cache_fox_v, cache_fox_logf,
           norm_mix, norm_mlp, norm_final, gdn_w_in, gdn_conv_w, gdn_a_log, gdn_dt_bias, gdn_norm_w,
           gdn_w_out, fox_w_in, fox_b_f, fox_w_out, mlp_w_up, mlp_w_down):
    Bp, Tp, D = x_prompt.shape
    Bs, Ts, _ = x_sample.shape
    depth = norm_mix.shape[0]
    _, _, hv, dk, dv = state_gdn.shape
    conv_dim = state_gdn_conv.shape[-1]
    cw1 = state_gdn_conv.shape[2]
    v_dim = hv * dv
    qk_dim = (conv_dim - v_dim) // 2
    hqk = qk_dim // dk
    _, _, P, H, dh = cache_fox_k.shape
    fox_dim = H * dh
    d_ff = mlp_w_up.shape[-1]
    Mp, Ms = Bp * Tp, Bs * Ts
    Cp = min(GDN_CHUNK, Tp)

    x = jnp.concatenate([x_prompt.reshape(Mp, D), x_sample.reshape(Ms, D)], axis=0)
    gdn_w_in_t = jnp.swapaxes(gdn_w_in, 1, 2)
    fox_w_in_t = jnp.swapaxes(fox_w_in, 1, 2)
    S_p, S_s, buf_p, buf_s = None, None, [], []
    n_gdn = state_gdn.shape[0]
    k_p, k_s, v_p, v_s, lf_p, lf_s = [], [], [], [], [], []

    for i in range(depth):
        j = i // 2
        if i % 2 == 0:
            n_main = conv_dim + v_dim
            proj, beta, g = norm_proj(x, norm_mix[i], gdn_w_in_t, j, n_main, w_t=True, gate="gdn",
                                      gate_params=(gdn_a_log[j], gdn_dt_bias[j]), name="gdn_in_proj")

            common = dict(hv=hv, hqk=hqk, dk=dk, dv=dv)
            o, S_p = gdn_chunks(proj, gdn_conv_w, None, g, beta, gdn_norm_w[j], None, j, n_gdn, S_p,
                                row0=0, nseq=Bp, T=Tp, C=Cp, out_rows=Mp + Ms, **common)
            o_s, S_s = gdn_chunks(proj, gdn_conv_w, state_gdn_conv, g, beta, gdn_norm_w[j], state_gdn, j,
                                  n_gdn, S_s, row0=Mp, nseq=Bs, T=Ts, C=Ts, out_rows=Ms, **common)
            o = lax.dynamic_update_slice(o, o_s, (Mp, 0))
            x = matmul_res(o, gdn_w_out, j, x, name="gdn_out_proj")
            buf_p.append(jnp.stack([proj[(b + 1) * Tp - cw1:(b + 1) * Tp, :conv_dim]
                                    for b in range(Bp)]).astype(F32))
            qkv_s = proj[Mp:, :conv_dim].reshape(Bs, Ts, conv_dim).astype(F32)
            buf_s.append(jnp.concatenate([state_gdn_conv[j], qkv_s], axis=1)[:, -cw1:])
        else:
            proj, logf = norm_proj(x, norm_mix[i], fox_w_in_t, j, 4 * fox_dim, w_t=True, gate="fox",
                                   gate_params=(fox_b_f[j],), name="fox_in_proj")
            cum_p = cumsum_time(logf[:Mp].reshape(Bp, Tp, H))
            o = fox_prompt_attention(proj, cum_p.transpose(1, 0, 2).reshape(Mp, H),
                                     cum_p.transpose(2, 1, 0).reshape(H, Mp), nseq=Bp, T=Tp, H=H, dh=dh,
                                     out_rows=Mp + Ms)
            lf_new = logf[Mp:].reshape(Bs, Ts, H)
            lf_all = jnp.concatenate([cache_fox_logf[j], lf_new], axis=1)
            cum_s = cumsum_time(lf_all).transpose(1, 0, 2)
            o_s = fox_sample_attention(proj, cache_fox_k, cache_fox_v, j, cum_s,
                                       row0=Mp, nseq=Bs, T=Ts, P=P, H=H, dh=dh)
            o = lax.dynamic_update_slice(o, o_s, (Mp, 0))
            x = matmul_res(o, fox_w_out, j, x, name="fox_out_proj")
            kv = proj[:, fox_dim:3 * fox_dim].reshape(Mp + Ms, 2, H, dh)
            k_p.append(kv[:Mp, 0].astype(F32).reshape(Bp, Tp, H, dh))
            k_s.append(kv[Mp:, 0].astype(F32).reshape(Bs, Ts, H, dh))
            v_p.append(kv[:Mp, 1].astype(F32).reshape(Bp, Tp, H, dh))
            v_s.append(kv[Mp:, 1].astype(F32).reshape(Bs, Ts, H, dh))
            lf_p.append(logf[:Mp].reshape(Bp, Tp, H))
            lf_s.append(lf_new)
        hmid = norm_proj(x, norm_mlp[i], mlp_w_up, i, d_ff, relu2=True, name="mlp_up")
        x = matmul_res(hmid, mlp_w_down, i, x, name="mlp_down")

    y_p = rmsnorm_rows(x, norm_final, 0, Mp)
    y_s = rmsnorm_rows(x, norm_final, Mp, Ms)
    st = jnp.stack
    return (y_p.reshape(Bp, Tp, D), y_s.reshape(Bs, Ts, D),
            S_p, st(buf_p), st(k_p), st(v_p), st(lf_p),
            S_s, st(buf_s), st(k_s), st(v_s), st(lf_s))
```

```python
import functools

import jax
import jax.numpy as jnp
from jax import lax
from jax.experimental import pallas as pl
from jax.experimental.pallas import tpu as pltpu

F32 = jnp.float32
BF16 = jnp.bfloat16
EPS = 1e-6
GDN_CHUNK = 64
FOX_KEY_BLOCKS = 1
GDN_HEAD_GROUP = 32
CONV_HALO_ROWS = 16
INV_BASE_BLOCK = 8
VMEM_LIMIT_BYTES = 60 * 1024 * 1024
LOG2E = 1.4426950408889634
SUBLANES = 8
BF16_ROWS = 16
LANES = 128
ROW_TILE = 1088
IN_COL_TILE = 1024
OUT_COL_TILE = 512
K_TILE = 4096
NORM_CHUNK_ROWS = 272
NORM_ROW_TILE = 512
TIME_BLOCK = 256
FOX_HEAD_GROUP = 2
HIGHEST = lax.Precision.HIGHEST
NT_DIMS = (((1,), (1,)), ((), ()))
TN_DIMS = (((0,), (0,)), ((), ()))


def _pick(n, pref, align):
    best = None
    for d in range(align, min(n, pref) + 1, align):
        if n % d == 0:
            best = d
    return n if best is None else best


def _params(*sem):
    return pltpu.CompilerParams(dimension_semantics=sem, vmem_limit_bytes=VMEM_LIMIT_BYTES)


def _sigmoid(x):
    return 1.0 / (1.0 + jnp.exp(-x))


def _softplus(x):
    return jnp.maximum(x, 0.0) + jnp.log1p(jnp.exp(-jnp.abs(x)))


def _rms_rows(x, w):
    ms = jnp.mean(x * x, axis=-1, keepdims=True)
    return x * lax.rsqrt(ms + EPS) * w


def _wdot(xn, w_ref, w_t):
    w = w_ref[...].astype(BF16)
    if w_t:
        return lax.dot_general(xn, w, NT_DIMS, preferred_element_type=F32)
    return jnp.dot(xn, w, preferred_element_type=F32)


def _gdn_gate_outputs(acc, alog_ref, dtb_ref, beta_ref, g_ref):
    hv = beta_ref.shape[1]
    beta_ref[...] = _sigmoid(acc[:, :hv])
    g_ref[...] = -jnp.exp(alog_ref[...]) * _softplus(acc[:, hv:2 * hv] + dtb_ref[...])


def _fox_gate_outputs(acc, bf_ref, lf_ref):
    f = acc[:, :lf_ref.shape[1]] + bf_ref[...]
    lf_ref[...] = -_softplus(-f)


GATES = {"gdn": (_gdn_gate_outputs, 2, 2), "fox": (_fox_gate_outputs, 1, 1)}


def _norm_proj_kernel(x_ref, nw_ref, w_ref, *rest, relu2, w_t, gate):
    if gate is None:
        o_ref, xn_ref = rest
    else:
        epilogue, n_par, n_out = GATES[gate]
        gw_ref, par_refs = rest[0], rest[1:1 + n_par]
        o_ref, gate_refs, xn_ref = rest[1 + n_par], rest[2 + n_par:2 + n_par + n_out], rest[-1]

    def tile(xn, w16):
        if w_t:
            acc = lax.dot_general(xn, w16, NT_DIMS, preferred_element_type=F32)
        else:
            acc = jnp.dot(xn, w16, preferred_element_type=F32)
        if relu2:
            acc = jnp.square(jnp.maximum(acc, 0.0))
        return acc.astype(o_ref.dtype)

    j = pl.program_id(1)

    @pl.when(j == 0)
    def _():
        tm = x_ref.shape[0]
        rows = _pick(tm, NORM_CHUNK_ROWS, BF16_ROWS)
        w16 = w_ref[...].astype(BF16)
        for r in range(tm // rows):
            sl = slice(r * rows, (r + 1) * rows)
            xn = _rms_rows(x_ref[sl, :], nw_ref[...]).astype(BF16)
            xn_ref[sl, :] = xn
            o_ref[sl, :] = tile(xn, w16)
        if gate is not None:
            epilogue(_wdot(xn_ref[...], gw_ref, True), *par_refs, *gate_refs)

    @pl.when(j > 0)
    def _():
        o_ref[...] = tile(xn_ref[...], w_ref[...].astype(BF16))


def norm_proj(x, nw, w_all, layer, n_cols, *, relu2=False, w_t=False, gate=None, gate_params=(),
              name="norm_proj"):
    M, K = x.shape
    tm = _pick(M, ROW_TILE, BF16_ROWS)
    tn = _pick(n_cols, IN_COL_TILE, LANES)
    if w_t:
        w_spec = pl.BlockSpec((None, tn, K), lambda i, j: (layer, j, 0))
    else:
        w_spec = pl.BlockSpec((None, K, tn), lambda i, j: (layer, 0, j))
    in_specs = [pl.BlockSpec((tm, K), lambda i, j: (i, 0)), pl.BlockSpec((1, K), lambda i, j: (0, 0)), w_spec]
    args = [x, nw.reshape(1, K), w_all]
    out_specs = [pl.BlockSpec((tm, tn), lambda i, j: (i, j))]
    out_shape = [jax.ShapeDtypeStruct((M, n_cols), BF16)]
    if gate is not None:
        assert w_t and n_cols % LANES == 0
        H = gate_params[0].shape[0]
        assert GATES[gate][2] * H <= LANES
        in_specs.append(pl.BlockSpec((None, LANES, K), lambda i, j: (layer, n_cols // LANES, 0)))
        args.append(w_all)
        for p in gate_params:
            in_specs.append(pl.BlockSpec((1, H), lambda i, j: (0, 0)))
            args.append(p.reshape(1, H))
        for _ in range(GATES[gate][2]):
            out_specs.append(pl.BlockSpec((tm, H), lambda i, j: (i, 0)))
            out_shape.append(jax.ShapeDtypeStruct((M, H), F32))
    out = pl.pallas_call(
        functools.partial(_norm_proj_kernel, relu2=relu2, w_t=w_t, gate=gate),
        grid=(M // tm, n_cols // tn),
        in_specs=in_specs,
        out_specs=out_specs,
        out_shape=out_shape,
        scratch_shapes=[pltpu.VMEM((tm, K), BF16)],
        compiler_params=_params("parallel", "arbitrary"),
        name=name,
    )(*args)
    return out[0] if gate is None else out


def _matmul_res_kernel(a_ref, w_ref, r_ref, o_ref, *acc, nk):
    part = jnp.dot(a_ref[...], w_ref[...].astype(BF16), preferred_element_type=F32)
    if nk == 1:
        o_ref[...] = r_ref[...] + part
        return
    (acc_ref,) = acc
    k = pl.program_id(1)
    j = pl.program_id(2)

    @pl.when(k == 0)
    def _():
        acc_ref[j] = part

    if nk > 2:
        @pl.when(jnp.logical_and(k > 0, k < nk - 1))
        def _():
            acc_ref[j] += part

    @pl.when(k == nk - 1)
    def _():
        o_ref[...] = r_ref[...] + (acc_ref[j] + part)


def matmul_res(a, w_all, layer, res, *, name="matmul_res"):
    M, K = a.shape
    N = w_all.shape[-1]
    tm = _pick(M, ROW_TILE, BF16_ROWS)
    tn = _pick(N, OUT_COL_TILE, LANES)
    tk = _pick(K, K_TILE, LANES)
    nk = K // tk
    out_idx = lambda i, k, j: (i, jnp.where(k == nk - 1, j, 0))
    return pl.pallas_call(
        functools.partial(_matmul_res_kernel, nk=nk),
        grid=(M // tm, nk, N // tn),
        in_specs=[pl.BlockSpec((tm, tk), lambda i, k, j: (i, k)),
                  pl.BlockSpec((None, tk, tn), lambda i, k, j: (layer, k, j)),
                  pl.BlockSpec((tm, tn), out_idx)],
        out_specs=pl.BlockSpec((tm, tn), out_idx),
        out_shape=jax.ShapeDtypeStruct((M, N), F32),
        scratch_shapes=[pltpu.VMEM((N // tn, tm, tn), F32)] if nk > 1 else [],
        compiler_params=_params("parallel", "arbitrary", "arbitrary"),
        name=name,
    )(a, w_all, res)


def _rmsnorm_kernel(x_ref, nw_ref, o_ref):
    o_ref[...] = _rms_rows(x_ref[...], nw_ref[...])


def rmsnorm_rows(x, nw, row0, n_rows):
    K = x.shape[1]
    tm = _pick(n_rows, NORM_ROW_TILE, SUBLANES)
    rb0 = row0 // tm
    return pl.pallas_call(
        _rmsnorm_kernel,
        grid=(n_rows // tm,),
        in_specs=[pl.BlockSpec((tm, K), lambda i: (rb0 + i, 0)), pl.BlockSpec((1, K), lambda i: (0, 0))],
        out_specs=pl.BlockSpec((tm, K), lambda i: (i, 0)),
        out_shape=jax.ShapeDtypeStruct((n_rows, K), F32),
        compiler_params=_params("parallel"),
        name="final_rmsnorm",
    )(x, nw.reshape(1, K))


def _gdn_chunk_kernel(x_ref, xp_ref, cw_ref, z_ref, g_ref, gt_ref, bt_ref, nw_ref, *rest,
                      C, hq, rep, dk, dv, n_chunks, has_s0, has_prev):
    rest = list(rest)
    buf_ref = rest.pop(0) if has_s0 else None
    s0_ref = rest.pop(0) if has_s0 else None
    if has_prev:
        rest.pop(0)
    o_ref, sout_ref, s_ref = rest
    n = pl.program_id(2)
    C2 = 2 * C
    cw = cw_ref.shape[0]
    qk_dim = hq * dk
    pair = 2 * dk
    assert dv == dk and cw - 1 <= CONV_HALO_ROWS <= C

    @pl.when(n == 0)
    def _():
        if has_s0:
            s_ref[...] = s0_ref[0]
        else:
            s_ref[...] = jnp.zeros_like(s_ref)

    K = CONV_HALO_ROWS + C
    sr = lax.broadcasted_iota(jnp.int32, ((cw - 1) * C, K), 0)
    sc = lax.broadcasted_iota(jnp.int32, ((cw - 1) * C, K), 1)
    cshift = C.bit_length() - 1
    sel = jnp.where(sc == CONV_HALO_ROWS + (sr & (C - 1)) - ((sr >> cshift) + 1), 1.0, 0.0).astype(BF16)
    first = n == 0
    trow = lax.broadcasted_iota(jnp.int32, (C, pair), 0)
    conv_cache = {}

    def conv_pair(p):
        if p in conv_cache:
            return conv_cache[p]
        cs = slice(p * pair, (p + 1) * pair)
        x_cur = x_ref[:, cs]
        x_prev = xp_ref[C - CONV_HALO_ROWS:C, cs]
        x_prev = jnp.where(first, jnp.zeros_like(x_prev), x_prev)
        sh = jnp.dot(sel, jnp.concatenate([x_prev, x_cur], axis=0), preferred_element_type=F32)
        acc = x_cur.astype(F32) * cw_ref[cw - 1:cw, cs]
        for s in range(1, cw):
            acc = acc + sh[(s - 1) * C:s * C] * cw_ref[cw - 1 - s:cw - s, cs]
        if has_s0:
            corr = jnp.zeros((C, pair), F32)
            for t in range(cw - 1):
                r = sum(cw_ref[cw - 1 - s:cw - s, cs] * buf_ref[0, cw - 1 - (s - t):cw - (s - t), cs]
                        for s in range(t + 1, cw))
                corr = jnp.where(trow == t, r, corr)
            acc = acc + jnp.where(first, corr, 0.0)
        y = acc * _sigmoid(acc)
        conv_cache[p] = y
        return y

    def conv(c0, l2_scale=None):
        y = conv_pair(c0 // pair)[:, (c0 % pair):(c0 % pair) + dk]
        if l2_scale is None:
            return y
        return y * (lax.rsqrt(jnp.sum(y * y, axis=-1, keepdims=True) + EPS) * l2_scale)

    row = lax.broadcasted_iota(jnp.int32, (C, C2), 0)
    lane = lax.broadcasted_iota(jnp.int32, (C, C2), 1)
    right = lane >= C
    col = jnp.where(right, lane - C, lane)
    incl = row >= col
    strict = row > col
    eye_right = jnp.where(lane == row + C, 1.0, 0.0).astype(F32)
    triu2 = jnp.where(row <= col, 1.0, 0.0).astype(F32)
    tril = jnp.where(lax.broadcasted_iota(jnp.int32, (C, C), 0) >= lax.broadcasted_iota(jnp.int32, (C, C), 1),
                     1.0, 0.0).astype(F32)
    zeros_z = jnp.zeros((C, C2), BF16)
    zeros_r = jnp.zeros((C, dk + dv), BF16)
    G2_all = jnp.dot(g_ref[0, n], triu2, precision=HIGHEST, preferred_element_type=F32)
    Gc_all = jnp.dot(tril, gt_ref[...], precision=HIGHEST, preferred_element_type=F32)
    beta_all = bt_ref[...]

    def head_group(heads):
        qa = sorted({i // rep for i in heads})
        q = {a: conv(a * dk, dk ** -0.5) for a in qa}
        k = {a: conv(qk_dim + a * dk, 1.0) for a in qa}
        q16 = {a: q[a].astype(BF16) for a in qa}
        kk16 = {a: jnp.concatenate([k[a].astype(BF16)] * 2, axis=0) for a in qa}
        G2, G_col, b_col, dec_incl, gamma, G_last, kb, vb = {}, {}, {}, {}, {}, {}, {}, {}
        for i in heads:
            G2[i] = G2_all[i:i + 1, :]
            G_col[i] = Gc_all[:, i:i + 1]
            b_col[i] = beta_all[:, i:i + 1]
            dec_incl[i] = jnp.exp(jnp.where(incl, G_col[i] - G2[i], -jnp.inf))
            gamma[i] = jnp.exp(G_col[i])
            G_last[i] = G2[i][:, C - 1:C]
            kb[i] = k[i // rep] * b_col[i]
            vb[i] = conv(2 * qk_dim + i * dv) * b_col[i]
        gram = {i: lax.dot_general(jnp.concatenate([kb[i].astype(BF16), q16[i // rep]], axis=0), kk16[i // rep],
                                   NT_DIMS, preferred_element_type=F32) for i in heads}
        Lf = {i: gram[i][:C] * dec_incl[i] for i in heads}
        base = min(INV_BASE_BLOCK, C)
        sh = base.bit_length() - 1
        in_base = jnp.logical_and(strict, (row >> sh) == (col >> sh))
        Z = {i: jnp.where(right, eye_right, -jnp.where(in_base, Lf[i], 0.0)) for i in heads}
        for _ in range((base - 1).bit_length()):
            Z16 = {i: Z[i].astype(BF16) for i in heads}
            Z = {i: jnp.dot(Z16[i], jnp.concatenate([Z16[i], zeros_z], axis=0), preferred_element_type=F32)
                 + jnp.where(right, Z[i], 0.0) for i in heads}
        b = base
        while b < C:
            sh = b.bit_length() - 1
            lower_left = jnp.logical_and(jnp.logical_and(right, (row >> (sh + 1)) == (col >> (sh + 1))),
                                         jnp.logical_and(((row >> sh) & 1) == 1, ((col >> sh) & 1) == 0))
            Z16 = {i: Z[i].astype(BF16) for i in heads}
            W16 = {i: jnp.dot(jnp.where(lower_left, Lf[i], 0.0).astype(BF16),
                              jnp.concatenate([zeros_z, Z16[i]], axis=0),
                              preferred_element_type=F32).astype(BF16) for i in heads}
            Z = {i: Z[i] - jnp.dot(Z16[i], jnp.concatenate([zeros_z, W16[i]], axis=0),
                                   preferred_element_type=F32) for i in heads}
            b *= 2
        wu = {}
        for i in heads:
            rhs = jnp.concatenate([(kb[i] * gamma[i]).astype(BF16), vb[i].astype(BF16)], axis=1)
            wu[i] = jnp.dot(Z[i].astype(BF16), jnp.concatenate([zeros_r, rhs], axis=0),
                            preferred_element_type=F32)
        S = {i: s_ref[i] for i in heads}
        S16 = {i: S[i].astype(BF16) for i in heads}
        ws_qs = {i: jnp.dot(jnp.concatenate([wu[i][:, :dk].astype(BF16),
                                             (q[i // rep] * gamma[i]).astype(BF16)], axis=0),
                            S16[i], preferred_element_type=F32) for i in heads}
        U16 = {i: (wu[i][:, dk:] - ws_qs[i][:C]).astype(BF16) for i in heads}
        A16 = {i: (gram[i][C:] * dec_incl[i])[:, :C].astype(BF16) for i in heads}
        o = {i: ws_qs[i][C:] + jnp.dot(A16[i], U16[i], preferred_element_type=F32) for i in heads}
        for i in heads:
            kd = k[i // rep] * jnp.exp(G_last[i] - G_col[i])
            s_ref[i] = jnp.exp(G_last[i]) * S[i] + lax.dot_general(kd.astype(BF16), U16[i], TN_DIMS,
                                                                   preferred_element_type=F32)
        for i in heads:
            z = z_ref[:, i * dv:(i + 1) * dv].astype(F32)
            o_ref[:, i * dv:(i + 1) * dv] = (_rms_rows(o[i], nw_ref[...])
                                             * (z * _sigmoid(z))).astype(o_ref.dtype)

    n_heads = hq * rep
    for h0 in range(0, n_heads, GDN_HEAD_GROUP):
        head_group(range(h0, min(h0 + GDN_HEAD_GROUP, n_heads)))

    @pl.when(n == n_chunks - 1)
    def _():
        sout_ref[0] = s_ref[...]


def gdn_chunks(proj, conv_w_all, buf_all, g, beta, norm_w, s0_all, layer, n_layers, s_prev, *, row0,
               nseq, T, C, hv, hqk, dk, dv, out_rows):
    N = T // C
    cw = conv_w_all.shape[1]
    conv_dim = 2 * hqk * dk + hv * dv
    assert conv_dim % (hv * dv) == 0
    has_s0 = s0_all is not None
    rb0 = row0 // C
    g_rows = g[row0:row0 + nseq * T].reshape(nseq, N, C, hv).transpose(0, 1, 3, 2)
    tok = pl.BlockSpec((C, hv), lambda b, h, n: (rb0 + b * N + n, 0))
    in_specs = [pl.BlockSpec((C, conv_dim), lambda b, h, n: (rb0 + b * N + n, 0)),
                pl.BlockSpec((C, conv_dim), lambda b, h, n: (rb0 + b * N + jnp.maximum(n - 1, 0), 0)),
                pl.BlockSpec((None, cw, conv_dim), lambda b, h, n: (layer, 0, 0)),
                pl.BlockSpec((C, hv * dv), lambda b, h, n: (rb0 + b * N + n, conv_dim // (hv * dv))),
                pl.BlockSpec((1, N, hv, C), lambda b, h, n: (b, 0, 0, 0)),
                tok, tok,
                pl.BlockSpec((1, dv), lambda b, h, n: (0, 0))]
    args = [proj, proj, conv_w_all, proj, g_rows, g, beta, norm_w.reshape(1, dv)]
    if has_s0:
        in_specs += [pl.BlockSpec((None, 1, cw - 1, conv_dim), lambda b, h, n: (layer, b, 0, 0)),
                     pl.BlockSpec((None, 1, hv, dk, dv), lambda b, h, n: (layer, b, 0, 0, 0))]
        args += [buf_all, s0_all]
    aliases = {}
    if s_prev is not None:
        aliases[len(args)] = 1
        in_specs.append(pl.BlockSpec(memory_space=pl.ANY))
        args.append(s_prev)
    return pl.pallas_call(
        functools.partial(_gdn_chunk_kernel, C=C, hq=hqk, rep=hv // hqk, dk=dk, dv=dv, n_chunks=N,
                          has_s0=has_s0, has_prev=s_prev is not None),
        grid=(nseq, 1, N),
        in_specs=in_specs,
        out_specs=[pl.BlockSpec((C, hv * dv), lambda b, h, n: (b * N + n, 0)),
                   pl.BlockSpec((None, 1, hv, dk, dv), lambda b, h, n: (layer, b, 0, 0, 0))],
        out_shape=[jax.ShapeDtypeStruct((out_rows, hv * dv), BF16),
                   jax.ShapeDtypeStruct((n_layers, nseq, hv, dk, dv), F32)],
        input_output_aliases=aliases,
        scratch_shapes=[pltpu.VMEM((hv, dk, dv), F32)],
        compiler_params=_params("parallel", "parallel", "arbitrary"),
        name="gdn_chunks",
    )(*args)


def _cumsum_kernel(x_ref, o_ref, carry_ref, *, tb):
    @pl.when(pl.program_id(0) == 0)
    def _():
        carry_ref[...] = jnp.zeros_like(carry_ref)

    row = lax.broadcasted_iota(jnp.int32, (tb, tb), 0)
    col = lax.broadcasted_iota(jnp.int32, (tb, tb), 1)
    tril = jnp.where(row >= col, 1.0, 0.0).astype(F32)
    c = jnp.dot(tril, x_ref[...], precision=HIGHEST, preferred_element_type=F32) + carry_ref[...]
    o_ref[...] = c
    carry_ref[...] = c[tb - 1:tb, :]


def cumsum_time(x):
    nseq, L, H = x.shape
    S = nseq * H
    tb = _pick(L, TIME_BLOCK, SUBLANES)
    out = pl.pallas_call(
        functools.partial(_cumsum_kernel, tb=tb),
        grid=(L // tb,),
        in_specs=[pl.BlockSpec((tb, S), lambda t: (t, 0))],
        out_specs=pl.BlockSpec((tb, S), lambda t: (t, 0)),
        out_shape=jax.ShapeDtypeStruct((L, S), F32),
        scratch_shapes=[pltpu.VMEM((1, S), F32)],
        compiler_params=_params("arbitrary"),
        name="fox_cumsum",
    )(x.transpose(1, 0, 2).reshape(L, S))
    return out.reshape(L, nseq, H)


def _fox_prompt_kernel(qi_ref, kj_ref, q_ref, k_ref, v_ref, og_ref, cq_ref, ckt_ref, o_ref,
                       acc_ref, m_ref, cqr_ref, q2_ref, *, bq, bk, H, dh, scale, hg):
    i = qi_ref[pl.program_id(1)]
    j = kj_ref[pl.program_id(1)]
    rep = bk // dh
    r = bk // bq
    j_last = i // r

    @pl.when(j == 0)
    def _():
        m_ref[...] = jnp.full(m_ref.shape, -jnp.inf, F32)
        acc_ref[...] = jnp.zeros_like(acc_ref)
        cqs = cq_ref[...] * LOG2E
        for h in range(H):
            cqr_ref[h] = jnp.broadcast_to(cqs[:, h:h + 1], (bq, dh))
        q2_ref[...] = (q_ref[...].astype(F32) * (scale * LOG2E)).astype(BF16)

    def block(masked):
        if masked:
            mask = (lax.broadcasted_iota(jnp.int32, (bq, bk), 1)
                    <= lax.broadcasted_iota(jnp.int32, (bq, bk), 0) + (i - r * j_last) * bq)
        ones16 = jnp.ones((bk, dh), BF16)
        ck2 = ckt_ref[...] * LOG2E

        def qk(h):
            hs = slice(h * dh, (h + 1) * dh)
            return lax.dot_general(q2_ref[:, hs], k_ref[:, hs], NT_DIMS, preferred_element_type=F32)

        groups = [range(g * hg, (g + 1) * hg) for g in range(H // hg)]
        s = {h: qk(h) for h in groups[0]}
        for gi, grp in enumerate(groups):
            if gi + 1 < len(groups):
                for h in groups[gi + 1]:
                    s[h] = qk(h)
            p16, alpha2 = {}, {}
            for h in grp:
                t = s.pop(h) - ck2[h:h + 1, :]
                if masked:
                    t = jnp.where(mask, t, -jnp.inf)
                cq2 = cqr_ref[h]
                m_prev = m_ref[h]
                m_new = jnp.maximum(m_prev, jnp.max(t, axis=1, keepdims=True) + cq2)
                m_ref[h] = m_new
                alpha = jnp.exp2(m_prev - m_new)
                c = cq2 - m_new
                p16[h] = jnp.exp2(t + jnp.concatenate([c] * rep, axis=1)).astype(BF16)
                alpha2[h] = jnp.concatenate([alpha, alpha], axis=1)
            for h in grp:
                hs = slice(h * dh, (h + 1) * dh)
                v_ext = jnp.concatenate([v_ref[:, hs], ones16], axis=1)
                acc_ref[h] = alpha2[h] * acc_ref[h] + jnp.dot(p16[h], v_ext, preferred_element_type=F32)

    @pl.when(j < j_last)
    def _():
        block(False)

    @pl.when(j == j_last)
    def _():
        block(True)
        for h in range(H):
            hs = slice(h * dh, (h + 1) * dh)
            a = acc_ref[h]
            o_ref[:, hs] = ((a[:, :dh] / a[:, dh:]) * _sigmoid(og_ref[:, hs].astype(F32))).astype(o_ref.dtype)


def fox_prompt_attention(proj, cum, cum_t, *, nseq, T, H, dh, out_rows):
    bq = _pick(T, TIME_BLOCK, LANES)
    bk = _pick(T, FOX_KEY_BLOCKS * bq, bq)
    nq, nk, r = T // bq, T // bk, bk // bq
    D = H * dh
    hg = _pick(H, FOX_HEAD_GROUP, 1)
    pairs = [(i, j) for i in range(nq) for j in range(i // r + 1)]
    qi = jnp.asarray([p[0] for p in pairs], jnp.int32)
    kj = jnp.asarray([p[1] for p in pairs], jnp.int32)
    qrow = lambda col: (lambda b, p, qi, kj: (b * nq + qi[p], col))
    krow = lambda col: (lambda b, p, qi, kj: (b * nk + kj[p], col))
    return pl.pallas_call(
        functools.partial(_fox_prompt_kernel, bq=bq, bk=bk, H=H, dh=dh, scale=dh ** -0.5, hg=hg),
        grid_spec=pltpu.PrefetchScalarGridSpec(
            num_scalar_prefetch=2,
            grid=(nseq, len(pairs)),
            in_specs=[pl.BlockSpec((bq, D), qrow(0)),
                      pl.BlockSpec((bk, D), krow(1)),
                      pl.BlockSpec((bk, D), krow(2)),
                      pl.BlockSpec((bq, D), qrow(3)),
                      pl.BlockSpec((bq, H), qrow(0)),
                      pl.BlockSpec((H, bk), lambda b, p, qi, kj: (0, b * nk + kj[p]))],
            out_specs=pl.BlockSpec((bq, D), qrow(0)),
            scratch_shapes=[pltpu.VMEM((H, bq, 2 * dh), F32), pltpu.VMEM((H, bq, dh), F32),
                            pltpu.VMEM((H, bq, dh), F32), pltpu.VMEM((bq, D), BF16)]),
        out_shape=jax.ShapeDtypeStruct((out_rows, D), BF16),
        compiler_params=_params("parallel", "arbitrary"),
        name="fox_prompt_attention",
    )(qi, kj, proj, proj, proj, proj, cum, cum_t)


def _fox_sample_kernel(q_ref, kn_ref, vn_ref, og_ref, kc_ref, vc_ref, cq_ref, ck_ref, o_ref,
                       *, P, T, G, dh, scale):
    R, PG = T * G, P * G
    gs = G.bit_length() - 1
    q = (q_ref[0, 0].astype(F32) * scale).astype(BF16)
    kc = kc_ref[0].reshape(PG, dh).astype(BF16)
    vc = vc_ref[0].reshape(PG, dh).astype(BF16)
    s_p = lax.dot_general(q, kc, NT_DIMS, preferred_element_type=F32)
    s_n = lax.dot_general(q, kn_ref[0, 0], NT_DIMS, preferred_element_type=F32)
    cq = cq_ref[0, 0]
    ck = ck_ref[0, 0]
    same_p = ((lax.broadcasted_iota(jnp.int32, (R, PG), 0) & (G - 1))
              == (lax.broadcasted_iota(jnp.int32, (R, PG), 1) & (G - 1)))
    rn = lax.broadcasted_iota(jnp.int32, (R, R), 0)
    cn = lax.broadcasted_iota(jnp.int32, (R, R), 1)
    same_n = jnp.logical_and((rn & (G - 1)) == (cn & (G - 1)), (cn >> gs) <= (rn >> gs))
    s_p = jnp.where(same_p, s_p + (cq - ck[:, :PG]), -jnp.inf)
    s_n = jnp.where(same_n, s_n + (cq - ck[:, PG:]), -jnp.inf)
    m = jnp.maximum(jnp.max(s_p, axis=1, keepdims=True), jnp.max(s_n, axis=1, keepdims=True))
    p_p = jnp.exp(s_p - m)
    p_n = jnp.exp(s_n - m)
    l = jnp.sum(p_p, axis=1, keepdims=True) + jnp.sum(p_n, axis=1, keepdims=True)
    o = (jnp.dot(p_p.astype(BF16), vc, preferred_element_type=F32)
         + jnp.dot(p_n.astype(BF16), vn_ref[0, 0], preferred_element_type=F32))
    o_ref[0, 0] = ((o / l) * _sigmoid(og_ref[0, 0].astype(F32))).astype(o_ref.dtype)


def fox_sample_attention(proj, k_cache_all, v_cache_all, layer, cum, *, row0, nseq, T, P, H, dh):
    G = SUBLANES if H % SUBLANES == 0 else H
    assert G & (G - 1) == 0
    NG = H // G
    R = T * G
    new = proj[row0:row0 + nseq * T].reshape(nseq, T, 4, NG, G, dh).transpose(2, 0, 3, 1, 4, 5)
    new = new.reshape(4, nseq, NG, R, dh)
    ck = cum.reshape(nseq, P + T, NG, G).transpose(0, 2, 1, 3).reshape(nseq, NG, 1, (P + T) * G)
    cq = cum[:, P:].reshape(nseq, T, NG, G).transpose(0, 2, 1, 3).reshape(nseq, NG, R, 1)
    part = lambda a: pl.BlockSpec((None, 1, 1, R, dh), lambda b, g: (a, b, g, 0, 0))
    cache = pl.BlockSpec((None, 1, P, G, dh), lambda b, g: (layer, b, 0, g, 0))
    out = pl.pallas_call(
        functools.partial(_fox_sample_kernel, P=P, T=T, G=G, dh=dh, scale=dh ** -0.5),
        grid=(nseq, NG),
        in_specs=[part(0), part(1), part(2), part(3), cache, cache,
                  pl.BlockSpec((1, 1, R, 1), lambda b, g: (b, g, 0, 0)),
                  pl.BlockSpec((1, 1, 1, (P + T) * G), lambda b, g: (b, g, 0, 0))],
        out_specs=pl.BlockSpec((1, 1, R, dh), lambda b, g: (b, g, 0, 0)),
        out_shape=jax.ShapeDtypeStruct((nseq, NG, R, dh), BF16),
        compiler_params=_params("parallel", "parallel"),
        name="fox_sample_attention",
    )(new, new, new, new, k_cache_all, v_cache_all, cq, ck)
    return out.reshape(nseq, NG, T, G, dh).transpose(0, 2, 1, 3, 4).reshape(nseq * T, H * dh)


def kernel(x_prompt, x_sample, state_gdn, state_gdn_conv, cache_fox_k, cache_fox_v, cache_fox_logf,
           norm_mix, norm_mlp, norm_final, gdn_w_in, gdn_conv_w, gdn_a_log, gdn_dt_bias, gdn_norm_w,
           gdn_w_out, fox_w_in, fox_b_f, fox_w_out, mlp_w_up, mlp_w_down):
    Bp, Tp, D = x_prompt.shape
    Bs, Ts, _ = x_sample.shape
    depth = norm_mix.shape[0]
    _, _, hv, dk, dv = state_gdn.shape
    conv_dim = state_gdn_conv.shape[-1]
    cw1 = state_gdn_conv.shape[2]
    v_dim = hv * dv
    qk_dim = (conv_dim - v_dim) // 2
    hqk = qk_dim // dk
    _, _, P, H, dh = cache_fox_k.shape
    fox_dim = H * dh
    d_ff = mlp_w_up.shape[-1]
    Mp, Ms = Bp * Tp, Bs * Ts
    Cp = min(GDN_CHUNK, Tp)

    x = jnp.concatenate([x_prompt.reshape(Mp, D), x_sample.reshape(Ms, D)], axis=0)
    gdn_w_in_t = jnp.swapaxes(gdn_w_in, 1, 2)
    fox_w_in_t = jnp.swapaxes(fox_w_in, 1, 2)
    S_p, S_s, buf_p, buf_s = None, None, [], []
    n_gdn = state_gdn.shape[0]
    k_p, k_s, v_p, v_s, lf_p, lf_s = [], [], [], [], [], []

    for i in range(depth):
        j = i // 2
        if i % 2 == 0:
            n_main = conv_dim + v_dim
            proj, beta, g = norm_proj(x, norm_mix[i], gdn_w_in_t, j, n_main, w_t=True, gate="gdn",
                                      gate_params=(gdn_a_log[j], gdn_dt_bias[j]), name="gdn_in_proj")

            common = dict(hv=hv, hqk=hqk, dk=dk, dv=dv)
            o, S_p = gdn_chunks(proj, gdn_conv_w, None, g, beta, gdn_norm_w[j], None, j, n_gdn, S_p,
                                row0=0, nseq=Bp, T=Tp, C=Cp, out_rows=Mp + Ms, **common)
            o_s, S_s = gdn_chunks(proj, gdn_conv_w, state_gdn_conv, g, beta, gdn_norm_w[j], state_gdn, j,
                                  n_gdn, S_s, row0=Mp, nseq=Bs, T=Ts, C=Ts, out_rows=Ms, **common)
            o = lax.dynamic_update_slice(o, o_s, (Mp, 0))
            x = matmul_res(o, gdn_w_out, j, x, name="gdn_out_proj")
            buf_p.append(jnp.stack([proj[(b + 1) * Tp - cw1:(b + 1) * Tp, :conv_dim]
                                    for b in range(Bp)]).astype(F32))
            qkv_s = proj[Mp:, :conv_dim].reshape(Bs, Ts, conv_dim).astype(F32)
            buf_s.append(jnp.concatenate([state_gdn_conv[j], qkv_s], axis=1)[:, -cw1:])
        else:
            proj, logf = norm_proj(x, norm_mix[i], fox_w_in_t, j, 4 * fox_dim, w_t=True, gate="fox",
                                   gate_params=(fox_b_f[j],), name="fox_in_proj")
            cum_p = cumsum_time(logf[:Mp].reshape(Bp, Tp, H))
            o = fox_prompt_attention(proj, cum_p.transpose(1, 0, 2).reshape(Mp, H),
                                     cum_p.transpose(2, 1, 0).reshape(H, Mp), nseq=Bp, T=Tp, H=H, dh=dh,
                                     out_rows=Mp + Ms)
            lf_new = logf[Mp:].reshape(Bs, Ts, H)
            lf_all = jnp.concatenate([cache_fox_logf[j], lf_new], axis=1)
            cum_s = cumsum_time(lf_all).transpose(1, 0, 2)
            o_s = fox_sample_attention(proj, cache_fox_k, cache_fox_v, j, cum_s,
                                       row0=Mp, nseq=Bs, T=Ts, P=P, H=H, dh=dh)
            o = lax.dynamic_update_slice(o, o_s, (Mp, 0))
            x = matmul_res(o, fox_w_out, j, x, name="fox_out_proj")
            kv = proj[:, fox_dim:3 * fox_dim].reshape(Mp + Ms, 2, H, dh)
            k_p.append(kv[:Mp, 0].astype(F32).reshape(Bp, Tp, H, dh))
            k_s.append(kv[Mp:, 0].astype(F32).reshape(Bs, Ts, H, dh))
            v_p.append(kv[:Mp, 1].astype(F32).reshape(Bp, Tp, H, dh))
            v_s.append(kv[Mp:, 1].astype(F32).reshape(Bs, Ts, H, dh))
            lf_p.append(logf[:Mp].reshape(Bp, Tp, H))
            lf_s.append(lf_new)
        hmid = norm_proj(x, norm_mlp[i], mlp_w_up, i, d_ff, relu2=True, name="mlp_up")
        x = matmul_res(hmid, mlp_w_down, i, x, name="mlp_down")

    y_p = rmsnorm_rows(x, norm_final, 0, Mp)
    y_s = rmsnorm_rows(x, norm_final, Mp, Ms)
    st = jnp.stack
    return (y_p.reshape(Bp, Tp, D), y_s.reshape(Bs, Ts, D),
            S_p, st(buf_p), st(k_p), st(v_p), st(lf_p),
            S_s, st(buf_s), st(k_s), st(v_s), st(lf_s))
```

```python
import functools

import jax
import jax.numpy as jnp
from jax import lax
from jax.experimental import pallas as pl
from jax.experimental.pallas import tpu as pltpu

F32 = jnp.float32
BF16 = jnp.bfloat16
EPS = 1e-6
GDN_CHUNK = 64
FOX_KEY_BLOCKS = 1
GDN_HEAD_GROUP = 32
CONV_HALO_ROWS = 16
INV_BASE_BLOCK = 8
VMEM_LIMIT_BYTES = 60 * 1024 * 1024
LOG2E = 1.4426950408889634
SUBLANES = 8
BF16_ROWS = 16
LANES = 128
ROW_TILE = 1088
IN_COL_TILE = 1024
OUT_COL_TILE = 512
K_TILE = 4096
NORM_CHUNK_ROWS = 272
NORM_ROW_TILE = 512
TIME_BLOCK = 256
FOX_HEAD_GROUP = 2
HIGHEST = lax.Precision.HIGHEST
NT_DIMS = (((1,), (1,)), ((), ()))
TN_DIMS = (((0,), (0,)), ((), ()))


def _pick(n, pref, align):
    best = None
    for d in range(align, min(n, pref) + 1, align):
        if n % d == 0:
            best = d
    return n if best is None else best


def _params(*sem):
    return pltpu.CompilerParams(dimension_semantics=sem, vmem_limit_bytes=VMEM_LIMIT_BYTES)


def _sigmoid(x):
    return 1.0 / (1.0 + jnp.exp(-x))


def _softplus(x):
    return jnp.maximum(x, 0.0) + jnp.log1p(jnp.exp(-jnp.abs(x)))


def _rms_rows(x, w):
    ms = jnp.mean(x * x, axis=-1, keepdims=True)
    return x * lax.rsqrt(ms + EPS) * w


def _wdot(xn, w_ref, w_t):
    w = w_ref[...].astype(BF16)
    if w_t:
        return lax.dot_general(xn, w, NT_DIMS, preferred_element_type=F32)
    return jnp.dot(xn, w, preferred_element_type=F32)


def _gdn_gate_outputs(acc, alog_ref, dtb_ref, beta_ref, g_ref):
    hv = beta_ref.shape[1]
    beta_ref[...] = _sigmoid(acc[:, :hv])
    g_ref[...] = -jnp.exp(alog_ref[...]) * _softplus(acc[:, hv:2 * hv] + dtb_ref[...])


def _fox_gate_outputs(acc, bf_ref, lf_ref):
    f = acc[:, :lf_ref.shape[1]] + bf_ref[...]
    lf_ref[...] = -_softplus(-f)


GATES = {"gdn": (_gdn_gate_outputs, 2, 2), "fox": (_fox_gate_outputs, 1, 1)}


def _norm_proj_kernel(x_ref, nw_ref, w_ref, *rest, relu2, w_t, gate):
    if gate is None:
        o_ref, xn_ref = rest
    else:
        epilogue, n_par, n_out = GATES[gate]
        gw_ref, par_refs = rest[0], rest[1:1 + n_par]
        o_ref, gate_refs, xn_ref = rest[1 + n_par], rest[2 + n_par:2 + n_par + n_out], rest[-1]

    def tile(xn, w16):
        if w_t:
            acc = lax.dot_general(xn, w16, NT_DIMS, preferred_element_type=F32)
        else:
            acc = jnp.dot(xn, w16, preferred_element_type=F32)
        if relu2:
            acc = jnp.square(jnp.maximum(acc, 0.0))
        return acc.astype(o_ref.dtype)

    j = pl.program_id(1)

    @pl.when(j == 0)
    def _():
        tm = x_ref.shape[0]
        rows = _pick(tm, NORM_CHUNK_ROWS, BF16_ROWS)
        w16 = w_ref[...].astype(BF16)
        for r in range(tm // rows):
            sl = slice(r * rows, (r + 1) * rows)
            xn = _rms_rows(x_ref[sl, :], nw_ref[...]).astype(BF16)
            xn_ref[sl, :] = xn
            o_ref[sl, :] = tile(xn, w16)
        if gate is not None:
            epilogue(_wdot(xn_ref[...], gw_ref, True), *par_refs, *gate_refs)

    @pl.when(j > 0)
    def _():
        o_ref[...] = tile(xn_ref[...], w_ref[...].astype(BF16))


def norm_proj(x, nw, w_all, layer, n_cols, *, relu2=False, w_t=False, gate=None, gate_params=(),
              name="norm_proj"):
    M, K = x.shape
    tm = _pick(M, ROW_TILE, BF16_ROWS)
    tn = _pick(n_cols, IN_COL_TILE, LANES)
    if w_t:
        w_spec = pl.BlockSpec((None, tn, K), lambda i, j: (layer, j, 0))
    else:
        w_spec = pl.BlockSpec((None, K, tn), lambda i, j: (layer, 0, j))
    in_specs = [pl.BlockSpec((tm, K), lambda i, j: (i, 0)), pl.BlockSpec((1, K), lambda i, j: (0, 0)), w_spec]
    args = [x, nw.reshape(1, K), w_all]
    out_specs = [pl.BlockSpec((tm, tn), lambda i, j: (i, j))]
    out_shape = [jax.ShapeDtypeStruct((M, n_cols), BF16)]
    if gate is not None:
        assert w_t and n_cols % LANES == 0
        H = gate_params[0].shape[0]
        assert GATES[gate][2] * H <= LANES
        in_specs.append(pl.BlockSpec((None, LANES, K), lambda i, j: (layer, n_cols // LANES, 0)))
        args.append(w_all)
        for p in gate_params:
            in_specs.append(pl.BlockSpec((1, H), lambda i, j: (0, 0)))
            args.append(p.reshape(1, H))
        for _ in range(GATES[gate][2]):
            out_specs.append(pl.BlockSpec((tm, H), lambda i, j: (i, 0)))
            out_shape.append(jax.ShapeDtypeStruct((M, H), F32))
    out = pl.pallas_call(
        functools.partial(_norm_proj_kernel, relu2=relu2, w_t=w_t, gate=gate),
        grid=(M // tm, n_cols // tn),
        in_specs=in_specs,
        out_specs=out_specs,
        out_shape=out_shape,
        scratch_shapes=[pltpu.VMEM((tm, K), BF16)],
        compiler_params=_params("parallel", "arbitrary"),
        name=name,
    )(*args)
    return out[0] if gate is None else out


def _matmul_res_kernel(a_ref, w_ref, r_ref, o_ref, *acc, nk):
    part = jnp.dot(a_ref[...], w_ref[...].astype(BF16), preferred_element_type=F32)
    if nk == 1:
        o_ref[...] = r_ref[...] + part
        return
    (acc_ref,) = acc
    k = pl.program_id(1)
    j = pl.program_id(2)

    @pl.when(k == 0)
    def _():
        acc_ref[j] = part

    if nk > 2:
        @pl.when(jnp.logical_and(k > 0, k < nk - 1))
        def _():
            acc_ref[j] += part

    @pl.when(k == nk - 1)
    def _():
        o_ref[...] = r_ref[...] + (acc_ref[j] + part)


def matmul_res(a, w_all, layer, res, *, name="matmul_res"):
    M, K = a.shape
    N = w_all.shape[-1]
    tm = _pick(M, ROW_TILE, BF16_ROWS)
    tn = _pick(N, OUT_COL_TILE, LANES)
    tk = _pick(K, K_TILE, LANES)
    nk = K // tk
    out_idx = lambda i, k, j: (i, jnp.where(k == nk - 1, j, 0))
    return pl.pallas_call(
        functools.partial(_matmul_res_kernel, nk=nk),
        grid=(M // tm, nk, N // tn),
        in_specs=[pl.BlockSpec((tm, tk), lambda i, k, j: (i, k)),
                  pl.BlockSpec((None, tk, tn), lambda i, k, j: (layer, k, j)),
                  pl.BlockSpec((tm, tn), out_idx)],
        out_specs=pl.BlockSpec((tm, tn), out_idx),
        out_shape=jax.ShapeDtypeStruct((M, N), F32),
        scratch_shapes=[pltpu.VMEM((N // tn, tm, tn), F32)] if nk > 1 else [],
        compiler_params=_params("parallel", "arbitrary", "arbitrary"),
        name=name,
    )(a, w_all, res)


def _rmsnorm_kernel(x_ref, nw_ref, o_ref):
    o_ref[...] = _rms_rows(x_ref[...], nw_ref[...])


def rmsnorm_rows(x, nw, row0, n_rows):
    K = x.shape[1]
    tm = _pick(n_rows, NORM_ROW_TILE, SUBLANES)
    rb0 = row0 // tm
    return pl.pallas_call(
        _rmsnorm_kernel,
        grid=(n_rows // tm,),
        in_specs=[pl.BlockSpec((tm, K), lambda i: (rb0 + i, 0)), pl.BlockSpec((1, K), lambda i: (0, 0))],
        out_specs=pl.BlockSpec((tm, K), lambda i: (i, 0)),
        out_shape=jax.ShapeDtypeStruct((n_rows, K), F32),
        compiler_params=_params("parallel"),
        name="final_rmsnorm",
    )(x, nw.reshape(1, K))


def _gdn_chunk_kernel(x_ref, xp_ref, cw_ref, z_ref, g_ref, gt_ref, bt_ref, nw_ref, *rest,
                      C, hq, rep, dk, dv, n_chunks, has_s0, has_prev):
    rest = list(rest)
    buf_ref = rest.pop(0) if has_s0 else None
    s0_ref = rest.pop(0) if has_s0 else None
    if has_prev:
        rest.pop(0)
    o_ref, sout_ref, s_ref = rest
    n = pl.program_id(2)
    C2 = 2 * C
    cw = cw_ref.shape[0]
    qk_dim = hq * dk
    pair = 2 * dk
    assert dv == dk and cw - 1 <= CONV_HALO_ROWS <= C

    @pl.when(n == 0)
    def _():
        if has_s0:
            s_ref[...] = s0_ref[0]
        else:
            s_ref[...] = jnp.zeros_like(s_ref)

    K = CONV_HALO_ROWS + C
    sr = lax.broadcasted_iota(jnp.int32, ((cw - 1) * C, K), 0)
    sc = lax.broadcasted_iota(jnp.int32, ((cw - 1) * C, K), 1)
    cshift = C.bit_length() - 1
    sel = jnp.where(sc == CONV_HALO_ROWS + (sr & (C - 1)) - ((sr >> cshift) + 1), 1.0, 0.0).astype(BF16)
    first = n == 0
    trow = lax.broadcasted_iota(jnp.int32, (C, pair), 0)
    conv_cache = {}

    def conv_pair(p):
        if p in conv_cache:
            return conv_cache[p]
        cs = slice(p * pair, (p + 1) * pair)
        x_cur = x_ref[:, cs]
        x_prev = xp_ref[C - CONV_HALO_ROWS:C, cs]
        x_prev = jnp.where(first, jnp.zeros_like(x_prev), x_prev)
        sh = jnp.dot(sel, jnp.concatenate([x_prev, x_cur], axis=0), preferred_element_type=F32)
        acc = x_cur.astype(F32) * cw_ref[cw - 1:cw, cs]
        for s in range(1, cw):
            acc = acc + sh[(s - 1) * C:s * C] * cw_ref[cw - 1 - s:cw - s, cs]
        if has_s0:
            corr = jnp.zeros((C, pair), F32)
            for t in range(cw - 1):
                r = sum(cw_ref[cw - 1 - s:cw - s, cs] * buf_ref[0, cw - 1 - (s - t):cw - (s - t), cs]
                        for s in range(t + 1, cw))
                corr = jnp.where(trow == t, r, corr)
            acc = acc + jnp.where(first, corr, 0.0)
        y = acc * _sigmoid(acc)
        conv_cache[p] = y
        return y

    def conv(c0, l2_scale=None):
        y = conv_pair(c0 // pair)[:, (c0 % pair):(c0 % pair) + dk]
        if l2_scale is None:
            return y
        return y * (lax.rsqrt(jnp.sum(y * y, axis=-1, keepdims=True) + EPS) * l2_scale)

    row = lax.broadcasted_iota(jnp.int32, (C, C2), 0)
    lane = lax.broadcasted_iota(jnp.int32, (C, C2), 1)
    right = lane >= C
    col = jnp.where(right, lane - C, lane)
    incl = row >= col
    strict = row > col
    eye_right = jnp.where(lane == row + C, 1.0, 0.0).astype(F32)
    triu2 = jnp.where(row <= col, 1.0, 0.0).astype(F32)
    tril = jnp.where(lax.broadcasted_iota(jnp.int32, (C, C), 0) >= lax.broadcasted_iota(jnp.int32, (C, C), 1),
                     1.0, 0.0).astype(F32)
    zeros_z = jnp.zeros((C, C2), BF16)
    zeros_r = jnp.zeros((C, dk + dv), BF16)
    G2_all = jnp.dot(g_ref[0, n], triu2, precision=HIGHEST, preferred_element_type=F32)
    Gc_all = jnp.dot(tril, gt_ref[...], precision=HIGHEST, preferred_element_type=F32)
    beta_all = bt_ref[...]

    def head_group(heads):
        qa = sorted({i // rep for i in heads})
        q = {a: conv(a * dk, dk ** -0.5) for a in qa}
        k = {a: conv(qk_dim + a * dk, 1.0) for a in qa}
        q16 = {a: q[a].astype(BF16) for a in qa}
        kk16 = {a: jnp.concatenate([k[a].astype(BF16)] * 2, axis=0) for a in qa}
        G2, G_col, b_col, dec_incl, gamma, G_last, kb, vb = {}, {}, {}, {}, {}, {}, {}, {}
        for i in heads:
            G2[i] = G2_all[i:i + 1, :]
            G_col[i] = Gc_all[:, i:i + 1]
            b_col[i] = beta_all[:, i:i + 1]
            dec_incl[i] = jnp.exp(jnp.where(incl, G_col[i] - G2[i], -jnp.inf))
            gamma[i] = jnp.exp(G_col[i])
            G_last[i] = G2[i][:, C - 1:C]
            kb[i] = k[i // rep] * b_col[i]
            vb[i] = conv(2 * qk_dim + i * dv) * b_col[i]
        gram = {i: lax.dot_general(jnp.concatenate([kb[i].astype(BF16), q16[i // rep]], axis=0), kk16[i // rep],
                                   NT_DIMS, preferred_element_type=F32) for i in heads}
        Lf = {i: gram[i][:C] * dec_incl[i] for i in heads}
        base = min(INV_BASE_BLOCK, C)
        sh = base.bit_length() - 1
        in_base = jnp.logical_and(strict, (row >> sh) == (col >> sh))
        Z = {i: jnp.where(right, eye_right, -jnp.where(in_base, Lf[i], 0.0)) for i in heads}
        for _ in range((base - 1).bit_length()):
            Z16 = {i: Z[i].astype(BF16) for i in heads}
            Z = {i: jnp.dot(Z16[i], jnp.concatenate([Z16[i], zeros_z], axis=0), preferred_element_type=F32)
                 + jnp.where(right, Z[i], 0.0) for i in heads}
        b = base
        while b < C:
            sh = b.bit_length() - 1
            lower_left = jnp.logical_and(jnp.logical_and(right, (row >> (sh + 1)) == (col >> (sh + 1))),
                                         jnp.logical_and(((row >> sh) & 1) == 1, ((col >> sh) & 1) == 0))
            Z16 = {i: Z[i].astype(BF16) for i in heads}
            W16 = {i: jnp.dot(jnp.where(lower_left, Lf[i], 0.0).astype(BF16),
                              jnp.concatenate([zeros_z, Z16[i]], axis=0),
                              preferred_element_type=F32).astype(BF16) for i in heads}
            Z = {i: Z[i] - jnp.dot(Z16[i], jnp.concatenate([zeros_z, W16[i]], axis=0),
                                   preferred_element_type=F32) for i in heads}
            b *= 2
        wu = {}
        for i in heads:
            rhs = jnp.concatenate([(kb[i] * gamma[i]).astype(BF16), vb[i].astype(BF16)], axis=1)
            wu[i] = jnp.dot(Z[i].astype(BF16), jnp.concatenate([zeros_r, rhs], axis=0),
                            preferred_element_type=F32)
        S = {i: s_ref[i] for i in heads}
        S16 = {i: S[i].astype(BF16) for i in heads}
        ws_qs = {i: jnp.dot(jnp.concatenate([wu[i][:, :dk].astype(BF16),
                                             (q[i // rep] * gamma[i]).astype(BF16)], axis=0),
                            S16[i], preferred_element_type=F32) for i in heads}
        U16 = {i: (wu[i][:, dk:] - ws_qs[i][:C]).astype(BF16) for i in heads}
        A16 = {i: (gram[i][C:] * dec_incl[i])[:, :C].astype(BF16) for i in heads}
        o = {i: ws_qs[i][C:] + jnp.dot(A16[i], U16[i], preferred_element_type=F32) for i in heads}
        for i in heads:
            kd = k[i // rep] * jnp.exp(G_last[i] - G_col[i])
            s_ref[i] = jnp.exp(G_last[i]) * S[i] + lax.dot_general(kd.astype(BF16), U16[i], TN_DIMS,
                                                                   preferred_element_type=F32)
        for i in heads:
            z = z_ref[:, i * dv:(i + 1) * dv].astype(F32)
            o_ref[:, i * dv:(i + 1) * dv] = (_rms_rows(o[i], nw_ref[...])
                                             * (z * _sigmoid(z))).astype(o_ref.dtype)

    n_heads = hq * rep
    for h0 in range(0, n_heads, GDN_HEAD_GROUP):
        head_group(range(h0, min(h0 + GDN_HEAD_GROUP, n_heads)))

    @pl.when(n == n_chunks - 1)
    def _():
        sout_ref[0] = s_ref[...]


def gdn_chunks(proj, conv_w_all, buf_all, g, beta, norm_w, s0_all, layer, n_layers, s_prev, *, row0,
               nseq, T, C, hv, hqk, dk, dv, out_rows):
    N = T // C
    cw = conv_w_all.shape[1]
    conv_dim = 2 * hqk * dk + hv * dv
    assert conv_dim % (hv * dv) == 0
    has_s0 = s0_all is not None
    rb0 = row0 // C
    g_rows = g[row0:row0 + nseq * T].reshape(nseq, N, C, hv).transpose(0, 1, 3, 2)
    tok = pl.BlockSpec((C, hv), lambda b, h, n: (rb0 + b * N + n, 0))
    in_specs = [pl.BlockSpec((C, conv_dim), lambda b, h, n: (rb0 + b * N + n, 0)),
                pl.BlockSpec((C, conv_dim), lambda b, h, n: (rb0 + b * N + jnp.maximum(n - 1, 0), 0)),
                pl.BlockSpec((None, cw, conv_dim), lambda b, h, n: (layer, 0, 0)),
                pl.BlockSpec((C, hv * dv), lambda b, h, n: (rb0 + b * N + n, conv_dim // (hv * dv))),
                pl.BlockSpec((1, N, hv, C), lambda b, h, n: (b, 0, 0, 0)),
                tok, tok,
                pl.BlockSpec((1, dv), lambda b, h, n: (0, 0))]
    args = [proj, proj, conv_w_all, proj, g_rows, g, beta, norm_w.reshape(1, dv)]
    if has_s0:
        in_specs += [pl.BlockSpec((None, 1, cw - 1, conv_dim), lambda b, h, n: (layer, b, 0, 0)),
                     pl.BlockSpec((None, 1, hv, dk, dv), lambda b, h, n: (layer, b, 0, 0, 0))]
        args += [buf_all, s0_all]
    aliases = {}
    if s_prev is not None:
        aliases[len(args)] = 1
        in_specs.append(pl.BlockSpec(memory_space=pl.ANY))
        args.append(s_prev)
    return pl.pallas_call(
        functools.partial(_gdn_chunk_kernel, C=C, hq=hqk, rep=hv // hqk, dk=dk, dv=dv, n_chunks=N,
                          has_s0=has_s0, has_prev=s_prev is not None),
        grid=(nseq, 1, N),
        in_specs=in_specs,
        out_specs=[pl.BlockSpec((C, hv * dv), lambda b, h, n: (b * N + n, 0)),
                   pl.BlockSpec((None, 1, hv, dk, dv), lambda b, h, n: (layer, b, 0, 0, 0))],
        out_shape=[jax.ShapeDtypeStruct((out_rows, hv * dv), BF16),
                   jax.ShapeDtypeStruct((n_layers, nseq, hv, dk, dv), F32)],
        input_output_aliases=aliases,
        scratch_shapes=[pltpu.VMEM((hv, dk, dv), F32)],
        compiler_params=_params("parallel", "parallel", "arbitrary"),
        name="gdn_chunks",
    )(*args)


def _cumsum_kernel(x_ref, o_ref, carry_ref, *, tb):
    @pl.when(pl.program_id(0) == 0)
    def _():
        carry_ref[...] = jnp.zeros_like(carry_ref)

    row = lax.broadcasted_iota(jnp.int32, (tb, tb), 0)
    col = lax.broadcasted_iota(jnp.int32, (tb, tb), 1)
    tril = jnp.where(row >= col, 1.0, 0.0).astype(F32)
    c = jnp.dot(tril, x_ref[...], precision=HIGHEST, preferred_element_type=F32) + carry_ref[...]
    o_ref[...] = c
    carry_ref[...] = c[tb - 1:tb, :]


def cumsum_time(x):
    nseq, L, H = x.shape
    S = nseq * H
    tb = _pick(L, TIME_BLOCK, SUBLANES)
    out = pl.pallas_call(
        functools.partial(_cumsum_kernel, tb=tb),
        grid=(L // tb,),
        in_specs=[pl.BlockSpec((tb, S), lambda t: (t, 0))],
        out_specs=pl.BlockSpec((tb, S), lambda t: (t, 0)),
        out_shape=jax.ShapeDtypeStruct((L, S), F32),
        scratch_shapes=[pltpu.VMEM((1, S), F32)],
        compiler_params=_params("arbitrary"),
        name="fox_cumsum",
    )(x.transpose(1, 0, 2).reshape(L, S))
    return out.reshape(L, nseq, H)


def _fox_prompt_kernel(qi_ref, kj_ref, q_ref, k_ref, v_ref, og_ref, cq_ref, ckt_ref, o_ref,
                       acc_ref, m_ref, cqr_ref, q2_ref, *, bq, bk, H, dh, scale, hg):
    i = qi_ref[pl.program_id(1)]
    j = kj_ref[pl.program_id(1)]
    rep = bk // dh
    r = bk // bq
    j_last = i // r

    @pl.when(j == 0)
    def _():
        m_ref[...] = jnp.full(m_ref.shape, -jnp.inf, F32)
        acc_ref[...] = jnp.zeros_like(acc_ref)
        cqs = cq_ref[...] * LOG2E
        for h in range(H):
            cqr_ref[h] = jnp.broadcast_to(cqs[:, h:h + 1], (bq, dh))
        q2_ref[...] = (q_ref[...].astype(F32) * (scale * LOG2E)).astype(BF16)

    def block(masked):
        if masked:
            mask = (lax.broadcasted_iota(jnp.int32, (bq, bk), 1)
                    <= lax.broadcasted_iota(jnp.int32, (bq, bk), 0) + (i - r * j_last) * bq)
        ones16 = jnp.ones((bk, dh), BF16)
        ck2 = ckt_ref[...] * LOG2E

        def qk(h):
            hs = slice(h * dh, (h + 1) * dh)
            return lax.dot_general(q2_ref[:, hs], k_ref[:, hs], NT_DIMS, preferred_element_type=F32)

        groups = [range(g * hg, (g + 1) * hg) for g in range(H // hg)]
        s = {h: qk(h) for h in groups[0]}
        for gi, grp in enumerate(groups):
            if gi + 1 < len(groups):
                for h in groups[gi + 1]:
                    s[h] = qk(h)
            p16, alpha2 = {}, {}
            for h in grp:
                t = s.pop(h) - ck2[h:h + 1, :]
                if masked:
                    t = jnp.where(mask, t, -jnp.inf)
                cq2 = cqr_ref[h]
                m_prev = m_ref[h]
                m_new = jnp.maximum(m_prev, jnp.max(t, axis=1, keepdims=True) + cq2)
                m_ref[h] = m_new
                alpha = jnp.exp2(m_prev - m_new)
                c = cq2 - m_new
                p16[h] = jnp.exp2(t + jnp.concatenate([c] * rep, axis=1)).astype(BF16)
                alpha2[h] = jnp.concatenate([alpha, alpha], axis=1)
            for h in grp:
                hs = slice(h * dh, (h + 1) * dh)
                v_ext = jnp.concatenate([v_ref[:, hs], ones16], axis=1)
                acc_ref[h] = alpha2[h] * acc_ref[h] + jnp.dot(p16[h], v_ext, preferred_element_type=F32)

    @pl.when(j < j_last)
    def _():
        block(False)

    @pl.when(j == j_last)
    def _():
        block(True)
        for h in range(H):
            hs = slice(h * dh, (h + 1) * dh)
            a = acc_ref[h]
            o_ref[:, hs] = ((a[:, :dh] / a[:, dh:]) * _sigmoid(og_ref[:, hs].astype(F32))).astype(o_ref.dtype)


def fox_prompt_attention(proj, cum, cum_t, *, nseq, T, H, dh, out_rows):
    bq = _pick(T, TIME_BLOCK, LANES)
    bk = _pick(T, FOX_KEY_BLOCKS * bq, bq)
    nq, nk, r = T // bq, T // bk, bk // bq
    D = H * dh
    hg = _pick(H, FOX_HEAD_GROUP, 1)
    pairs = [(i, j) for i in range(nq) for j in range(i // r + 1)]
    qi = jnp.asarray([p[0] for p in pairs], jnp.int32)
    kj = jnp.asarray([p[1] for p in pairs], jnp.int32)
    qrow = lambda col: (lambda b, p, qi, kj: (b * nq + qi[p], col))
    krow = lambda col: (lambda b, p, qi, kj: (b * nk + kj[p], col))
    return pl.pallas_call(
        functools.partial(_fox_prompt_kernel, bq=bq, bk=bk, H=H, dh=dh, scale=dh ** -0.5, hg=hg),
        grid_spec=pltpu.PrefetchScalarGridSpec(
            num_scalar_prefetch=2,
            grid=(nseq, len(pairs)),
            in_specs=[pl.BlockSpec((bq, D), qrow(0)),
                      pl.BlockSpec((bk, D), krow(1)),
                      pl.BlockSpec((bk, D), krow(2)),
                      pl.BlockSpec((bq, D), qrow(3)),
                      pl.BlockSpec((bq, H), qrow(0)),
                      pl.BlockSpec((H, bk), lambda b, p, qi, kj: (0, b * nk + kj[p]))],
            out_specs=pl.BlockSpec((bq, D), qrow(0)),
            scratch_shapes=[pltpu.VMEM((H, bq, 2 * dh), F32), pltpu.VMEM((H, bq, dh), F32),
                            pltpu.VMEM((H, bq, dh), F32), pltpu.VMEM((bq, D), BF16)]),
        out_shape=jax.ShapeDtypeStruct((out_rows, D), BF16),
        compiler_params=_params("parallel", "arbitrary"),
        name="fox_prompt_attention",
    )(qi, kj, proj, proj, proj, proj, cum, cum_t)


def _fox_sample_kernel(q_ref, kn_ref, vn_ref, og_ref, kc_ref, vc_ref, cq_ref, ck_ref, mb_ref, o_ref,
                       *, P, T, G, dh, scale):
    PG = P * G
    q = (q_ref[0, 0].astype(F32) * scale).astype(BF16)
    kc = kc_ref[0].reshape(PG, dh).astype(BF16)
    vc = vc_ref[0].reshape(PG, dh).astype(BF16)
    s_p = lax.dot_general(q, kc, NT_DIMS, preferred_element_type=F32)
    s_n = lax.dot_general(q, kn_ref[0, 0], NT_DIMS, preferred_element_type=F32)
    cq = cq_ref[0, 0]
    ck = ck_ref[0, 0]
    t_p = (s_p - ck[:, :PG]) + mb_ref[:, :PG]
    t_n = (s_n - ck[:, PG:]) + mb_ref[:, PG:]
    m = jnp.maximum(jnp.max(t_p, axis=1, keepdims=True), jnp.max(t_n, axis=1, keepdims=True)) + cq
    c = cq - m
    p_p = jnp.exp(t_p + c)
    p_n = jnp.exp(t_n + c)
    l = jnp.sum(p_p, axis=1, keepdims=True) + jnp.sum(p_n, axis=1, keepdims=True)
    o = (jnp.dot(p_p.astype(BF16), vc, preferred_element_type=F32)
         + jnp.dot(p_n.astype(BF16), vn_ref[0, 0], preferred_element_type=F32))
    o_ref[0, 0] = ((o / l) * _sigmoid(og_ref[0, 0].astype(F32))).astype(o_ref.dtype)


def fox_sample_attention(proj, k_cache_all, v_cache_all, layer, cum, *, row0, nseq, T, P, H, dh):
    G = SUBLANES if H % SUBLANES == 0 else H
    assert G & (G - 1) == 0
    NG = H // G
    R = T * G
    new = proj[row0:row0 + nseq * T].reshape(nseq, T, 4, NG, G, dh).transpose(2, 0, 3, 1, 4, 5)
    new = new.reshape(4, nseq, NG, R, dh)
    ck = cum.reshape(nseq, P + T, NG, G).transpose(0, 2, 1, 3).reshape(nseq, NG, 1, (P + T) * G)
    cq = cum[:, P:].reshape(nseq, T, NG, G).transpose(0, 2, 1, 3).reshape(nseq, NG, R, 1)
    rr = jnp.arange(R, dtype=jnp.int32)[:, None]
    cc = jnp.arange((P + T) * G, dtype=jnp.int32)[None, :]
    visible = jnp.logical_and(rr % G == cc % G, cc // G <= P + rr // G)
    mask_bias = jnp.where(visible, 0.0, -jnp.inf).astype(F32)
    part = lambda a: pl.BlockSpec((None, 1, 1, R, dh), lambda b, g: (a, b, g, 0, 0))
    cache = pl.BlockSpec((None, 1, P, G, dh), lambda b, g: (layer, b, 0, g, 0))
    out = pl.pallas_call(
        functools.partial(_fox_sample_kernel, P=P, T=T, G=G, dh=dh, scale=dh ** -0.5),
        grid=(nseq, NG),
        in_specs=[part(0), part(1), part(2), part(3), cache, cache,
                  pl.BlockSpec((1, 1, R, 1), lambda b, g: (b, g, 0, 0)),
                  pl.BlockSpec((1, 1, 1, (P + T) * G), lambda b, g: (b, g, 0, 0)),
                  pl.BlockSpec((R, (P + T) * G), lambda b, g: (0, 0))],
        out_specs=pl.BlockSpec((1, 1, R, dh), lambda b, g: (b, g, 0, 0)),
        out_shape=jax.ShapeDtypeStruct((nseq, NG, R, dh), BF16),
        compiler_params=_params("parallel", "parallel"),
        name="fox_sample_attention",
    )(new, new, new, new, k_cache_all, v_cache_all, cq, ck, mask_bias)
    return out.reshape(nseq, NG, T, G, dh).transpose(0, 2, 1, 3, 4).reshape(nseq * T, H * dh)


def kernel(x_prompt, x_sample, state_gdn, state_gdn_conv, cache_fox_k, cache_fox_v, cache_fox_logf,
           norm_mix, norm_mlp, norm_final, gdn_w_in, gdn_conv_w, gdn_a_log, gdn_dt_bias, gdn_norm_w,
           gdn_w_out, fox_w_in, fox_b_f, fox_w_out, mlp_w_up, mlp_w_down):
    Bp, Tp, D = x_prompt.shape
    Bs, Ts, _ = x_sample.shape
    depth = norm_mix.shape[0]
    _, _, hv, dk, dv = state_gdn.shape
    conv_dim = state_gdn_conv.shape[-1]
    cw1 = state_gdn_conv.shape[2]
    v_dim = hv * dv
    qk_dim = (conv_dim - v_dim) // 2
    hqk = qk_dim // dk
    _, _, P, H, dh = cache_fox_k.shape
    fox_dim = H * dh
    d_ff = mlp_w_up.shape[-1]
    Mp, Ms = Bp * Tp, Bs * Ts
    Cp = min(GDN_CHUNK, Tp)

    x = jnp.concatenate([x_prompt.reshape(Mp, D), x_sample.reshape(Ms, D)], axis=0)
    gdn_w_in_t = jnp.swapaxes(gdn_w_in, 1, 2)
    fox_w_in_t = jnp.swapaxes(fox_w_in, 1, 2)
    S_p, S_s, buf_p, buf_s = None, None, [], []
    n_gdn = state_gdn.shape[0]
    k_p, k_s, v_p, v_s, lf_p, lf_s = [], [], [], [], [], []

    for i in range(depth):
        j = i // 2
        if i % 2 == 0:
            n_main = conv_dim + v_dim
            proj, beta, g = norm_proj(x, norm_mix[i], gdn_w_in_t, j, n_main, w_t=True, gate="gdn",
                                      gate_params=(gdn_a_log[j], gdn_dt_bias[j]), name="gdn_in_proj")

            common = dict(hv=hv, hqk=hqk, dk=dk, dv=dv)
            o, S_p = gdn_chunks(proj, gdn_conv_w, None, g, beta, gdn_norm_w[j], None, j, n_gdn, S_p,
                                row0=0, nseq=Bp, T=Tp, C=Cp, out_rows=Mp + Ms, **common)
            o_s, S_s = gdn_chunks(proj, gdn_conv_w, state_gdn_conv, g, beta, gdn_norm_w[j], state_gdn, j,
                                  n_gdn, S_s, row0=Mp, nseq=Bs, T=Ts, C=Ts, out_rows=Ms, **common)
            o = lax.dynamic_update_slice(o, o_s, (Mp, 0))
            x = matmul_res(o, gdn_w_out, j, x, name="gdn_out_proj")
            buf_p.append(jnp.stack([proj[(b + 1) * Tp - cw1:(b + 1) * Tp, :conv_dim]
                                    for b in range(Bp)]).astype(F32))
            qkv_s = proj[Mp:, :conv_dim].reshape(Bs, Ts, conv_dim).astype(F32)
            buf_s.append(jnp.concatenate([state_gdn_conv[j], qkv_s], axis=1)[:, -cw1:])
        else:
            proj, logf = norm_proj(x, norm_mix[i], fox_w_in_t, j, 4 * fox_dim, w_t=True, gate="fox",
                                   gate_params=(fox_b_f[j],), name="fox_in_proj")
            cum_p = cumsum_time(logf[:Mp].reshape(Bp, Tp, H))
            o = fox_prompt_attention(proj, cum_p.transpose(1, 0, 2).reshape(Mp, H),
                                     cum_p.transpose(2, 1, 0).reshape(H, Mp), nseq=Bp, T=Tp, H=H, dh=dh,
                                     out_rows=Mp + Ms)
            lf_new = logf[Mp:].reshape(Bs, Ts, H)
            lf_all = jnp.concatenate([cache_fox_logf[j], lf_new], axis=1)
            cum_s = cumsum_time(lf_all).transpose(1, 0, 2)
            o_s = fox_sample_attention(proj, cache_fox_k, cache_fox_v, j, cum_s,
                                       row0=Mp, nseq=Bs, T=Ts, P=P, H=H, dh=dh)
            o = lax.dynamic_update_slice(o, o_s, (Mp, 0))
            x = matmul_res(o, fox_w_out, j, x, name="fox_out_proj")
            kv = proj[:, fox_dim:3 * fox_dim].reshape(Mp + Ms, 2, H, dh)
            k_p.append(kv[:Mp, 0].astype(F32).reshape(Bp, Tp, H, dh))
            k_s.append(kv[Mp:, 0].astype(F32).reshape(Bs, Ts, H, dh))
            v_p.append(kv[:Mp, 1].astype(F32).reshape(Bp, Tp, H, dh))
            v_s.append(kv[Mp:, 1].astype(F32).reshape(Bs, Ts, H, dh))
            lf_p.append(logf[:Mp].reshape(Bp, Tp, H))
            lf_s.append(lf_new)
        hmid = norm_proj(x, norm_mlp[i], mlp_w_up, i, d_ff, relu2=True, name="mlp_up")
        x = matmul_res(hmid, mlp_w_down, i, x, name="mlp_down")

    y_p = rmsnorm_rows(x, norm_final, 0, Mp)
    y_s = rmsnorm_rows(x, norm_final, Mp, Ms)
    st = jnp.stack
    return (y_p.reshape(Bp, Tp, D), y_s.reshape(Bs, Ts, D),
            S_p, st(buf_p), st(k_p), st(v_p), st(lf_p),
            S_s, st(buf_s), st(k_s), st(v_s), st(lf_s))
```

```python
import functools

import jax
import jax.numpy as jnp
from jax import lax
from jax.experimental import pallas as pl
from jax.experimental.pallas import tpu as pltpu

F32 = jnp.float32
BF16 = jnp.bfloat16
EPS = 1e-6
GDN_CHUNK = 64
FOX_KEY_BLOCKS = 1
GDN_HEAD_GROUP = 32
CONV_HALO_ROWS = 16
INV_BASE_BLOCK = 8
VMEM_LIMIT_BYTES = 60 * 1024 * 1024
LOG2E = 1.4426950408889634
SUBLANES = 8
BF16_ROWS = 16
LANES = 128
ROW_TILE = 1088
IN_COL_TILE = 1024
OUT_COL_TILE = 512
K_TILE = 4096
NORM_CHUNK_ROWS = 272
NORM_ROW_TILE = 512
TIME_BLOCK = 256
FOX_HEAD_GROUP = 2
HIGHEST = lax.Precision.HIGHEST
NT_DIMS = (((1,), (1,)), ((), ()))
TN_DIMS = (((0,), (0,)), ((), ()))


def _pick(n, pref, align):
    best = None
    for d in range(align, min(n, pref) + 1, align):
        if n % d == 0:
            best = d
    return n if best is None else best


def _params(*sem):
    return pltpu.CompilerParams(dimension_semantics=sem, vmem_limit_bytes=VMEM_LIMIT_BYTES)


def _sigmoid(x):
    return 1.0 / (1.0 + jnp.exp(-x))


def _softplus(x):
    return jnp.maximum(x, 0.0) + jnp.log1p(jnp.exp(-jnp.abs(x)))


def _rms_rows(x, w):
    ms = jnp.mean(x * x, axis=-1, keepdims=True)
    return x * lax.rsqrt(ms + EPS) * w


def _wdot(xn, w_ref, w_t):
    w = w_ref[...].astype(BF16)
    if w_t:
        return lax.dot_general(xn, w, NT_DIMS, preferred_element_type=F32)
    return jnp.dot(xn, w, preferred_element_type=F32)


def _gdn_gate_outputs(acc, alog_ref, dtb_ref, beta_ref, g_ref):
    hv = beta_ref.shape[1]
    beta_ref[...] = _sigmoid(acc[:, :hv])
    g_ref[...] = -jnp.exp(alog_ref[...]) * _softplus(acc[:, hv:2 * hv] + dtb_ref[...])


def _fox_gate_outputs(acc, bf_ref, lf_ref):
    f = acc[:, :lf_ref.shape[1]] + bf_ref[...]
    lf_ref[...] = -_softplus(-f)


GATES = {"gdn": (_gdn_gate_outputs, 2, 2), "fox": (_fox_gate_outputs, 1, 1)}


def _norm_proj_kernel(x_ref, nw_ref, w_ref, *rest, relu2, w_t, gate):
    if gate is None:
        o_ref, xn_ref = rest
    else:
        epilogue, n_par, n_out = GATES[gate]
        gw_ref, par_refs = rest[0], rest[1:1 + n_par]
        o_ref, gate_refs, xn_ref = rest[1 + n_par], rest[2 + n_par:2 + n_par + n_out], rest[-1]

    def tile(xn, w16):
        if w_t:
            acc = lax.dot_general(xn, w16, NT_DIMS, preferred_element_type=F32)
        else:
            acc = jnp.dot(xn, w16, preferred_element_type=F32)
        if relu2:
            acc = jnp.square(jnp.maximum(acc, 0.0))
        return acc.astype(o_ref.dtype)

    j = pl.program_id(1)

    @pl.when(j == 0)
    def _():
        tm = x_ref.shape[0]
        rows = _pick(tm, NORM_CHUNK_ROWS, BF16_ROWS)
        w16 = w_ref[...].astype(BF16)
        for r in range(tm // rows):
            sl = slice(r * rows, (r + 1) * rows)
            xn = _rms_rows(x_ref[sl, :], nw_ref[...]).astype(BF16)
            xn_ref[sl, :] = xn
            o_ref[sl, :] = tile(xn, w16)
        if gate is not None:
            epilogue(_wdot(xn_ref[...], gw_ref, True), *par_refs, *gate_refs)

    @pl.when(j > 0)
    def _():
        o_ref[...] = tile(xn_ref[...], w_ref[...].astype(BF16))


def norm_proj(x, nw, w_all, layer, n_cols, *, relu2=False, w_t=False, gate=None, gate_params=(),
              name="norm_proj"):
    M, K = x.shape
    tm = _pick(M, ROW_TILE, BF16_ROWS)
    tn = _pick(n_cols, IN_COL_TILE, LANES)
    if w_t:
        w_spec = pl.BlockSpec((None, tn, K), lambda i, j: (layer, j, 0))
    else:
        w_spec = pl.BlockSpec((None, K, tn), lambda i, j: (layer, 0, j))
    in_specs = [pl.BlockSpec((tm, K), lambda i, j: (i, 0)), pl.BlockSpec((1, K), lambda i, j: (0, 0)), w_spec]
    args = [x, nw.reshape(1, K), w_all]
    out_specs = [pl.BlockSpec((tm, tn), lambda i, j: (i, j))]
    out_shape = [jax.ShapeDtypeStruct((M, n_cols), BF16)]
    if gate is not None:
        assert w_t and n_cols % LANES == 0
        H = gate_params[0].shape[0]
        assert GATES[gate][2] * H <= LANES
        in_specs.append(pl.BlockSpec((None, LANES, K), lambda i, j: (layer, n_cols // LANES, 0)))
        args.append(w_all)
        for p in gate_params:
            in_specs.append(pl.BlockSpec((1, H), lambda i, j: (0, 0)))
            args.append(p.reshape(1, H))
        for _ in range(GATES[gate][2]):
            out_specs.append(pl.BlockSpec((tm, H), lambda i, j: (i, 0)))
            out_shape.append(jax.ShapeDtypeStruct((M, H), F32))
    out = pl.pallas_call(
        functools.partial(_norm_proj_kernel, relu2=relu2, w_t=w_t, gate=gate),
        grid=(M // tm, n_cols // tn),
        in_specs=in_specs,
        out_specs=out_specs,
        out_shape=out_shape,
        scratch_shapes=[pltpu.VMEM((tm, K), BF16)],
        compiler_params=_params("parallel", "arbitrary"),
        name=name,
    )(*args)
    return out[0] if gate is None else out


def _matmul_res_kernel(a_ref, w_ref, r_ref, o_ref, *acc, nk):
    part = jnp.dot(a_ref[...], w_ref[...].astype(BF16), preferred_element_type=F32)
    if nk == 1:
        o_ref[...] = r_ref[...] + part
        return
    (acc_ref,) = acc
    k = pl.program_id(1)
    j = pl.program_id(2)

    @pl.when(k == 0)
    def _():
        acc_ref[j] = part

    if nk > 2:
        @pl.when(jnp.logical_and(k > 0, k < nk - 1))
        def _():
            acc_ref[j] += part

    @pl.when(k == nk - 1)
    def _():
        o_ref[...] = r_ref[...] + (acc_ref[j] + part)


def matmul_res(a, w_all, layer, res, *, name="matmul_res"):
    M, K = a.shape
    N = w_all.shape[-1]
    tm = _pick(M, ROW_TILE, BF16_ROWS)
    tn = _pick(N, OUT_COL_TILE, LANES)
    tk = _pick(K, K_TILE, LANES)
    nk = K // tk
    out_idx = lambda i, k, j: (i, jnp.where(k == nk - 1, j, 0))
    return pl.pallas_call(
        functools.partial(_matmul_res_kernel, nk=nk),
        grid=(M // tm, nk, N // tn),
        in_specs=[pl.BlockSpec((tm, tk), lambda i, k, j: (i, k)),
                  pl.BlockSpec((None, tk, tn), lambda i, k, j: (layer, k, j)),
                  pl.BlockSpec((tm, tn), out_idx)],
        out_specs=pl.BlockSpec((tm, tn), out_idx),
        out_shape=jax.ShapeDtypeStruct((M, N), F32),
        scratch_shapes=[pltpu.VMEM((N // tn, tm, tn), F32)] if nk > 1 else [],
        compiler_params=_params("parallel", "arbitrary", "arbitrary"),
        name=name,
    )(a, w_all, res)


def _rmsnorm_kernel(x_ref, nw_ref, o_ref):
    o_ref[...] = _rms_rows(x_ref[...], nw_ref[...])


def rmsnorm_rows(x, nw, row0, n_rows):
    K = x.shape[1]
    tm = _pick(n_rows, NORM_ROW_TILE, SUBLANES)
    rb0 = row0 // tm
    return pl.pallas_call(
        _rmsnorm_kernel,
        grid=(n_rows // tm,),
        in_specs=[pl.BlockSpec((tm, K), lambda i: (rb0 + i, 0)), pl.BlockSpec((1, K), lambda i: (0, 0))],
        out_specs=pl.BlockSpec((tm, K), lambda i: (i, 0)),
        out_shape=jax.ShapeDtypeStruct((n_rows, K), F32),
        compiler_params=_params("parallel"),
        name="final_rmsnorm",
    )(x, nw.reshape(1, K))


def _gdn_chunk_kernel(x_ref, xp_ref, cw_ref, z_ref, g_ref, gt_ref, bt_ref, nw_ref, *rest,
                      C, hq, rep, dk, dv, n_chunks, has_s0, has_prev):
    rest = list(rest)
    buf_ref = rest.pop(0) if has_s0 else None
    s0_ref = rest.pop(0) if has_s0 else None
    if has_prev:
        rest.pop(0)
    o_ref, sout_ref, s_ref = rest
    n = pl.program_id(2)
    C2 = 2 * C
    cw = cw_ref.shape[0]
    qk_dim = hq * dk
    pair = 2 * dk
    assert dv == dk and cw - 1 <= CONV_HALO_ROWS <= C

    @pl.when(n == 0)
    def _():
        if has_s0:
            s_ref[...] = s0_ref[0]
        else:
            s_ref[...] = jnp.zeros_like(s_ref)

    K = CONV_HALO_ROWS + C
    sr = lax.broadcasted_iota(jnp.int32, ((cw - 1) * C, K), 0)
    sc = lax.broadcasted_iota(jnp.int32, ((cw - 1) * C, K), 1)
    cshift = C.bit_length() - 1
    sel = jnp.where(sc == CONV_HALO_ROWS + (sr & (C - 1)) - ((sr >> cshift) + 1), 1.0, 0.0).astype(BF16)
    first = n == 0
    trow = lax.broadcasted_iota(jnp.int32, (C, pair), 0)
    conv_cache = {}

    def conv_pair(p):
        if p in conv_cache:
            return conv_cache[p]
        cs = slice(p * pair, (p + 1) * pair)
        x_cur = x_ref[:, cs]
        x_prev = xp_ref[C - CONV_HALO_ROWS:C, cs]
        x_prev = jnp.where(first, jnp.zeros_like(x_prev), x_prev)
        sh = jnp.dot(sel, jnp.concatenate([x_prev, x_cur], axis=0), preferred_element_type=F32)
        acc = x_cur.astype(F32) * cw_ref[cw - 1:cw, cs]
        for s in range(1, cw):
            acc = acc + sh[(s - 1) * C:s * C] * cw_ref[cw - 1 - s:cw - s, cs]
        if has_s0:
            corr = jnp.zeros((C, pair), F32)
            for t in range(cw - 1):
                r = sum(cw_ref[cw - 1 - s:cw - s, cs] * buf_ref[0, cw - 1 - (s - t):cw - (s - t), cs]
                        for s in range(t + 1, cw))
                corr = jnp.where(trow == t, r, corr)
            acc = acc + jnp.where(first, corr, 0.0)
        y = acc * _sigmoid(acc)
        conv_cache[p] = y
        return y

    def conv(c0, l2_scale=None):
        y = conv_pair(c0 // pair)[:, (c0 % pair):(c0 % pair) + dk]
        if l2_scale is None:
            return y
        return y * (lax.rsqrt(jnp.sum(y * y, axis=-1, keepdims=True) + EPS) * l2_scale)

    row = lax.broadcasted_iota(jnp.int32, (C, C2), 0)
    lane = lax.broadcasted_iota(jnp.int32, (C, C2), 1)
    right = lane >= C
    col = jnp.where(right, lane - C, lane)
    incl = row >= col
    strict = row > col
    eye_right = jnp.where(lane == row + C, 1.0, 0.0).astype(F32)
    triu2 = jnp.where(row <= col, 1.0, 0.0).astype(F32)
    tril = jnp.where(lax.broadcasted_iota(jnp.int32, (C, C), 0) >= lax.broadcasted_iota(jnp.int32, (C, C), 1),
                     1.0, 0.0).astype(F32)
    zeros_z = jnp.zeros((C, C2), BF16)
    zeros_r = jnp.zeros((C, dk + dv), BF16)
    G2_all = jnp.dot(g_ref[0, n], triu2, precision=HIGHEST, preferred_element_type=F32)
    Gc_all = jnp.dot(tril, gt_ref[...], precision=HIGHEST, preferred_element_type=F32)
    beta_all = bt_ref[...]

    def head_group(heads):
        qa = sorted({i // rep for i in heads})
        q = {a: conv(a * dk, dk ** -0.5) for a in qa}
        k = {a: conv(qk_dim + a * dk, 1.0) for a in qa}
        q16 = {a: q[a].astype(BF16) for a in qa}
        kk16 = {a: jnp.concatenate([k[a].astype(BF16)] * 2, axis=0) for a in qa}
        G2, G_col, b_col, dec_incl, gamma, G_last, kb, vb = {}, {}, {}, {}, {}, {}, {}, {}
        for i in heads:
            G2[i] = G2_all[i:i + 1, :]
            G_col[i] = Gc_all[:, i:i + 1]
            b_col[i] = beta_all[:, i:i + 1]
            dec_incl[i] = jnp.exp(jnp.where(incl, G_col[i] - G2[i], -jnp.inf))
            gamma[i] = jnp.exp(G_col[i])
            G_last[i] = G2[i][:, C - 1:C]
            kb[i] = k[i // rep] * b_col[i]
            vb[i] = conv(2 * qk_dim + i * dv) * b_col[i]
        gram = {i: lax.dot_general(jnp.concatenate([kb[i].astype(BF16), q16[i // rep]], axis=0), kk16[i // rep],
                                   NT_DIMS, preferred_element_type=F32) for i in heads}
        Lf = {i: gram[i][:C] * dec_incl[i] for i in heads}
        base = min(INV_BASE_BLOCK, C)
        sh = base.bit_length() - 1
        in_base = jnp.logical_and(strict, (row >> sh) == (col >> sh))
        Z = {i: jnp.where(right, eye_right, -jnp.where(in_base, Lf[i], 0.0)) for i in heads}
        for _ in range((base - 1).bit_length()):
            Z16 = {i: Z[i].astype(BF16) for i in heads}
            Z = {i: jnp.dot(Z16[i], jnp.concatenate([Z16[i], zeros_z], axis=0), preferred_element_type=F32)
                 + jnp.where(right, Z[i], 0.0) for i in heads}
        b = base
        while b < C:
            sh = b.bit_length() - 1
            lower_left = jnp.logical_and(jnp.logical_and(right, (row >> (sh + 1)) == (col >> (sh + 1))),
                                         jnp.logical_and(((row >> sh) & 1) == 1, ((col >> sh) & 1) == 0))
            Z16 = {i: Z[i].astype(BF16) for i in heads}
            W16 = {i: jnp.dot(jnp.where(lower_left, Lf[i], 0.0).astype(BF16),
                              jnp.concatenate([zeros_z, Z16[i]], axis=0),
                              preferred_element_type=F32).astype(BF16) for i in heads}
            Z = {i: Z[i] - jnp.dot(Z16[i], jnp.concatenate([zeros_z, W16[i]], axis=0),
                                   preferred_element_type=F32) for i in heads}
            b *= 2
        wu = {}
        for i in heads:
            rhs = jnp.concatenate([(kb[i] * gamma[i]).astype(BF16), vb[i].astype(BF16)], axis=1)
            wu[i] = jnp.dot(Z[i].astype(BF16), jnp.concatenate([zeros_r, rhs], axis=0),
                            preferred_element_type=F32)
        S = {i: s_ref[i] for i in heads}
        S16 = {i: S[i].astype(BF16) for i in heads}
        ws_qs = {i: jnp.dot(jnp.concatenate([wu[i][:, :dk].astype(BF16),
                                             (q[i // rep] * gamma[i]).astype(BF16)], axis=0),
                            S16[i], preferred_element_type=F32) for i in heads}
        U16 = {i: (wu[i][:, dk:] - ws_qs[i][:C]).astype(BF16) for i in heads}
        A16 = {i: (gram[i][C:] * dec_incl[i])[:, :C].astype(BF16) for i in heads}
        o = {i: ws_qs[i][C:] + jnp.dot(A16[i], U16[i], preferred_element_type=F32) for i in heads}
        for i in heads:
            kd = k[i // rep] * jnp.exp(G_last[i] - G_col[i])
            s_ref[i] = jnp.exp(G_last[i]) * S[i] + lax.dot_general(kd.astype(BF16), U16[i], TN_DIMS,
                                                                   preferred_element_type=F32)
        for i in heads:
            z = z_ref[:, i * dv:(i + 1) * dv].astype(F32)
            o_ref[:, i * dv:(i + 1) * dv] = (_rms_rows(o[i], nw_ref[...])
                                             * (z * _sigmoid(z))).astype(o_ref.dtype)

    n_heads = hq * rep
    for h0 in range(0, n_heads, GDN_HEAD_GROUP):
        head_group(range(h0, min(h0 + GDN_HEAD_GROUP, n_heads)))

    @pl.when(n == n_chunks - 1)
    def _():
        sout_ref[0] = s_ref[...]


def gdn_chunks(proj, conv_w_all, buf_all, g, beta, norm_w, s0_all, layer, n_layers, s_prev, *, row0,
               nseq, T, C, hv, hqk, dk, dv, out_rows):
    N = T // C
    cw = conv_w_all.shape[1]
    conv_dim = 2 * hqk * dk + hv * dv
    assert conv_dim % (hv * dv) == 0 and T % C == 0 and C & (C - 1) == 0
    has_s0 = s0_all is not None
    rb0 = row0 // C
    g_rows = g[row0:row0 + nseq * T].reshape(nseq, N, C, hv).transpose(0, 1, 3, 2)
    tok = pl.BlockSpec((C, hv), lambda b, h, n: (rb0 + b * N + n, 0))
    in_specs = [pl.BlockSpec((C, conv_dim), lambda b, h, n: (rb0 + b * N + n, 0)),
                pl.BlockSpec((C, conv_dim), lambda b, h, n: (rb0 + b * N + jnp.maximum(n - 1, 0), 0)),
                pl.BlockSpec((None, cw, conv_dim), lambda b, h, n: (layer, 0, 0)),
                pl.BlockSpec((C, hv * dv), lambda b, h, n: (rb0 + b * N + n, conv_dim // (hv * dv))),
                pl.BlockSpec((1, N, hv, C), lambda b, h, n: (b, 0, 0, 0)),
                tok, tok,
                pl.BlockSpec((1, dv), lambda b, h, n: (0, 0))]
    args = [proj, proj, conv_w_all, proj, g_rows, g, beta, norm_w.reshape(1, dv)]
    if has_s0:
        in_specs += [pl.BlockSpec((None, 1, cw - 1, conv_dim), lambda b, h, n: (layer, b, 0, 0)),
                     pl.BlockSpec((None, 1, hv, dk, dv), lambda b, h, n: (layer, b, 0, 0, 0))]
        args += [buf_all, s0_all]
    aliases = {}
    if s_prev is not None:
        aliases[len(args)] = 1
        in_specs.append(pl.BlockSpec(memory_space=pl.ANY))
        args.append(s_prev)
    return pl.pallas_call(
        functools.partial(_gdn_chunk_kernel, C=C, hq=hqk, rep=hv // hqk, dk=dk, dv=dv, n_chunks=N,
                          has_s0=has_s0, has_prev=s_prev is not None),
        grid=(nseq, 1, N),
        in_specs=in_specs,
        out_specs=[pl.BlockSpec((C, hv * dv), lambda b, h, n: (b * N + n, 0)),
                   pl.BlockSpec((None, 1, hv, dk, dv), lambda b, h, n: (layer, b, 0, 0, 0))],
        out_shape=[jax.ShapeDtypeStruct((out_rows, hv * dv), BF16),
                   jax.ShapeDtypeStruct((n_layers, nseq, hv, dk, dv), F32)],
        input_output_aliases=aliases,
        scratch_shapes=[pltpu.VMEM((hv, dk, dv), F32)],
        compiler_params=_params("parallel", "parallel", "arbitrary"),
        name="gdn_chunks",
    )(*args)


def _cumsum_kernel(x_ref, o_ref, carry_ref, *, tb):
    @pl.when(pl.program_id(0) == 0)
    def _():
        carry_ref[...] = jnp.zeros_like(carry_ref)

    row = lax.broadcasted_iota(jnp.int32, (tb, tb), 0)
    col = lax.broadcasted_iota(jnp.int32, (tb, tb), 1)
    tril = jnp.where(row >= col, 1.0, 0.0).astype(F32)
    c = jnp.dot(tril, x_ref[...], precision=HIGHEST, preferred_element_type=F32) + carry_ref[...]
    o_ref[...] = c
    carry_ref[...] = c[tb - 1:tb, :]


def cumsum_time(x):
    nseq, L, H = x.shape
    S = nseq * H
    tb = _pick(L, TIME_BLOCK, SUBLANES)
    out = pl.pallas_call(
        functools.partial(_cumsum_kernel, tb=tb),
        grid=(L // tb,),
        in_specs=[pl.BlockSpec((tb, S), lambda t: (t, 0))],
        out_specs=pl.BlockSpec((tb, S), lambda t: (t, 0)),
        out_shape=jax.ShapeDtypeStruct((L, S), F32),
        scratch_shapes=[pltpu.VMEM((1, S), F32)],
        compiler_params=_params("arbitrary"),
        name="fox_cumsum",
    )(x.transpose(1, 0, 2).reshape(L, S))
    return out.reshape(L, nseq, H)


def _fox_prompt_kernel(qi_ref, kj_ref, q_ref, k_ref, v_ref, og_ref, cq_ref, ckt_ref, o_ref,
                       acc_ref, m_ref, cqr_ref, q2_ref, *, bq, bk, H, dh, scale, hg):
    i = qi_ref[pl.program_id(1)]
    j = kj_ref[pl.program_id(1)]
    rep = bk // dh
    r = bk // bq
    j_last = i // r

    @pl.when(j == 0)
    def _():
        m_ref[...] = jnp.full(m_ref.shape, -jnp.inf, F32)
        acc_ref[...] = jnp.zeros_like(acc_ref)
        cqs = cq_ref[...] * LOG2E
        for h in range(H):
            cqr_ref[h] = jnp.broadcast_to(cqs[:, h:h + 1], (bq, dh))
        q2_ref[...] = (q_ref[...].astype(F32) * (scale * LOG2E)).astype(BF16)

    def block(masked):
        if masked:
            mask = (lax.broadcasted_iota(jnp.int32, (bq, bk), 1)
                    <= lax.broadcasted_iota(jnp.int32, (bq, bk), 0) + (i - r * j_last) * bq)
        ones16 = jnp.ones((bk, dh), BF16)
        ck2 = ckt_ref[...] * LOG2E

        def qk(h):
            hs = slice(h * dh, (h + 1) * dh)
            return lax.dot_general(q2_ref[:, hs], k_ref[:, hs], NT_DIMS, preferred_element_type=F32)

        groups = [range(g * hg, (g + 1) * hg) for g in range(H // hg)]
        s = {h: qk(h) for h in groups[0]}
        for gi, grp in enumerate(groups):
            if gi + 1 < len(groups):
                for h in groups[gi + 1]:
                    s[h] = qk(h)
            p16, alpha2 = {}, {}
            for h in grp:
                t = s.pop(h) - ck2[h:h + 1, :]
                if masked:
                    t = jnp.where(mask, t, -jnp.inf)
                cq2 = cqr_ref[h]
                m_prev = m_ref[h]
                m_new = jnp.maximum(m_prev, jnp.max(t, axis=1, keepdims=True) + cq2)
                m_ref[h] = m_new
                alpha = jnp.exp2(m_prev - m_new)
                c = cq2 - m_new
                p16[h] = jnp.exp2(t + jnp.concatenate([c] * rep, axis=1)).astype(BF16)
                alpha2[h] = jnp.concatenate([alpha, alpha], axis=1)
            for h in grp:
                hs = slice(h * dh, (h + 1) * dh)
                v_ext = jnp.concatenate([v_ref[:, hs], ones16], axis=1)
                acc_ref[h] = alpha2[h] * acc_ref[h] + jnp.dot(p16[h], v_ext, preferred_element_type=F32)

    @pl.when(j < j_last)
    def _():
        block(False)

    @pl.when(j == j_last)
    def _():
        block(True)
        for h in range(H):
            hs = slice(h * dh, (h + 1) * dh)
            a = acc_ref[h]
            o_ref[:, hs] = ((a[:, :dh] / a[:, dh:]) * _sigmoid(og_ref[:, hs].astype(F32))).astype(o_ref.dtype)


def fox_prompt_attention(proj, cum, cum_t, *, nseq, T, H, dh, out_rows):
    bq = _pick(T, TIME_BLOCK, LANES)
    bk = _pick(T, FOX_KEY_BLOCKS * bq, bq)
    nq, nk, r = T // bq, T // bk, bk // bq
    D = H * dh
    hg = _pick(H, FOX_HEAD_GROUP, 1)
    pairs = [(i, j) for i in range(nq) for j in range(i // r + 1)]
    qi = jnp.asarray([p[0] for p in pairs], jnp.int32)
    kj = jnp.asarray([p[1] for p in pairs], jnp.int32)
    qrow = lambda col: (lambda b, p, qi, kj: (b * nq + qi[p], col))
    krow = lambda col: (lambda b, p, qi, kj: (b * nk + kj[p], col))
    return pl.pallas_call(
        functools.partial(_fox_prompt_kernel, bq=bq, bk=bk, H=H, dh=dh, scale=dh ** -0.5, hg=hg),
        grid_spec=pltpu.PrefetchScalarGridSpec(
            num_scalar_prefetch=2,
            grid=(nseq, len(pairs)),
            in_specs=[pl.BlockSpec((bq, D), qrow(0)),
                      pl.BlockSpec((bk, D), krow(1)),
                      pl.BlockSpec((bk, D), krow(2)),
                      pl.BlockSpec((bq, D), qrow(3)),
                      pl.BlockSpec((bq, H), qrow(0)),
                      pl.BlockSpec((H, bk), lambda b, p, qi, kj: (0, b * nk + kj[p]))],
            out_specs=pl.BlockSpec((bq, D), qrow(0)),
            scratch_shapes=[pltpu.VMEM((H, bq, 2 * dh), F32), pltpu.VMEM((H, bq, dh), F32),
                            pltpu.VMEM((H, bq, dh), F32), pltpu.VMEM((bq, D), BF16)]),
        out_shape=jax.ShapeDtypeStruct((out_rows, D), BF16),
        compiler_params=_params("parallel", "arbitrary"),
        name="fox_prompt_attention",
    )(qi, kj, proj, proj, proj, proj, cum, cum_t)


def _fox_sample_kernel(q_ref, kn_ref, vn_ref, og_ref, kc_ref, vc_ref, cq_ref, ck_ref, mb_ref, o_ref,
                       *, P, T, G, dh, scale):
    PG = P * G
    for g in range(kc_ref.shape[2] // G):
        hs = slice(g * G, (g + 1) * G)
        q = (q_ref[0, g].astype(F32) * scale).astype(BF16)
        kc = kc_ref[0, :, hs, :].reshape(PG, dh).astype(BF16)
        vc = vc_ref[0, :, hs, :].reshape(PG, dh).astype(BF16)
        s_p = lax.dot_general(q, kc, NT_DIMS, preferred_element_type=F32)
        s_n = lax.dot_general(q, kn_ref[0, g], NT_DIMS, preferred_element_type=F32)
        cq = cq_ref[0, g]
        ck = ck_ref[0, g]
        t_p = (s_p - ck[:, :PG]) + mb_ref[:, :PG]
        t_n = (s_n - ck[:, PG:]) + mb_ref[:, PG:]
        m = jnp.maximum(jnp.max(t_p, axis=1, keepdims=True), jnp.max(t_n, axis=1, keepdims=True)) + cq
        c = cq - m
        p_p = jnp.exp(t_p + c)
        p_n = jnp.exp(t_n + c)
        l = jnp.sum(p_p, axis=1, keepdims=True) + jnp.sum(p_n, axis=1, keepdims=True)
        o = (jnp.dot(p_p.astype(BF16), vc, preferred_element_type=F32)
             + jnp.dot(p_n.astype(BF16), vn_ref[0, g], preferred_element_type=F32))
        o_ref[0, g] = ((o / l) * _sigmoid(og_ref[0, g].astype(F32))).astype(o_ref.dtype)


def fox_sample_attention(proj, k_cache_all, v_cache_all, layer, cum, *, row0, nseq, T, P, H, dh):
    G = SUBLANES if H % SUBLANES == 0 else H
    assert G & (G - 1) == 0
    NG = H // G
    R = T * G
    new = proj[row0:row0 + nseq * T].reshape(nseq, T, 4, NG, G, dh).transpose(2, 0, 3, 1, 4, 5)
    new = new.reshape(4, nseq, NG, R, dh)
    ck = cum.reshape(nseq, P + T, NG, G).transpose(0, 2, 1, 3).reshape(nseq, NG, 1, (P + T) * G)
    cq = cum[:, P:].reshape(nseq, T, NG, G).transpose(0, 2, 1, 3).reshape(nseq, NG, R, 1)
    rr = jnp.arange(R, dtype=jnp.int32)[:, None]
    cc = jnp.arange((P + T) * G, dtype=jnp.int32)[None, :]
    visible = jnp.logical_and(rr % G == cc % G, cc // G <= P + rr // G)
    mask_bias = jnp.where(visible, 0.0, -jnp.inf).astype(F32)
    part = lambda a: pl.BlockSpec((None, 1, NG, R, dh), lambda b: (a, b, 0, 0, 0))
    cache = pl.BlockSpec((None, 1, P, H, dh), lambda b: (layer, b, 0, 0, 0))
    out = pl.pallas_call(
        functools.partial(_fox_sample_kernel, P=P, T=T, G=G, dh=dh, scale=dh ** -0.5),
        grid=(nseq,),
        in_specs=[part(0), part(1), part(2), part(3), cache, cache,
                  pl.BlockSpec((1, NG, R, 1), lambda b: (b, 0, 0, 0)),
                  pl.BlockSpec((1, NG, 1, (P + T) * G), lambda b: (b, 0, 0, 0)),
                  pl.BlockSpec((R, (P + T) * G), lambda b: (0, 0))],
        out_specs=pl.BlockSpec((1, NG, R, dh), lambda b: (b, 0, 0, 0)),
        out_shape=jax.ShapeDtypeStruct((nseq, NG, R, dh), BF16),
        compiler_params=_params("parallel"),
        name="fox_sample_attention",
    )(new, new, new, new, k_cache_all, v_cache_all, cq, ck, mask_bias)
    return out.reshape(nseq, NG, T, G, dh).transpose(0, 2, 1, 3, 4).reshape(nseq * T, H * dh)


def kernel(x_prompt, x_sample, state_gdn, state_gdn_conv, cache_fox_k, cache_fox_v, cache_fox_logf,
           norm_mix, norm_mlp, norm_final, gdn_w_in, gdn_conv_w, gdn_a_log, gdn_dt_bias, gdn_norm_w,
           gdn_w_out, fox_w_in, fox_b_f, fox_w_out, mlp_w_up, mlp_w_down):
    Bp, Tp, D = x_prompt.shape
    Bs, Ts, _ = x_sample.shape
    depth = norm_mix.shape[0]
    _, _, hv, dk, dv = state_gdn.shape
    conv_dim = state_gdn_conv.shape[-1]
    cw1 = state_gdn_conv.shape[2]
    v_dim = hv * dv
    qk_dim = (conv_dim - v_dim) // 2
    hqk = qk_dim // dk
    _, _, P, H, dh = cache_fox_k.shape
    fox_dim = H * dh
    d_ff = mlp_w_up.shape[-1]
    Mp, Ms = Bp * Tp, Bs * Ts
    Cp = min(GDN_CHUNK, Tp)

    x = jnp.concatenate([x_prompt.reshape(Mp, D), x_sample.reshape(Ms, D)], axis=0)
    gdn_w_in_t = jnp.swapaxes(gdn_w_in, 1, 2)
    fox_w_in_t = jnp.swapaxes(fox_w_in, 1, 2)
    S_p, S_s, buf_p, buf_s = None, None, [], []
    n_gdn = state_gdn.shape[0]
    k_p, k_s, v_p, v_s, lf_p, lf_s = [], [], [], [], [], []

    for i in range(depth):
        j = i // 2
        if i % 2 == 0:
            n_main = conv_dim + v_dim
            proj, beta, g = norm_proj(x, norm_mix[i], gdn_w_in_t, j, n_main, w_t=True, gate="gdn",
                                      gate_params=(gdn_a_log[j], gdn_dt_bias[j]), name="gdn_in_proj")

            common = dict(hv=hv, hqk=hqk, dk=dk, dv=dv)
            o, S_p = gdn_chunks(proj, gdn_conv_w, None, g, beta, gdn_norm_w[j], None, j, n_gdn, S_p,
                                row0=0, nseq=Bp, T=Tp, C=Cp, out_rows=Mp + Ms, **common)
            o_s, S_s = gdn_chunks(proj, gdn_conv_w, state_gdn_conv, g, beta, gdn_norm_w[j], state_gdn, j,
                                  n_gdn, S_s, row0=Mp, nseq=Bs, T=Ts, C=Ts, out_rows=Ms, **common)
            o = lax.dynamic_update_slice(o, o_s, (Mp, 0))
            x = matmul_res(o, gdn_w_out, j, x, name="gdn_out_proj")
            buf_p.append(jnp.stack([proj[(b + 1) * Tp - cw1:(b + 1) * Tp, :conv_dim]
                                    for b in range(Bp)]).astype(F32))
            qkv_s = proj[Mp:, :conv_dim].reshape(Bs, Ts, conv_dim).astype(F32)
            buf_s.append(jnp.concatenate([state_gdn_conv[j], qkv_s], axis=1)[:, -cw1:])
        else:
            proj, logf = norm_proj(x, norm_mix[i], fox_w_in_t, j, 4 * fox_dim, w_t=True, gate="fox",
                                   gate_params=(fox_b_f[j],), name="fox_in_proj")
            cum_p = cumsum_time(logf[:Mp].reshape(Bp, Tp, H))
            o = fox_prompt_attention(proj, cum_p.transpose(1, 0, 2).reshape(Mp, H),
                                     cum_p.transpose(2, 1, 0).reshape(H, Mp), nseq=Bp, T=Tp, H=H, dh=dh,
                                     out_rows=Mp + Ms)
            lf_new = logf[Mp:].reshape(Bs, Ts, H)
            lf_all = jnp.concatenate([cache_fox_logf[j], lf_new], axis=1)
            cum_s = cumsum_time(lf_all).transpose(1, 0, 2)
            o_s = fox_sample_attention(proj, cache_fox_k, cache_fox_v, j, cum_s,
                                       row0=Mp, nseq=Bs, T=Ts, P=P, H=H, dh=dh)
            o = lax.dynamic_update_slice(o, o_s, (Mp, 0))
            x = matmul_res(o, fox_w_out, j, x, name="fox_out_proj")
            kv = proj[:, fox_dim:3 * fox_dim].reshape(Mp + Ms, 2, H, dh)
            k_p.append(kv[:Mp, 0].astype(F32).reshape(Bp, Tp, H, dh))
            k_s.append(kv[Mp:, 0].astype(F32).reshape(Bs, Ts, H, dh))
            v_p.append(kv[:Mp, 1].astype(F32).reshape(Bp, Tp, H, dh))
            v_s.append(kv[Mp:, 1].astype(F32).reshape(Bs, Ts, H, dh))
            lf_p.append(logf[:Mp].reshape(Bp, Tp, H))
            lf_s.append(lf_new)
        hmid = norm_proj(x, norm_mlp[i], mlp_w_up, i, d_ff, relu2=True, name="mlp_up")
        x = matmul_res(hmid, mlp_w_down, i, x, name="mlp_down")

    y_p = rmsnorm_rows(x, norm_final, 0, Mp)
    y_s = rmsnorm_rows(x, norm_final, Mp, Ms)
    st = jnp.stack
    return (y_p.reshape(Bp, Tp, D), y_s.reshape(Bs, Ts, D),
            S_p, st(buf_p), st(k_p), st(v_p), st(lf_p),
            S_s, st(buf_s), st(k_s), st(v_s), st(lf_s))
```

```python
import functools

import jax
import jax.numpy as jnp
from jax import lax
from jax.experimental import pallas as pl
from jax.experimental.pallas import tpu as pltpu

F32 = jnp.float32
BF16 = jnp.bfloat16
EPS = 1e-6
GDN_CHUNK = 64
FOX_KEY_BLOCKS = 1
GDN_BLOCKS_PER_STEP = 2
GDN_HEAD_GROUP = 32
CONV_HALO_ROWS = 16
INV_BASE_BLOCK = 8
VMEM_LIMIT_BYTES = 60 * 1024 * 1024
LOG2E = 1.4426950408889634
SUBLANES = 8
BF16_ROWS = 16
LANES = 128
ROW_TILE = 1088
IN_COL_TILE = 1024
OUT_COL_TILE = 512
K_TILE = 4096
NORM_CHUNK_ROWS = 272
NORM_ROW_TILE = 512
TIME_BLOCK = 256
FOX_HEAD_GROUP = 2
HIGHEST = lax.Precision.HIGHEST
NT_DIMS = (((1,), (1,)), ((), ()))
TN_DIMS = (((0,), (0,)), ((), ()))


def _pick(n, pref, align):
    best = None
    for d in range(align, min(n, pref) + 1, align):
        if n % d == 0:
            best = d
    return n if best is None else best


def _params(*sem):
    return pltpu.CompilerParams(dimension_semantics=sem, vmem_limit_bytes=VMEM_LIMIT_BYTES)


def _sigmoid(x):
    return 1.0 / (1.0 + jnp.exp(-x))


def _softplus(x):
    return jnp.maximum(x, 0.0) + jnp.log1p(jnp.exp(-jnp.abs(x)))


def _rms_rows(x, w):
    ms = jnp.mean(x * x, axis=-1, keepdims=True)
    return x * lax.rsqrt(ms + EPS) * w


def _wdot(xn, w_ref, w_t):
    w = w_ref[...].astype(BF16)
    if w_t:
        return lax.dot_general(xn, w, NT_DIMS, preferred_element_type=F32)
    return jnp.dot(xn, w, preferred_element_type=F32)


def _gdn_gate_outputs(acc, alog_ref, dtb_ref, beta_ref, g_ref):
    hv = beta_ref.shape[1]
    beta_ref[...] = _sigmoid(acc[:, :hv])
    g_ref[...] = -jnp.exp(alog_ref[...]) * _softplus(acc[:, hv:2 * hv] + dtb_ref[...])


def _fox_gate_outputs(acc, bf_ref, lf_ref):
    f = acc[:, :lf_ref.shape[1]] + bf_ref[...]
    lf_ref[...] = -_softplus(-f)


GATES = {"gdn": (_gdn_gate_outputs, 2, 2), "fox": (_fox_gate_outputs, 1, 1)}


def _norm_proj_kernel(x_ref, nw_ref, w_ref, *rest, relu2, w_t, gate):
    if gate is None:
        o_ref, xn_ref = rest
    else:
        epilogue, n_par, n_out = GATES[gate]
        gw_ref, par_refs = rest[0], rest[1:1 + n_par]
        o_ref, gate_refs, xn_ref = rest[1 + n_par], rest[2 + n_par:2 + n_par + n_out], rest[-1]

    def tile(xn, w16):
        if w_t:
            acc = lax.dot_general(xn, w16, NT_DIMS, preferred_element_type=F32)
        else:
            acc = jnp.dot(xn, w16, preferred_element_type=F32)
        if relu2:
            acc = jnp.square(jnp.maximum(acc, 0.0))
        return acc.astype(o_ref.dtype)

    j = pl.program_id(1)

    @pl.when(j == 0)
    def _():
        tm = x_ref.shape[0]
        rows = _pick(tm, NORM_CHUNK_ROWS, BF16_ROWS)
        w16 = w_ref[...].astype(BF16)
        for r in range(tm // rows):
            sl = slice(r * rows, (r + 1) * rows)
            xn = _rms_rows(x_ref[sl, :], nw_ref[...]).astype(BF16)
            xn_ref[sl, :] = xn
            o_ref[sl, :] = tile(xn, w16)
        if gate is not None:
            epilogue(_wdot(xn_ref[...], gw_ref, True), *par_refs, *gate_refs)

    @pl.when(j > 0)
    def _():
        o_ref[...] = tile(xn_ref[...], w_ref[...].astype(BF16))


def norm_proj(x, nw, w_all, layer, n_cols, *, relu2=False, w_t=False, gate=None, gate_params=(),
              name="norm_proj"):
    M, K = x.shape
    tm = _pick(M, ROW_TILE, BF16_ROWS)
    tn = _pick(n_cols, IN_COL_TILE, LANES)
    if w_t:
        w_spec = pl.BlockSpec((None, tn, K), lambda i, j: (layer, j, 0))
    else:
        w_spec = pl.BlockSpec((None, K, tn), lambda i, j: (layer, 0, j))
    in_specs = [pl.BlockSpec((tm, K), lambda i, j: (i, 0)), pl.BlockSpec((1, K), lambda i, j: (0, 0)), w_spec]
    args = [x, nw.reshape(1, K), w_all]
    out_specs = [pl.BlockSpec((tm, tn), lambda i, j: (i, j))]
    out_shape = [jax.ShapeDtypeStruct((M, n_cols), BF16)]
    if gate is not None:
        assert w_t and n_cols % LANES == 0
        H = gate_params[0].shape[0]
        assert GATES[gate][2] * H <= LANES
        in_specs.append(pl.BlockSpec((None, LANES, K), lambda i, j: (layer, n_cols // LANES, 0)))
        args.append(w_all)
        for p in gate_params:
            in_specs.append(pl.BlockSpec((1, H), lambda i, j: (0, 0)))
            args.append(p.reshape(1, H))
        for _ in range(GATES[gate][2]):
            out_specs.append(pl.BlockSpec((tm, H), lambda i, j: (i, 0)))
            out_shape.append(jax.ShapeDtypeStruct((M, H), F32))
    out = pl.pallas_call(
        functools.partial(_norm_proj_kernel, relu2=relu2, w_t=w_t, gate=gate),
        grid=(M // tm, n_cols // tn),
        in_specs=in_specs,
        out_specs=out_specs,
        out_shape=out_shape,
        scratch_shapes=[pltpu.VMEM((tm, K), BF16)],
        compiler_params=_params("parallel", "arbitrary"),
        name=name,
    )(*args)
    return out[0] if gate is None else out


def _matmul_res_kernel(a_ref, w_ref, r_ref, o_ref, *acc, nk):
    part = jnp.dot(a_ref[...], w_ref[...].astype(BF16), preferred_element_type=F32)
    if nk == 1:
        o_ref[...] = r_ref[...] + part
        return
    (acc_ref,) = acc
    k = pl.program_id(1)
    j = pl.program_id(2)

    @pl.when(k == 0)
    def _():
        acc_ref[j] = part

    if nk > 2:
        @pl.when(jnp.logical_and(k > 0, k < nk - 1))
        def _():
            acc_ref[j] += part

    @pl.when(k == nk - 1)
    def _():
        o_ref[...] = r_ref[...] + (acc_ref[j] + part)


def matmul_res(a, w_all, layer, res, *, name="matmul_res"):
    M, K = a.shape
    N = w_all.shape[-1]
    tm = _pick(M, ROW_TILE, BF16_ROWS)
    tn = _pick(N, OUT_COL_TILE, LANES)
    tk = _pick(K, K_TILE, LANES)
    nk = K // tk
    out_idx = lambda i, k, j: (i, jnp.where(k == nk - 1, j, 0))
    return pl.pallas_call(
        functools.partial(_matmul_res_kernel, nk=nk),
        grid=(M // tm, nk, N // tn),
        in_specs=[pl.BlockSpec((tm, tk), lambda i, k, j: (i, k)),
                  pl.BlockSpec((None, tk, tn), lambda i, k, j: (layer, k, j)),
                  pl.BlockSpec((tm, tn), out_idx)],
        out_specs=pl.BlockSpec((tm, tn), out_idx),
        out_shape=jax.ShapeDtypeStruct((M, N), F32),
        scratch_shapes=[pltpu.VMEM((N // tn, tm, tn), F32)] if nk > 1 else [],
        compiler_params=_params("parallel", "arbitrary", "arbitrary"),
        name=name,
    )(a, w_all, res)


def _rmsnorm_kernel(x_ref, nw_ref, o_ref):
    o_ref[...] = _rms_rows(x_ref[...], nw_ref[...])


def rmsnorm_rows(x, nw, row0, n_rows):
    K = x.shape[1]
    tm = _pick(n_rows, NORM_ROW_TILE, SUBLANES)
    rb0 = row0 // tm
    return pl.pallas_call(
        _rmsnorm_kernel,
        grid=(n_rows // tm,),
        in_specs=[pl.BlockSpec((tm, K), lambda i: (rb0 + i, 0)), pl.BlockSpec((1, K), lambda i: (0, 0))],
        out_specs=pl.BlockSpec((tm, K), lambda i: (i, 0)),
        out_shape=jax.ShapeDtypeStruct((n_rows, K), F32),
        compiler_params=_params("parallel"),
        name="final_rmsnorm",
    )(x, nw.reshape(1, K))


def _gdn_chunk_kernel(x_ref, xp_ref, cw_ref, z_ref, g_ref, gt_ref, bt_ref, nw_ref, *rest,
                      C, sub, hq, rep, dk, dv, n_steps, has_s0, has_prev):
    rest = list(rest)
    buf_ref = rest.pop(0) if has_s0 else None
    s0_ref = rest.pop(0) if has_s0 else None
    if has_prev:
        rest.pop(0)
    o_ref, sout_ref, s_ref = rest
    n = pl.program_id(2)
    C2 = 2 * C
    cw = cw_ref.shape[0]
    qk_dim = hq * dk
    pair = 2 * dk
    assert dv == dk and cw - 1 <= CONV_HALO_ROWS <= C

    @pl.when(n == 0)
    def _():
        if has_s0:
            s_ref[...] = s0_ref[0]
        else:
            s_ref[...] = jnp.zeros_like(s_ref)

    K = CONV_HALO_ROWS + C
    sr = lax.broadcasted_iota(jnp.int32, ((cw - 1) * C, K), 0)
    sc = lax.broadcasted_iota(jnp.int32, ((cw - 1) * C, K), 1)
    cshift = C.bit_length() - 1
    sel = jnp.where(sc == CONV_HALO_ROWS + (sr & (C - 1)) - ((sr >> cshift) + 1), 1.0, 0.0).astype(BF16)
    first = n == 0
    trow = lax.broadcasted_iota(jnp.int32, (C, pair), 0)
    row = lax.broadcasted_iota(jnp.int32, (C, C2), 0)
    lane = lax.broadcasted_iota(jnp.int32, (C, C2), 1)
    right = lane >= C
    col = jnp.where(right, lane - C, lane)
    incl = row >= col
    strict = row > col
    eye_right = jnp.where(lane == row + C, 1.0, 0.0).astype(F32)
    triu2 = jnp.where(row <= col, 1.0, 0.0).astype(F32)
    tril = jnp.where(lax.broadcasted_iota(jnp.int32, (C, C), 0) >= lax.broadcasted_iota(jnp.int32, (C, C), 1),
                     1.0, 0.0).astype(F32)
    zeros_z = jnp.zeros((C, C2), BF16)
    zeros_r = jnp.zeros((C, dk + dv), BF16)

    def chunk(u):
        r0 = u * C
        conv_cache = {}

        def conv_pair(p):
            if p in conv_cache:
                return conv_cache[p]
            cs = slice(p * pair, (p + 1) * pair)
            x_cur = x_ref[r0:r0 + C, cs]
            if u == 0:
                x_prev = xp_ref[sub * C - CONV_HALO_ROWS:sub * C, cs]
                x_prev = jnp.where(first, jnp.zeros_like(x_prev), x_prev)
            else:
                x_prev = x_ref[r0 - CONV_HALO_ROWS:r0, cs]
            sh = jnp.dot(sel, jnp.concatenate([x_prev, x_cur], axis=0), preferred_element_type=F32)
            acc = x_cur.astype(F32) * cw_ref[cw - 1:cw, cs]
            for s in range(1, cw):
                acc = acc + sh[(s - 1) * C:s * C] * cw_ref[cw - 1 - s:cw - s, cs]
            if has_s0 and u == 0:
                corr = jnp.zeros((C, pair), F32)
                for t in range(cw - 1):
                    r = sum(cw_ref[cw - 1 - s:cw - s, cs] * buf_ref[0, cw - 1 - (s - t):cw - (s - t), cs]
                            for s in range(t + 1, cw))
                    corr = jnp.where(trow == t, r, corr)
                acc = acc + jnp.where(first, corr, 0.0)
            y = acc * _sigmoid(acc)
            conv_cache[p] = y
            return y

        def conv(c0, l2_scale=None):
            y = conv_pair(c0 // pair)[:, (c0 % pair):(c0 % pair) + dk]
            if l2_scale is None:
                return y
            return y * (lax.rsqrt(jnp.sum(y * y, axis=-1, keepdims=True) + EPS) * l2_scale)

        G2_all = jnp.dot(g_ref[0, n * sub + u], triu2, precision=HIGHEST, preferred_element_type=F32)
        Gc_all = jnp.dot(tril, gt_ref[r0:r0 + C, :], precision=HIGHEST, preferred_element_type=F32)
        beta_all = bt_ref[r0:r0 + C, :]
        head_group(u, conv, G2_all, Gc_all, beta_all)

    def head_group_of(u, conv, G2_all, Gc_all, beta_all, heads):
        r0 = u * C
        qa = sorted({i // rep for i in heads})
        q = {a: conv(a * dk, dk ** -0.5) for a in qa}
        k = {a: conv(qk_dim + a * dk, 1.0) for a in qa}
        q16 = {a: q[a].astype(BF16) for a in qa}
        kk16 = {a: jnp.concatenate([k[a].astype(BF16)] * 2, axis=0) for a in qa}
        G2, G_col, b_col, dec_incl, gamma, G_last, kb, vb = {}, {}, {}, {}, {}, {}, {}, {}
        for i in heads:
            G2[i] = G2_all[i:i + 1, :]
            G_col[i] = Gc_all[:, i:i + 1]
            b_col[i] = beta_all[:, i:i + 1]
            dec_incl[i] = jnp.exp(jnp.where(incl, G_col[i] - G2[i], -jnp.inf))
            gamma[i] = jnp.exp(G_col[i])
            G_last[i] = G2[i][:, C - 1:C]
            kb[i] = k[i // rep] * b_col[i]
            vb[i] = conv(2 * qk_dim + i * dv) * b_col[i]
        gram = {i: lax.dot_general(jnp.concatenate([kb[i].astype(BF16), q16[i // rep]], axis=0), kk16[i // rep],
                                   NT_DIMS, preferred_element_type=F32) for i in heads}
        Lf = {i: gram[i][:C] * dec_incl[i] for i in heads}
        base = min(INV_BASE_BLOCK, C)
        sh = base.bit_length() - 1
        in_base = jnp.logical_and(strict, (row >> sh) == (col >> sh))
        Z = {i: jnp.where(right, eye_right, -jnp.where(in_base, Lf[i], 0.0)) for i in heads}
        for _ in range((base - 1).bit_length()):
            Z16 = {i: Z[i].astype(BF16) for i in heads}
            Z = {i: jnp.dot(Z16[i], jnp.concatenate([Z16[i], zeros_z], axis=0), preferred_element_type=F32)
                 + jnp.where(right, Z[i], 0.0) for i in heads}
        b = base
        while b < C:
            sh = b.bit_length() - 1
            lower_left = jnp.logical_and(jnp.logical_and(right, (row >> (sh + 1)) == (col >> (sh + 1))),
                                         jnp.logical_and(((row >> sh) & 1) == 1, ((col >> sh) & 1) == 0))
            Z16 = {i: Z[i].astype(BF16) for i in heads}
            W16 = {i: jnp.dot(jnp.where(lower_left, Lf[i], 0.0).astype(BF16),
                              jnp.concatenate([zeros_z, Z16[i]], axis=0),
                              preferred_element_type=F32).astype(BF16) for i in heads}
            Z = {i: Z[i] - jnp.dot(Z16[i], jnp.concatenate([zeros_z, W16[i]], axis=0),
                                   preferred_element_type=F32) for i in heads}
            b *= 2
        wu = {}
        for i in heads:
            rhs = jnp.concatenate([(kb[i] * gamma[i]).astype(BF16), vb[i].astype(BF16)], axis=1)
            wu[i] = jnp.dot(Z[i].astype(BF16), jnp.concatenate([zeros_r, rhs], axis=0),
                            preferred_element_type=F32)
        S = {i: s_ref[i] for i in heads}
        S16 = {i: S[i].astype(BF16) for i in heads}
        ws_qs = {i: jnp.dot(jnp.concatenate([wu[i][:, :dk].astype(BF16),
                                             (q[i // rep] * gamma[i]).astype(BF16)], axis=0),
                            S16[i], preferred_element_type=F32) for i in heads}
        U16 = {i: (wu[i][:, dk:] - ws_qs[i][:C]).astype(BF16) for i in heads}
        A16 = {i: (gram[i][C:] * dec_incl[i])[:, :C].astype(BF16) for i in heads}
        o = {i: ws_qs[i][C:] + jnp.dot(A16[i], U16[i], preferred_element_type=F32) for i in heads}
        for i in heads:
            kd = k[i // rep] * jnp.exp(G_last[i] - G_col[i])
            s_ref[i] = jnp.exp(G_last[i]) * S[i] + lax.dot_general(kd.astype(BF16), U16[i], TN_DIMS,
                                                                   preferred_element_type=F32)
        for i in heads:
            z = z_ref[r0:r0 + C, i * dv:(i + 1) * dv].astype(F32)
            o_ref[r0:r0 + C, i * dv:(i + 1) * dv] = (_rms_rows(o[i], nw_ref[...])
                                             * (z * _sigmoid(z))).astype(o_ref.dtype)

    def head_group(u, conv, G2_all, Gc_all, beta_all):
        n_heads = hq * rep
        for h0 in range(0, n_heads, GDN_HEAD_GROUP):
            head_group_of(u, conv, G2_all, Gc_all, beta_all, range(h0, min(h0 + GDN_HEAD_GROUP, n_heads)))

    for u in range(sub):
        chunk(u)

    @pl.when(n == n_steps - 1)
    def _():
        sout_ref[0] = s_ref[...]


def gdn_chunks(proj, conv_w_all, buf_all, g, beta, norm_w, s0_all, layer, n_layers, s_prev, *, row0,
               nseq, T, C, hv, hqk, dk, dv, out_rows):
    sub = GDN_BLOCKS_PER_STEP if (T // C) % GDN_BLOCKS_PER_STEP == 0 else 1
    N = T // (C * sub)
    R = C * sub
    cw = conv_w_all.shape[1]
    conv_dim = 2 * hqk * dk + hv * dv
    assert conv_dim % (hv * dv) == 0 and T % C == 0 and C & (C - 1) == 0
    has_s0 = s0_all is not None
    rb0 = row0 // R
    g_rows = g[row0:row0 + nseq * T].reshape(nseq, T // C, C, hv).transpose(0, 1, 3, 2)
    tok = pl.BlockSpec((R, hv), lambda b, h, n: (rb0 + b * N + n, 0))
    in_specs = [pl.BlockSpec((R, conv_dim), lambda b, h, n: (rb0 + b * N + n, 0)),
                pl.BlockSpec((R, conv_dim), lambda b, h, n: (rb0 + b * N + jnp.maximum(n - 1, 0), 0)),
                pl.BlockSpec((None, cw, conv_dim), lambda b, h, n: (layer, 0, 0)),
                pl.BlockSpec((R, hv * dv), lambda b, h, n: (rb0 + b * N + n, conv_dim // (hv * dv))),
                pl.BlockSpec((1, T // C, hv, C), lambda b, h, n: (b, 0, 0, 0)),
                tok, tok,
                pl.BlockSpec((1, dv), lambda b, h, n: (0, 0))]
    args = [proj, proj, conv_w_all, proj, g_rows, g, beta, norm_w.reshape(1, dv)]
    if has_s0:
        in_specs += [pl.BlockSpec((None, 1, cw - 1, conv_dim), lambda b, h, n: (layer, b, 0, 0)),
                     pl.BlockSpec((None, 1, hv, dk, dv), lambda b, h, n: (layer, b, 0, 0, 0))]
        args += [buf_all, s0_all]
    aliases = {}
    if s_prev is not None:
        aliases[len(args)] = 1
        in_specs.append(pl.BlockSpec(memory_space=pl.ANY))
        args.append(s_prev)
    return pl.pallas_call(
        functools.partial(_gdn_chunk_kernel, C=C, sub=sub, hq=hqk, rep=hv // hqk, dk=dk, dv=dv, n_steps=N,
                          has_s0=has_s0, has_prev=s_prev is not None),
        grid=(nseq, 1, N),
        in_specs=in_specs,
        out_specs=[pl.BlockSpec((R, hv * dv), lambda b, h, n: (b * N + n, 0)),
                   pl.BlockSpec((None, 1, hv, dk, dv), lambda b, h, n: (layer, b, 0, 0, 0))],
        out_shape=[jax.ShapeDtypeStruct((out_rows, hv * dv), BF16),
                   jax.ShapeDtypeStruct((n_layers, nseq, hv, dk, dv), F32)],
        input_output_aliases=aliases,
        scratch_shapes=[pltpu.VMEM((hv, dk, dv), F32)],
        compiler_params=_params("parallel", "parallel", "arbitrary"),
        name="gdn_chunks",
    )(*args)


def _cumsum_kernel(x_ref, o_ref, carry_ref, *, tb):
    @pl.when(pl.program_id(0) == 0)
    def _():
        carry_ref[...] = jnp.zeros_like(carry_ref)

    row = lax.broadcasted_iota(jnp.int32, (tb, tb), 0)
    col = lax.broadcasted_iota(jnp.int32, (tb, tb), 1)
    tril = jnp.where(row >= col, 1.0, 0.0).astype(F32)
    c = jnp.dot(tril, x_ref[...], precision=HIGHEST, preferred_element_type=F32) + carry_ref[...]
    o_ref[...] = c
    carry_ref[...] = c[tb - 1:tb, :]


def cumsum_time(x):
    nseq, L, H = x.shape
    S = nseq * H
    tb = _pick(L, TIME_BLOCK, SUBLANES)
    out = pl.pallas_call(
        functools.partial(_cumsum_kernel, tb=tb),
        grid=(L // tb,),
        in_specs=[pl.BlockSpec((tb, S), lambda t: (t, 0))],
        out_specs=pl.BlockSpec((tb, S), lambda t: (t, 0)),
        out_shape=jax.ShapeDtypeStruct((L, S), F32),
        scratch_shapes=[pltpu.VMEM((1, S), F32)],
        compiler_params=_params("arbitrary"),
        name="fox_cumsum",
    )(x.transpose(1, 0, 2).reshape(L, S))
    return out.reshape(L, nseq, H)


def _fox_prompt_kernel(qi_ref, kj_ref, q_ref, k_ref, v_ref, og_ref, cq_ref, ckt_ref, o_ref,
                       acc_ref, m_ref, cqr_ref, q2_ref, *, bq, bk, H, dh, scale, hg):
    i = qi_ref[pl.program_id(1)]
    j = kj_ref[pl.program_id(1)]
    rep = bk // dh
    r = bk // bq
    j_last = i // r

    @pl.when(j == 0)
    def _():
        m_ref[...] = jnp.full(m_ref.shape, -jnp.inf, F32)
        acc_ref[...] = jnp.zeros_like(acc_ref)
        cqs = cq_ref[...] * LOG2E
        for h in range(H):
            cqr_ref[h] = jnp.broadcast_to(cqs[:, h:h + 1], (bq, dh))
        q2_ref[...] = (q_ref[...].astype(F32) * (scale * LOG2E)).astype(BF16)

    def block(masked):
        if masked:
            mask = (lax.broadcasted_iota(jnp.int32, (bq, bk), 1)
                    <= lax.broadcasted_iota(jnp.int32, (bq, bk), 0) + (i - r * j_last) * bq)
        ones16 = jnp.ones((bk, dh), BF16)
        ck2 = ckt_ref[...] * LOG2E

        def qk(h):
            hs = slice(h * dh, (h + 1) * dh)
            return lax.dot_general(q2_ref[:, hs], k_ref[:, hs], NT_DIMS, preferred_element_type=F32)

        groups = [range(g * hg, (g + 1) * hg) for g in range(H // hg)]
        s = {h: qk(h) for h in groups[0]}
        for gi, grp in enumerate(groups):
            if gi + 1 < len(groups):
                for h in groups[gi + 1]:
                    s[h] = qk(h)
            p16, alpha2 = {}, {}
            for h in grp:
                t = s.pop(h) - ck2[h:h + 1, :]
                if masked:
                    t = jnp.where(mask, t, -jnp.inf)
                cq2 = cqr_ref[h]
                m_prev = m_ref[h]
                m_new = jnp.maximum(m_prev, jnp.max(t, axis=1, keepdims=True) + cq2)
                m_ref[h] = m_new
                alpha = jnp.exp2(m_prev - m_new)
                c = cq2 - m_new
                p16[h] = jnp.exp2(t + jnp.concatenate([c] * rep, axis=1)).astype(BF16)
                alpha2[h] = jnp.concatenate([alpha, alpha], axis=1)
            for h in grp:
                hs = slice(h * dh, (h + 1) * dh)
                v_ext = jnp.concatenate([v_ref[:, hs], ones16], axis=1)
                acc_ref[h] = alpha2[h] * acc_ref[h] + jnp.dot(p16[h], v_ext, preferred_element_type=F32)

    @pl.when(j < j_last)
    def _():
        block(False)

    @pl.when(j == j_last)
    def _():
        block(True)
        for h in range(H):
            hs = slice(h * dh, (h + 1) * dh)
            a = acc_ref[h]
            o_ref[:, hs] = ((a[:, :dh] / a[:, dh:]) * _sigmoid(og_ref[:, hs].astype(F32))).astype(o_ref.dtype)


def fox_prompt_attention(proj, cum, cum_t, *, nseq, T, H, dh, out_rows):
    bq = _pick(T, TIME_BLOCK, LANES)
    bk = _pick(T, FOX_KEY_BLOCKS * bq, bq)
    nq, nk, r = T // bq, T // bk, bk // bq
    D = H * dh
    hg = _pick(H, FOX_HEAD_GROUP, 1)
    pairs = [(i, j) for i in range(nq) for j in range(i // r + 1)]
    qi = jnp.asarray([p[0] for p in pairs], jnp.int32)
    kj = jnp.asarray([p[1] for p in pairs], jnp.int32)
    qrow = lambda col: (lambda b, p, qi, kj: (b * nq + qi[p], col))
    krow = lambda col: (lambda b, p, qi, kj: (b * nk + kj[p], col))
    return pl.pallas_call(
        functools.partial(_fox_prompt_kernel, bq=bq, bk=bk, H=H, dh=dh, scale=dh ** -0.5, hg=hg),
        grid_spec=pltpu.PrefetchScalarGridSpec(
            num_scalar_prefetch=2,
            grid=(nseq, len(pairs)),
            in_specs=[pl.BlockSpec((bq, D), qrow(0)),
                      pl.BlockSpec((bk, D), krow(1)),
                      pl.BlockSpec((bk, D), krow(2)),
                      pl.BlockSpec((bq, D), qrow(3)),
                      pl.BlockSpec((bq, H), qrow(0)),
                      pl.BlockSpec((H, bk), lambda b, p, qi, kj: (0, b * nk + kj[p]))],
            out_specs=pl.BlockSpec((bq, D), qrow(0)),
            scratch_shapes=[pltpu.VMEM((H, bq, 2 * dh), F32), pltpu.VMEM((H, bq, dh), F32),
                            pltpu.VMEM((H, bq, dh), F32), pltpu.VMEM((bq, D), BF16)]),
        out_shape=jax.ShapeDtypeStruct((out_rows, D), BF16),
        compiler_params=_params("parallel", "arbitrary"),
        name="fox_prompt_attention",
    )(qi, kj, proj, proj, proj, proj, cum, cum_t)


def _fox_sample_kernel(q_ref, kn_ref, vn_ref, og_ref, kc_ref, vc_ref, cq_ref, ck_ref, mb_ref, o_ref,
                       *, P, T, G, dh, scale):
    PG = P * G
    for g in range(kc_ref.shape[2] // G):
        hs = slice(g * G, (g + 1) * G)
        q = (q_ref[0, g].astype(F32) * scale).astype(BF16)
        kc = kc_ref[0, :, hs, :].reshape(PG, dh).astype(BF16)
        vc = vc_ref[0, :, hs, :].reshape(PG, dh).astype(BF16)
        s_p = lax.dot_general(q, kc, NT_DIMS, preferred_element_type=F32)
        s_n = lax.dot_general(q, kn_ref[0, g], NT_DIMS, preferred_element_type=F32)
        cq = cq_ref[0, g]
        ck = ck_ref[0, g]
        t_p = (s_p - ck[:, :PG]) + mb_ref[:, :PG]
        t_n = (s_n - ck[:, PG:]) + mb_ref[:, PG:]
        m = jnp.maximum(jnp.max(t_p, axis=1, keepdims=True), jnp.max(t_n, axis=1, keepdims=True)) + cq
        c = cq - m
        p_p = jnp.exp(t_p + c)
        p_n = jnp.exp(t_n + c)
        l = jnp.sum(p_p, axis=1, keepdims=True) + jnp.sum(p_n, axis=1, keepdims=True)
        o = (jnp.dot(p_p.astype(BF16), vc, preferred_element_type=F32)
             + jnp.dot(p_n.astype(BF16), vn_ref[0, g], preferred_element_type=F32))
        o_ref[0, g] = ((o / l) * _sigmoid(og_ref[0, g].astype(F32))).astype(o_ref.dtype)


def fox_sample_attention(proj, k_cache_all, v_cache_all, layer, cum, *, row0, nseq, T, P, H, dh):
    G = SUBLANES if H % SUBLANES == 0 else H
    assert G & (G - 1) == 0
    NG = H // G
    R = T * G
    new = proj[row0:row0 + nseq * T].reshape(nseq, T, 4, NG, G, dh).transpose(2, 0, 3, 1, 4, 5)
    new = new.reshape(4, nseq, NG, R, dh)
    ck = cum.reshape(nseq, P + T, NG, G).transpose(0, 2, 1, 3).reshape(nseq, NG, 1, (P + T) * G)
    cq = cum[:, P:].reshape(nseq, T, NG, G).transpose(0, 2, 1, 3).reshape(nseq, NG, R, 1)
    rr = jnp.arange(R, dtype=jnp.int32)[:, None]
    cc = jnp.arange((P + T) * G, dtype=jnp.int32)[None, :]
    visible = jnp.logical_and(rr % G == cc % G, cc // G <= P + rr // G)
    mask_bias = jnp.where(visible, 0.0, -jnp.inf).astype(F32)
    part = lambda a: pl.BlockSpec((None, 1, NG, R, dh), lambda b: (a, b, 0, 0, 0))
    cache = pl.BlockSpec((None, 1, P, H, dh), lambda b: (layer, b, 0, 0, 0))
    out = pl.pallas_call(
        functools.partial(_fox_sample_kernel, P=P, T=T, G=G, dh=dh, scale=dh ** -0.5),
        grid=(nseq,),
        in_specs=[part(0), part(1), part(2), part(3), cache, cache,
                  pl.BlockSpec((1, NG, R, 1), lambda b: (b, 0, 0, 0)),
                  pl.BlockSpec((1, NG, 1, (P + T) * G), lambda b: (b, 0, 0, 0)),
                  pl.BlockSpec((R, (P + T) * G), lambda b: (0, 0))],
        out_specs=pl.BlockSpec((1, NG, R, dh), lambda b: (b, 0, 0, 0)),
        out_shape=jax.ShapeDtypeStruct((nseq, NG, R, dh), BF16),
        compiler_params=_params("parallel"),
        name="fox_sample_attention",
    )(new, new, new, new, k_cache_all, v_cache_all, cq, ck, mask_bias)
    return out.reshape(nseq, NG, T, G, dh).transpose(0, 2, 1, 3, 4).reshape(nseq * T, H * dh)


def kernel(x_prompt, x_sample, state_gdn, state_gdn_conv, cache_fox_k, cache_fox_v, cache_fox_logf,
           norm_mix, norm_mlp, norm_final, gdn_w_in, gdn_conv_w, gdn_a_log, gdn_dt_bias, gdn_norm_w,
           gdn_w_out, fox_w_in, fox_b_f, fox_w_out, mlp_w_up, mlp_w_down):
    Bp, Tp, D = x_prompt.shape
    Bs, Ts, _ = x_sample.shape
    depth = norm_mix.shape[0]
    _, _, hv, dk, dv = state_gdn.shape
    conv_dim = state_gdn_conv.shape[-1]
    cw1 = state_gdn_conv.shape[2]
    v_dim = hv * dv
    qk_dim = (conv_dim - v_dim) // 2
    hqk = qk_dim // dk
    _, _, P, H, dh = cache_fox_k.shape
    fox_dim = H * dh
    d_ff = mlp_w_up.shape[-1]
    Mp, Ms = Bp * Tp, Bs * Ts
    Cp = min(GDN_CHUNK, Tp)

    x = jnp.concatenate([x_prompt.reshape(Mp, D), x_sample.reshape(Ms, D)], axis=0)
    gdn_w_in_t = jnp.swapaxes(gdn_w_in, 1, 2)
    fox_w_in_t = jnp.swapaxes(fox_w_in, 1, 2)
    S_p, S_s, buf_p, buf_s = None, None, [], []
    n_gdn = state_gdn.shape[0]
    k_p, k_s, v_p, v_s, lf_p, lf_s = [], [], [], [], [], []

    for i in range(depth):
        j = i // 2
        if i % 2 == 0:
            n_main = conv_dim + v_dim
            proj, beta, g = norm_proj(x, norm_mix[i], gdn_w_in_t, j, n_main, w_t=True, gate="gdn",
                                      gate_params=(gdn_a_log[j], gdn_dt_bias[j]), name="gdn_in_proj")

            common = dict(hv=hv, hqk=hqk, dk=dk, dv=dv)
            o, S_p = gdn_chunks(proj, gdn_conv_w, None, g, beta, gdn_norm_w[j], None, j, n_gdn, S_p,
                                row0=0, nseq=Bp, T=Tp, C=Cp, out_rows=Mp + Ms, **common)
            o_s, S_s = gdn_chunks(proj, gdn_conv_w, state_gdn_conv, g, beta, gdn_norm_w[j], state_gdn, j,
                                  n_gdn, S_s, row0=Mp, nseq=Bs, T=Ts, C=Ts, out_rows=Ms, **common)
            o = lax.dynamic_update_slice(o, o_s, (Mp, 0))
            x = matmul_res(o, gdn_w_out, j, x, name="gdn_out_proj")
            buf_p.append(jnp.stack([proj[(b + 1) * Tp - cw1:(b + 1) * Tp, :conv_dim]
                                    for b in range(Bp)]).astype(F32))
            qkv_s = proj[Mp:, :conv_dim].reshape(Bs, Ts, conv_dim).astype(F32)
            buf_s.append(jnp.concatenate([state_gdn_conv[j], qkv_s], axis=1)[:, -cw1:])
        else:
            proj, logf = norm_proj(x, norm_mix[i], fox_w_in_t, j, 4 * fox_dim, w_t=True, gate="fox",
                                   gate_params=(fox_b_f[j],), name="fox_in_proj")
            cum_p = cumsum_time(logf[:Mp].reshape(Bp, Tp, H))
            o = fox_prompt_attention(proj, cum_p.transpose(1, 0, 2).reshape(Mp, H),
                                     cum_p.transpose(2, 1, 0).reshape(H, Mp), nseq=Bp, T=Tp, H=H, dh=dh,
                                     out_rows=Mp + Ms)
            lf_new = logf[Mp:].reshape(Bs, Ts, H)
            lf_all = jnp.concatenate([cache_fox_logf[j], lf_new], axis=1)
            cum_s = cumsum_time(lf_all).transpose(1, 0, 2)
            o_s = fox_sample_attention(proj, cache_fox_k, cache_fox_v, j, cum_s,
                                       row0=Mp, nseq=Bs, T=Ts, P=P, H=H, dh=dh)
            o = lax.dynamic_update_slice(o, o_s, (Mp, 0))
            x = matmul_res(o, fox_w_out, j, x, name="fox_out_proj")
            kv = proj[:, fox_dim:3 * fox_dim].reshape(Mp + Ms, 2, H, dh)
            k_p.append(kv[:Mp, 0].astype(F32).reshape(Bp, Tp, H, dh))
            k_s.append(kv[Mp:, 0].astype(F32).reshape(Bs, Ts, H, dh))
            v_p.append(kv[:Mp, 1].astype(F32).reshape(Bp, Tp, H, dh))
            v_s.append(kv[Mp:, 1].astype(F32).reshape(Bs, Ts, H, dh))
            lf_p.append(logf[:Mp].reshape(Bp, Tp, H))
            lf_s.append(lf_new)
        hmid = norm_proj(x, norm_mlp[i], mlp_w_up, i, d_ff, relu2=True, name="mlp_up")
        x = matmul_res(hmid, mlp_w_down, i, x, name="mlp_down")

    y_p = rmsnorm_rows(x, norm_final, 0, Mp)
    y_s = rmsnorm_rows(x, norm_final, Mp, Ms)
    st = jnp.stack
    return (y_p.reshape(Bp, Tp, D), y_s.reshape(Bs, Ts, D),
            S_p, st(buf_p), st(k_p), st(v_p), st(lf_p),
            S_s, st(buf_s), st(k_s), st(v_s), st(lf_s))
```

```python
import functools

import jax
import jax.numpy as jnp
from jax import lax
from jax.experimental import pallas as pl
from jax.experimental.pallas import tpu as pltpu

F32 = jnp.float32
BF16 = jnp.bfloat16
EPS = 1e-6
GDN_CHUNK = 64
FOX_KEY_BLOCKS = 1
GDN_BLOCKS_PER_STEP = 4
GDN_HEAD_GROUP = 32
CONV_HALO_ROWS = 16
INV_BASE_BLOCK = 8
VMEM_LIMIT_BYTES = 60 * 1024 * 1024
LOG2E = 1.4426950408889634
SUBLANES = 8
BF16_ROWS = 16
LANES = 128
ROW_TILE = 1088
IN_COL_TILE = 1024
OUT_COL_TILE = 512
K_TILE = 4096
NORM_CHUNK_ROWS = 272
NORM_ROW_TILE = 512
TIME_BLOCK = 256
FOX_HEAD_GROUP = 2
HIGHEST = lax.Precision.HIGHEST
NT_DIMS = (((1,), (1,)), ((), ()))
TN_DIMS = (((0,), (0,)), ((), ()))


def _pick(n, pref, align):
    best = None
    for d in range(align, min(n, pref) + 1, align):
        if n % d == 0:
            best = d
    return n if best is None else best


def _params(*sem):
    return pltpu.CompilerParams(dimension_semantics=sem, vmem_limit_bytes=VMEM_LIMIT_BYTES)


def _sigmoid(x):
    return 1.0 / (1.0 + jnp.exp(-x))


def _softplus(x):
    return jnp.maximum(x, 0.0) + jnp.log1p(jnp.exp(-jnp.abs(x)))


def _rms_rows(x, w):
    ms = jnp.mean(x * x, axis=-1, keepdims=True)
    return x * lax.rsqrt(ms + EPS) * w


def _wdot(xn, w_ref, w_t):
    w = w_ref[...].astype(BF16)
    if w_t:
        return lax.dot_general(xn, w, NT_DIMS, preferred_element_type=F32)
    return jnp.dot(xn, w, preferred_element_type=F32)


def _gdn_gate_outputs(acc, alog_ref, dtb_ref, beta_ref, g_ref):
    hv = beta_ref.shape[1]
    beta_ref[...] = _sigmoid(acc[:, :hv])
    g_ref[...] = -jnp.exp(alog_ref[...]) * _softplus(acc[:, hv:2 * hv] + dtb_ref[...])


def _fox_gate_outputs(acc, bf_ref, lf_ref):
    f = acc[:, :lf_ref.shape[1]] + bf_ref[...]
    lf_ref[...] = -_softplus(-f)


GATES = {"gdn": (_gdn_gate_outputs, 2, 2), "fox": (_fox_gate_outputs, 1, 1)}


def _norm_proj_kernel(x_ref, nw_ref, w_ref, *rest, relu2, w_t, gate):
    if gate is None:
        o_ref, xn_ref = rest
    else:
        epilogue, n_par, n_out = GATES[gate]
        gw_ref, par_refs = rest[0], rest[1:1 + n_par]
        o_ref, gate_refs, xn_ref = rest[1 + n_par], rest[2 + n_par:2 + n_par + n_out], rest[-1]

    def tile(xn, w16):
        if w_t:
            acc = lax.dot_general(xn, w16, NT_DIMS, preferred_element_type=F32)
        else:
            acc = jnp.dot(xn, w16, preferred_element_type=F32)
        if relu2:
            acc = jnp.square(jnp.maximum(acc, 0.0))
        return acc.astype(o_ref.dtype)

    j = pl.program_id(1)

    @pl.when(j == 0)
    def _():
        tm = x_ref.shape[0]
        rows = _pick(tm, NORM_CHUNK_ROWS, BF16_ROWS)
        w16 = w_ref[...].astype(BF16)
        for r in range(tm // rows):
            sl = slice(r * rows, (r + 1) * rows)
            xn = _rms_rows(x_ref[sl, :], nw_ref[...]).astype(BF16)
            xn_ref[sl, :] = xn
            o_ref[sl, :] = tile(xn, w16)
        if gate is not None:
            epilogue(_wdot(xn_ref[...], gw_ref, True), *par_refs, *gate_refs)

    @pl.when(j > 0)
    def _():
        o_ref[...] = tile(xn_ref[...], w_ref[...].astype(BF16))


def norm_proj(x, nw, w_all, layer, n_cols, *, relu2=False, w_t=False, gate=None, gate_params=(),
              name="norm_proj"):
    M, K = x.shape
    tm = _pick(M, ROW_TILE, BF16_ROWS)
    tn = _pick(n_cols, IN_COL_TILE, LANES)
    if w_t:
        w_spec = pl.BlockSpec((None, tn, K), lambda i, j: (layer, j, 0))
    else:
        w_spec = pl.BlockSpec((None, K, tn), lambda i, j: (layer, 0, j))
    in_specs = [pl.BlockSpec((tm, K), lambda i, j: (i, 0)), pl.BlockSpec((1, K), lambda i, j: (0, 0)), w_spec]
    args = [x, nw.reshape(1, K), w_all]
    out_specs = [pl.BlockSpec((tm, tn), lambda i, j: (i, j))]
    out_shape = [jax.ShapeDtypeStruct((M, n_cols), BF16)]
    if gate is not None:
        assert w_t and n_cols % LANES == 0
        H = gate_params[0].shape[0]
        assert GATES[gate][2] * H <= LANES
        in_specs.append(pl.BlockSpec((None, LANES, K), lambda i, j: (layer, n_cols // LANES, 0)))
        args.append(w_all)
        for p in gate_params:
            in_specs.append(pl.BlockSpec((1, H), lambda i, j: (0, 0)))
            args.append(p.reshape(1, H))
        for _ in range(GATES[gate][2]):
            out_specs.append(pl.BlockSpec((tm, H), lambda i, j: (i, 0)))
            out_shape.append(jax.ShapeDtypeStruct((M, H), F32))
    out = pl.pallas_call(
        functools.partial(_norm_proj_kernel, relu2=relu2, w_t=w_t, gate=gate),
        grid=(M // tm, n_cols // tn),
        in_specs=in_specs,
        out_specs=out_specs,
        out_shape=out_shape,
        scratch_shapes=[pltpu.VMEM((tm, K), BF16)],
        compiler_params=_params("parallel", "arbitrary"),
        name=name,
    )(*args)
    return out[0] if gate is None else out


def _matmul_res_kernel(a_ref, w_ref, r_ref, o_ref, *acc, nk):
    part = jnp.dot(a_ref[...], w_ref[...].astype(BF16), preferred_element_type=F32)
    if nk == 1:
        o_ref[...] = r_ref[...] + part
        return
    (acc_ref,) = acc
    k = pl.program_id(1)
    j = pl.program_id(2)

    @pl.when(k == 0)
    def _():
        acc_ref[j] = part

    if nk > 2:
        @pl.when(jnp.logical_and(k > 0, k < nk - 1))
        def _():
            acc_ref[j] += part

    @pl.when(k == nk - 1)
    def _():
        o_ref[...] = r_ref[...] + (acc_ref[j] + part)


def matmul_res(a, w_all, layer, res, *, name="matmul_res"):
    M, K = a.shape
    N = w_all.shape[-1]
    tm = _pick(M, ROW_TILE, BF16_ROWS)
    tn = _pick(N, OUT_COL_TILE, LANES)
    tk = _pick(K, K_TILE, LANES)
    nk = K // tk
    out_idx = lambda i, k, j: (i, jnp.where(k == nk - 1, j, 0))
    return pl.pallas_call(
        functools.partial(_matmul_res_kernel, nk=nk),
        grid=(M // tm, nk, N // tn),
        in_specs=[pl.BlockSpec((tm, tk), lambda i, k, j: (i, k)),
                  pl.BlockSpec((None, tk, tn), lambda i, k, j: (layer, k, j)),
                  pl.BlockSpec((tm, tn), out_idx)],
        out_specs=pl.BlockSpec((tm, tn), out_idx),
        out_shape=jax.ShapeDtypeStruct((M, N), F32),
        scratch_shapes=[pltpu.VMEM((N // tn, tm, tn), F32)] if nk > 1 else [],
        compiler_params=_params("parallel", "arbitrary", "arbitrary"),
        name=name,
    )(a, w_all, res)


def _rmsnorm_kernel(x_ref, nw_ref, o_ref):
    o_ref[...] = _rms_rows(x_ref[...], nw_ref[...])


def rmsnorm_rows(x, nw, row0, n_rows):
    K = x.shape[1]
    tm = _pick(n_rows, NORM_ROW_TILE, SUBLANES)
    rb0 = row0 // tm
    return pl.pallas_call(
        _rmsnorm_kernel,
        grid=(n_rows // tm,),
        in_specs=[pl.BlockSpec((tm, K), lambda i: (rb0 + i, 0)), pl.BlockSpec((1, K), lambda i: (0, 0))],
        out_specs=pl.BlockSpec((tm, K), lambda i: (i, 0)),
        out_shape=jax.ShapeDtypeStruct((n_rows, K), F32),
        compiler_params=_params("parallel"),
        name="final_rmsnorm",
    )(x, nw.reshape(1, K))


def _gdn_chunk_kernel(x_ref, xp_ref, cw_ref, z_ref, g_ref, gt_ref, bt_ref, nw_ref, *rest,
                      C, sub, hq, rep, dk, dv, n_steps, has_s0, has_prev):
    rest = list(rest)
    buf_ref = rest.pop(0) if has_s0 else None
    s0_ref = rest.pop(0) if has_s0 else None
    if has_prev:
        rest.pop(0)
    o_ref, sout_ref, s_ref = rest
    n = pl.program_id(2)
    C2 = 2 * C
    cw = cw_ref.shape[0]
    qk_dim = hq * dk
    pair = 2 * dk
    assert dv == dk and cw - 1 <= CONV_HALO_ROWS <= C

    @pl.when(n == 0)
    def _():
        if has_s0:
            s_ref[...] = s0_ref[0]
        else:
            s_ref[...] = jnp.zeros_like(s_ref)

    K = CONV_HALO_ROWS + C
    sr = lax.broadcasted_iota(jnp.int32, ((cw - 1) * C, K), 0)
    sc = lax.broadcasted_iota(jnp.int32, ((cw - 1) * C, K), 1)
    cshift = C.bit_length() - 1
    sel = jnp.where(sc == CONV_HALO_ROWS + (sr & (C - 1)) - ((sr >> cshift) + 1), 1.0, 0.0).astype(BF16)
    first = n == 0
    trow = lax.broadcasted_iota(jnp.int32, (C, pair), 0)
    row = lax.broadcasted_iota(jnp.int32, (C, C2), 0)
    lane = lax.broadcasted_iota(jnp.int32, (C, C2), 1)
    right = lane >= C
    col = jnp.where(right, lane - C, lane)
    incl = row >= col
    strict = row > col
    eye_right = jnp.where(lane == row + C, 1.0, 0.0).astype(F32)
    triu2 = jnp.where(row <= col, 1.0, 0.0).astype(F32)
    tril = jnp.where(lax.broadcasted_iota(jnp.int32, (C, C), 0) >= lax.broadcasted_iota(jnp.int32, (C, C), 1),
                     1.0, 0.0).astype(F32)
    zeros_z = jnp.zeros((C, C2), BF16)
    zeros_r = jnp.zeros((C, dk + dv), BF16)

    def chunk(u):
        r0 = u * C
        conv_cache = {}

        def conv_pair(p):
            if p in conv_cache:
                return conv_cache[p]
            cs = slice(p * pair, (p + 1) * pair)
            x_cur = x_ref[r0:r0 + C, cs]
            if u == 0:
                x_prev = xp_ref[sub * C - CONV_HALO_ROWS:sub * C, cs]
                x_prev = jnp.where(first, jnp.zeros_like(x_prev), x_prev)
            else:
                x_prev = x_ref[r0 - CONV_HALO_ROWS:r0, cs]
            sh = jnp.dot(sel, jnp.concatenate([x_prev, x_cur], axis=0), preferred_element_type=F32)
            acc = x_cur.astype(F32) * cw_ref[cw - 1:cw, cs]
            for s in range(1, cw):
                acc = acc + sh[(s - 1) * C:s * C] * cw_ref[cw - 1 - s:cw - s, cs]
            if has_s0 and u == 0:
                corr = jnp.zeros((C, pair), F32)
                for t in range(cw - 1):
                    r = sum(cw_ref[cw - 1 - s:cw - s, cs] * buf_ref[0, cw - 1 - (s - t):cw - (s - t), cs]
                            for s in range(t + 1, cw))
                    corr = jnp.where(trow == t, r, corr)
                acc = acc + jnp.where(first, corr, 0.0)
            y = acc * _sigmoid(acc)
            conv_cache[p] = y
            return y

        def conv(c0, l2_scale=None):
            y = conv_pair(c0 // pair)[:, (c0 % pair):(c0 % pair) + dk]
            if l2_scale is None:
                return y
            return y * (lax.rsqrt(jnp.sum(y * y, axis=-1, keepdims=True) + EPS) * l2_scale)

        G2_all = jnp.dot(g_ref[0, n * sub + u], triu2, precision=HIGHEST, preferred_element_type=F32)
        Gc_all = jnp.dot(tril, gt_ref[r0:r0 + C, :], precision=HIGHEST, preferred_element_type=F32)
        beta_all = bt_ref[r0:r0 + C, :]
        head_group(u, conv, G2_all, Gc_all, beta_all)

    def head_group_of(u, conv, G2_all, Gc_all, beta_all, heads):
        r0 = u * C
        qa = sorted({i // rep for i in heads})
        q = {a: conv(a * dk, dk ** -0.5) for a in qa}
        k = {a: conv(qk_dim + a * dk, 1.0) for a in qa}
        q16 = {a: q[a].astype(BF16) for a in qa}
        kk16 = {a: jnp.concatenate([k[a].astype(BF16)] * 2, axis=0) for a in qa}
        G2, G_col, b_col, dec_incl, gamma, G_last, kb, vb = {}, {}, {}, {}, {}, {}, {}, {}
        for i in heads:
            G2[i] = G2_all[i:i + 1, :]
            G_col[i] = Gc_all[:, i:i + 1]
            b_col[i] = beta_all[:, i:i + 1]
            dec_incl[i] = jnp.exp(jnp.where(incl, G_col[i] - G2[i], -jnp.inf))
            gamma[i] = jnp.exp(G_col[i])
            G_last[i] = G2[i][:, C - 1:C]
            kb[i] = k[i // rep] * b_col[i]
            vb[i] = conv(2 * qk_dim + i * dv) * b_col[i]
        gram = {i: lax.dot_general(jnp.concatenate([kb[i].astype(BF16), q16[i // rep]], axis=0), kk16[i // rep],
                                   NT_DIMS, preferred_element_type=F32) for i in heads}
        Lf = {i: gram[i][:C] * dec_incl[i] for i in heads}
        base = min(INV_BASE_BLOCK, C)
        sh = base.bit_length() - 1
        in_base = jnp.logical_and(strict, (row >> sh) == (col >> sh))
        Z = {i: jnp.where(right, eye_right, -jnp.where(in_base, Lf[i], 0.0)) for i in heads}
        for _ in range((base - 1).bit_length()):
            Z16 = {i: Z[i].astype(BF16) for i in heads}
            Z = {i: jnp.dot(Z16[i], jnp.concatenate([Z16[i], zeros_z], axis=0), preferred_element_type=F32)
                 + jnp.where(right, Z[i], 0.0) for i in heads}
        b = base
        while b < C:
            sh = b.bit_length() - 1
            lower_left = jnp.logical_and(jnp.logical_and(right, (row >> (sh + 1)) == (col >> (sh + 1))),
                                         jnp.logical_and(((row >> sh) & 1) == 1, ((col >> sh) & 1) == 0))
            Z16 = {i: Z[i].astype(BF16) for i in heads}
            W16 = {i: jnp.dot(jnp.where(lower_left, Lf[i], 0.0).astype(BF16),
                              jnp.concatenate([zeros_z, Z16[i]], axis=0),
                              preferred_element_type=F32).astype(BF16) for i in heads}
            Z = {i: Z[i] - jnp.dot(Z16[i], jnp.concatenate([zeros_z, W16[i]], axis=0),
                                   preferred_element_type=F32) for i in heads}
            b *= 2
        wu = {}
        for i in heads:
            rhs = jnp.concatenate([(kb[i] * gamma[i]).astype(BF16), vb[i].astype(BF16)], axis=1)
            wu[i] = jnp.dot(Z[i].astype(BF16), jnp.concatenate([zeros_r, rhs], axis=0),
                            preferred_element_type=F32)
        S = {i: s_ref[i] for i in heads}
        S16 = {i: S[i].astype(BF16) for i in heads}
        ws_qs = {i: jnp.dot(jnp.concatenate([wu[i][:, :dk].astype(BF16),
                                             (q[i // rep] * gamma[i]).astype(BF16)], axis=0),
                            S16[i], preferred_element_type=F32) for i in heads}
        U16 = {i: (wu[i][:, dk:] - ws_qs[i][:C]).astype(BF16) for i in heads}
        A16 = {i: (gram[i][C:] * dec_incl[i])[:, :C].astype(BF16) for i in heads}
        o = {i: ws_qs[i][C:] + jnp.dot(A16[i], U16[i], preferred_element_type=F32) for i in heads}
        for i in heads:
            kd = k[i // rep] * jnp.exp(G_last[i] - G_col[i])
            s_ref[i] = jnp.exp(G_last[i]) * S[i] + lax.dot_general(kd.astype(BF16), U16[i], TN_DIMS,
                                                                   preferred_element_type=F32)
        for i in heads:
            z = z_ref[r0:r0 + C, i * dv:(i + 1) * dv].astype(F32)
            o_ref[r0:r0 + C, i * dv:(i + 1) * dv] = (_rms_rows(o[i], nw_ref[...])
                                             * (z * _sigmoid(z))).astype(o_ref.dtype)

    def head_group(u, conv, G2_all, Gc_all, beta_all):
        n_heads = hq * rep
        for h0 in range(0, n_heads, GDN_HEAD_GROUP):
            head_group_of(u, conv, G2_all, Gc_all, beta_all, range(h0, min(h0 + GDN_HEAD_GROUP, n_heads)))

    for u in range(sub):
        chunk(u)

    @pl.when(n == n_steps - 1)
    def _():
        sout_ref[0] = s_ref[...]


def gdn_chunks(proj, conv_w_all, buf_all, g, beta, norm_w, s0_all, layer, n_layers, s_prev, *, row0,
               nseq, T, C, hv, hqk, dk, dv, out_rows):
    sub = GDN_BLOCKS_PER_STEP if (T // C) % GDN_BLOCKS_PER_STEP == 0 else 1
    N = T // (C * sub)
    R = C * sub
    cw = conv_w_all.shape[1]
    conv_dim = 2 * hqk * dk + hv * dv
    assert conv_dim % (hv * dv) == 0 and T % C == 0 and C & (C - 1) == 0
    has_s0 = s0_all is not None
    rb0 = row0 // R
    g_rows = g[row0:row0 + nseq * T].reshape(nseq, T // C, C, hv).transpose(0, 1, 3, 2)
    tok = pl.BlockSpec((R, hv), lambda b, h, n: (rb0 + b * N + n, 0))
    in_specs = [pl.BlockSpec((R, conv_dim), lambda b, h, n: (rb0 + b * N + n, 0)),
                pl.BlockSpec((R, conv_dim), lambda b, h, n: (rb0 + b * N + jnp.maximum(n - 1, 0), 0)),
                pl.BlockSpec((None, cw, conv_dim), lambda b, h, n: (layer, 0, 0)),
                pl.BlockSpec((R, hv * dv), lambda b, h, n: (rb0 + b * N + n, conv_dim // (hv * dv))),
                pl.BlockSpec((1, T // C, hv, C), lambda b, h, n: (b, 0, 0, 0)),
                tok, tok,
                pl.BlockSpec((1, dv), lambda b, h, n: (0, 0))]
    args = [proj, proj, conv_w_all, proj, g_rows, g, beta, norm_w.reshape(1, dv)]
    if has_s0:
        in_specs += [pl.BlockSpec((None, 1, cw - 1, conv_dim), lambda b, h, n: (layer, b, 0, 0)),
                     pl.BlockSpec((None, 1, hv, dk, dv), lambda b, h, n: (layer, b, 0, 0, 0))]
        args += [buf_all, s0_all]
    aliases = {}
    if s_prev is not None:
        aliases[len(args)] = 1
        in_specs.append(pl.BlockSpec(memory_space=pl.ANY))
        args.append(s_prev)
    return pl.pallas_call(
        functools.partial(_gdn_chunk_kernel, C=C, sub=sub, hq=hqk, rep=hv // hqk, dk=dk, dv=dv, n_steps=N,
                          has_s0=has_s0, has_prev=s_prev is not None),
        grid=(nseq, 1, N),
        in_specs=in_specs,
        out_specs=[pl.BlockSpec((R, hv * dv), lambda b, h, n: (b * N + n, 0)),
                   pl.BlockSpec((None, 1, hv, dk, dv), lambda b, h, n: (layer, b, 0, 0, 0))],
        out_shape=[jax.ShapeDtypeStruct((out_rows, hv * dv), BF16),
                   jax.ShapeDtypeStruct((n_layers, nseq, hv, dk, dv), F32)],
        input_output_aliases=aliases,
        scratch_shapes=[pltpu.VMEM((hv, dk, dv), F32)],
        compiler_params=_params("parallel", "parallel", "arbitrary"),
        name="gdn_chunks",
    )(*args)


def _cumsum_kernel(x_ref, o_ref, carry_ref, *, tb):
    @pl.when(pl.program_id(0) == 0)
    def _():
        carry_ref[...] = jnp.zeros_like(carry_ref)

    row = lax.broadcasted_iota(jnp.int32, (tb, tb), 0)
    col = lax.broadcasted_iota(jnp.int32, (tb, tb), 1)
    tril = jnp.where(row >= col, 1.0, 0.0).astype(F32)
    c = jnp.dot(tril, x_ref[...], precision=HIGHEST, preferred_element_type=F32) + carry_ref[...]
    o_ref[...] = c
    carry_ref[...] = c[tb - 1:tb, :]


def cumsum_time(x):
    nseq, L, H = x.shape
    S = nseq * H
    tb = _pick(L, TIME_BLOCK, SUBLANES)
    out = pl.pallas_call(
        functools.partial(_cumsum_kernel, tb=tb),
        grid=(L // tb,),
        in_specs=[pl.BlockSpec((tb, S), lambda t: (t, 0))],
        out_specs=pl.BlockSpec((tb, S), lambda t: (t, 0)),
        out_shape=jax.ShapeDtypeStruct((L, S), F32),
        scratch_shapes=[pltpu.VMEM((1, S), F32)],
        compiler_params=_params("arbitrary"),
        name="fox_cumsum",
    )(x.transpose(1, 0, 2).reshape(L, S))
    return out.reshape(L, nseq, H)


def _fox_prompt_kernel(qi_ref, kj_ref, q_ref, k_ref, v_ref, og_ref, cq_ref, ckt_ref, o_ref,
                       acc_ref, m_ref, cqr_ref, q2_ref, *, bq, bk, H, dh, scale, hg):
    i = qi_ref[pl.program_id(1)]
    j = kj_ref[pl.program_id(1)]
    rep = bk // dh
    r = bk // bq
    j_last = i // r

    @pl.when(j == 0)
    def _():
        m_ref[...] = jnp.full(m_ref.shape, -jnp.inf, F32)
        acc_ref[...] = jnp.zeros_like(acc_ref)
        cqs = cq_ref[...] * LOG2E
        for h in range(H):
            cqr_ref[h] = jnp.broadcast_to(cqs[:, h:h + 1], (bq, dh))
        q2_ref[...] = (q_ref[...].astype(F32) * (scale * LOG2E)).astype(BF16)

    def block(masked):
        if masked:
            mask = (lax.broadcasted_iota(jnp.int32, (bq, bk), 1)
                    <= lax.broadcasted_iota(jnp.int32, (bq, bk), 0) + (i - r * j_last) * bq)
        ones16 = jnp.ones((bk, dh), BF16)
        ck2 = ckt_ref[...] * LOG2E

        def qk(h):
            hs = slice(h * dh, (h + 1) * dh)
            return lax.dot_general(q2_ref[:, hs], k_ref[:, hs], NT_DIMS, preferred_element_type=F32)

        groups = [range(g * hg, (g + 1) * hg) for g in range(H // hg)]
        s = {h: qk(h) for h in groups[0]}
        for gi, grp in enumerate(groups):
            if gi + 1 < len(groups):
                for h in groups[gi + 1]:
                    s[h] = qk(h)
            p16, alpha2 = {}, {}
            for h in grp:
                t = s.pop(h) - ck2[h:h + 1, :]
                if masked:
                    t = jnp.where(mask, t, -jnp.inf)
                cq2 = cqr_ref[h]
                m_prev = m_ref[h]
                m_new = jnp.maximum(m_prev, jnp.max(t, axis=1, keepdims=True) + cq2)
                m_ref[h] = m_new
                alpha = jnp.exp2(m_prev - m_new)
                c = cq2 - m_new
                p16[h] = jnp.exp2(t + jnp.concatenate([c] * rep, axis=1)).astype(BF16)
                alpha2[h] = jnp.concatenate([alpha, alpha], axis=1)
            for h in grp:
                hs = slice(h * dh, (h + 1) * dh)
                v_ext = jnp.concatenate([v_ref[:, hs], ones16], axis=1)
                acc_ref[h] = alpha2[h] * acc_ref[h] + jnp.dot(p16[h], v_ext, preferred_element_type=F32)

    @pl.when(j < j_last)
    def _():
        block(False)

    @pl.when(j == j_last)
    def _():
        block(True)
        for h in range(H):
            hs = slice(h * dh, (h + 1) * dh)
            a = acc_ref[h]
            o_ref[:, hs] = ((a[:, :dh] / a[:, dh:]) * _sigmoid(og_ref[:, hs].astype(F32))).astype(o_ref.dtype)


def fox_prompt_attention(proj, cum, cum_t, *, nseq, T, H, dh, out_rows):
    bq = _pick(T, TIME_BLOCK, LANES)
    bk = _pick(T, FOX_KEY_BLOCKS * bq, bq)
    nq, nk, r = T // bq, T // bk, bk // bq
    D = H * dh
    hg = _pick(H, FOX_HEAD_GROUP, 1)
    pairs = [(i, j) for i in range(nq) for j in range(i // r + 1)]
    qi = jnp.asarray([p[0] for p in pairs], jnp.int32)
    kj = jnp.asarray([p[1] for p in pairs], jnp.int32)
    qrow = lambda col: (lambda b, p, qi, kj: (b * nq + qi[p], col))
    krow = lambda col: (lambda b, p, qi, kj: (b * nk + kj[p], col))
    return pl.pallas_call(
        functools.partial(_fox_prompt_kernel, bq=bq, bk=bk, H=H, dh=dh, scale=dh ** -0.5, hg=hg),
        grid_spec=pltpu.PrefetchScalarGridSpec(
            num_scalar_prefetch=2,
            grid=(nseq, len(pairs)),
            in_specs=[pl.BlockSpec((bq, D), qrow(0)),
                      pl.BlockSpec((bk, D), krow(1)),
                      pl.BlockSpec((bk, D), krow(2)),
                      pl.BlockSpec((bq, D), qrow(3)),
                      pl.BlockSpec((bq, H), qrow(0)),
                      pl.BlockSpec((H, bk), lambda b, p, qi, kj: (0, b * nk + kj[p]))],
            out_specs=pl.BlockSpec((bq, D), qrow(0)),
            scratch_shapes=[pltpu.VMEM((H, bq, 2 * dh), F32), pltpu.VMEM((H, bq, dh), F32),
                            pltpu.VMEM((H, bq, dh), F32), pltpu.VMEM((bq, D), BF16)]),
        out_shape=jax.ShapeDtypeStruct((out_rows, D), BF16),
        compiler_params=_params("parallel", "arbitrary"),
        name="fox_prompt_attention",
    )(qi, kj, proj, proj, proj, proj, cum, cum_t)


def _fox_sample_kernel(q_ref, kn_ref, vn_ref, og_ref, kc_ref, vc_ref, cq_ref, ck_ref, mb_ref, o_ref,
                       *, P, T, G, dh, scale):
    PG = P * G
    for g in range(kc_ref.shape[2] // G):
        hs = slice(g * G, (g + 1) * G)
        q = (q_ref[0, g].astype(F32) * scale).astype(BF16)
        kc = kc_ref[0, :, hs, :].reshape(PG, dh).astype(BF16)
        vc = vc_ref[0, :, hs, :].reshape(PG, dh).astype(BF16)
        s_p = lax.dot_general(q, kc, NT_DIMS, preferred_element_type=F32)
        s_n = lax.dot_general(q, kn_ref[0, g], NT_DIMS, preferred_element_type=F32)
        cq = cq_ref[0, g]
        ck = ck_ref[0, g]
        t_p = (s_p - ck[:, :PG]) + mb_ref[:, :PG]
        t_n = (s_n - ck[:, PG:]) + mb_ref[:, PG:]
        m = jnp.maximum(jnp.max(t_p, axis=1, keepdims=True), jnp.max(t_n, axis=1, keepdims=True)) + cq
        c = cq - m
        p_p = jnp.exp(t_p + c)
        p_n = jnp.exp(t_n + c)
        l = jnp.sum(p_p, axis=1, keepdims=True) + jnp.sum(p_n, axis=1, keepdims=True)
        o = (jnp.dot(p_p.astype(BF16), vc, preferred_element_type=F32)
             + jnp.dot(p_n.astype(BF16), vn_ref[0, g], preferred_element_type=F32))
        o_ref[0, g] = ((o / l) * _sigmoid(og_ref[0, g].astype(F32))).astype(o_ref.dtype)


def fox_sample_attention(proj, k_cache_all, v_cache_all, layer, cum, *, row0, nseq, T, P, H, dh):
    G = SUBLANES if H % SUBLANES == 0 else H
    assert G & (G - 1) == 0
    NG = H // G
    R = T * G
    new = proj[row0:row0 + nseq * T].reshape(nseq, T, 4, NG, G, dh).transpose(2, 0, 3, 1, 4, 5)
    new = new.reshape(4, nseq, NG, R, dh)
    ck = cum.reshape(nseq, P + T, NG, G).transpose(0, 2, 1, 3).reshape(nseq, NG, 1, (P + T) * G)
    cq = cum[:, P:].reshape(nseq, T, NG, G).transpose(0, 2, 1, 3).reshape(nseq, NG, R, 1)
    rr = jnp.arange(R, dtype=jnp.int32)[:, None]
    cc = jnp.arange((P + T) * G, dtype=jnp.int32)[None, :]
    visible = jnp.logical_and(rr % G == cc % G, cc // G <= P + rr // G)
    mask_bias = jnp.where(visible, 0.0, -jnp.inf).astype(F32)
    part = lambda a: pl.BlockSpec((None, 1, NG, R, dh), lambda b: (a, b, 0, 0, 0))
    cache = pl.BlockSpec((None, 1, P, H, dh), lambda b: (layer, b, 0, 0, 0))
    out = pl.pallas_call(
        functools.partial(_fox_sample_kernel, P=P, T=T, G=G, dh=dh, scale=dh ** -0.5),
        grid=(nseq,),
        in_specs=[part(0), part(1), part(2), part(3), cache, cache,
                  pl.BlockSpec((1, NG, R, 1), lambda b: (b, 0, 0, 0)),
                  pl.BlockSpec((1, NG, 1, (P + T) * G), lambda b: (b, 0, 0, 0)),
                  pl.BlockSpec((R, (P + T) * G), lambda b: (0, 0))],
        out_specs=pl.BlockSpec((1, NG, R, dh), lambda b: (b, 0, 0, 0)),
        out_shape=jax.ShapeDtypeStruct((nseq, NG, R, dh), BF16),
        compiler_params=_params("parallel"),
        name="fox_sample_attention",
    )(new, new, new, new, k_cache_all, v_cache_all, cq, ck, mask_bias)
    return out.reshape(nseq, NG, T, G, dh).transpose(0, 2, 1, 3, 4).reshape(nseq * T, H * dh)


def kernel(x_prompt, x_sample, state_gdn, state_gdn_conv, cache_fox_k, cache_fox_v, cache_fox_logf,
           norm_mix, norm_mlp, norm_final, gdn_w_in, gdn_conv_w, gdn_a_log, gdn_dt_bias, gdn_norm_w,
           gdn_w_out, fox_w_in, fox_b_f, fox_w_out, mlp_w_up, mlp_w_down):
    Bp, Tp, D = x_prompt.shape
    Bs, Ts, _ = x_sample.shape
    depth = norm_mix.shape[0]
    _, _, hv, dk, dv = state_gdn.shape
    conv_dim = state_gdn_conv.shape[-1]
    cw1 = state_gdn_conv.shape[2]
    v_dim = hv * dv
    qk_dim = (conv_dim - v_dim) // 2
    hqk = qk_dim // dk
    _, _, P, H, dh = cache_fox_k.shape
    fox_dim = H * dh
    d_ff = mlp_w_up.shape[-1]
    Mp, Ms = Bp * Tp, Bs * Ts
    Cp = min(GDN_CHUNK, Tp)

    x = jnp.concatenate([x_prompt.reshape(Mp, D), x_sample.reshape(Ms, D)], axis=0)
    gdn_w_in_t = jnp.swapaxes(gdn_w_in, 1, 2)
    fox_w_in_t = jnp.swapaxes(fox_w_in, 1, 2)
    S_p, S_s, buf_p, buf_s = None, None, [], []
    n_gdn = state_gdn.shape[0]
    k_p, k_s, v_p, v_s, lf_p, lf_s = [], [], [], [], [], []

    for i in range(depth):
        j = i // 2
        if i % 2 == 0:
            n_main = conv_dim + v_dim
            proj, beta, g = norm_proj(x, norm_mix[i], gdn_w_in_t, j, n_main, w_t=True, gate="gdn",
                                      gate_params=(gdn_a_log[j], gdn_dt_bias[j]), name="gdn_in_proj")

            common = dict(hv=hv, hqk=hqk, dk=dk, dv=dv)
            o, S_p = gdn_chunks(proj, gdn_conv_w, None, g, beta, gdn_norm_w[j], None, j, n_gdn, S_p,
                                row0=0, nseq=Bp, T=Tp, C=Cp, out_rows=Mp + Ms, **common)
            o_s, S_s = gdn_chunks(proj, gdn_conv_w, state_gdn_conv, g, beta, gdn_norm_w[j], state_gdn, j,
                                  n_gdn, S_s, row0=Mp, nseq=Bs, T=Ts, C=Ts, out_rows=Ms, **common)
            o = lax.dynamic_update_slice(o, o_s, (Mp, 0))
            x = matmul_res(o, gdn_w_out, j, x, name="gdn_out_proj")
            buf_p.append(jnp.stack([proj[(b + 1) * Tp - cw1:(b + 1) * Tp, :conv_dim]
                                    for b in range(Bp)]).astype(F32))
            qkv_s = proj[Mp:, :conv_dim].reshape(Bs, Ts, conv_dim).astype(F32)
            buf_s.append(jnp.concatenate([state_gdn_conv[j], qkv_s], axis=1)[:, -cw1:])
        else:
            proj, logf = norm_proj(x, norm_mix[i], fox_w_in_t, j, 4 * fox_dim, w_t=True, gate="fox",
                                   gate_params=(fox_b_f[j],), name="fox_in_proj")
            cum_p = cumsum_time(logf[:Mp].reshape(Bp, Tp, H))
            o = fox_prompt_attention(proj, cum_p.transpose(1, 0, 2).reshape(Mp, H),
                                     cum_p.transpose(2, 1, 0).reshape(H, Mp), nseq=Bp, T=Tp, H=H, dh=dh,
                                     out_rows=Mp + Ms)
            lf_new = logf[Mp:].reshape(Bs, Ts, H)
            lf_all = jnp.concatenate([cache_fox_logf[j], lf_new], axis=1)
            cum_s = cumsum_time(lf_all).transpose(1, 0, 2)
            o_s = fox_sample_attention(proj, cache_fox_k, cache_fox_v, j, cum_s,
                                       row0=Mp, nseq=Bs, T=Ts, P=P, H=H, dh=dh)
            o = lax.dynamic_update_slice(o, o_s, (Mp, 0))
            x = matmul_res(o, fox_w_out, j, x, name="fox_out_proj")
            kv = proj[:, fox_dim:3 * fox_dim].reshape(Mp + Ms, 2, H, dh)
            k_p.append(kv[:Mp, 0].astype(F32).reshape(Bp, Tp, H, dh))
            k_s.append(kv[Mp:, 0].astype(F32).reshape(Bs, Ts, H, dh))
            v_p.append(kv[:Mp, 1].astype(F32).reshape(Bp, Tp, H, dh))
            v_s.append(kv[Mp:, 1].astype(F32).reshape(Bs, Ts, H, dh))
            lf_p.append(logf[:Mp].reshape(Bp, Tp, H))
            lf_s.append(lf_new)
        hmid = norm_proj(x, norm_mlp[i], mlp_w_up, i, d_ff, relu2=True, name="mlp_up")
        x = matmul_res(hmid, mlp_w_down, i, x, name="mlp_down")

    y_p = rmsnorm_rows(x, norm_final, 0, Mp)
    y_s = rmsnorm_rows(x, norm_final, Mp, Ms)
    st = jnp.stack
    return (y_p.reshape(Bp, Tp, D), y_s.reshape(Bs, Ts, D),
            S_p, st(buf_p), st(k_p), st(v_p), st(lf_p),
            S_s, st(buf_s), st(k_s), st(v_s), st(lf_s))
```
